```python
import jax, jax.numpy as jnp
from jax import lax
import numpy as np

D_MODEL = 1024
BATCH = 8
SEQ = 2048
DEPTH = 1
DEC_BATCH = 128
DEC_SEQ = 4
PAST_LEN = 16384
PAGE_SIZE = 128

D_RWKV = D_MODEL // 2
HEAD_DIM = 64
N_HEADS = D_RWKV // HEAD_DIM
D_CONV = D_MODEL - D_RWKV
CONV_WIDTH = 31
DECAY_RANK = 64
ICLR_RANK = 64
GATE_RANK = 128
D_FF = ((8 * D_MODEL // 3 + 127) // 128) * 128
N_SHIFT = 3 * D_RWKV + DECAY_RANK + ICLR_RANK + GATE_RANK
D_IN = N_SHIFT + 2 * D_CONV
N_SUB = 3
RMS_EPS = 1e-6
LN_EPS = 1e-5
GN_EPS = 64e-5

kernel_name = "hymba_rwkv7_conformer_macaron_step"


def rms_norm(x, g):
    xf = x.astype(jnp.float32)
    y = xf * lax.rsqrt(jnp.mean(xf * xf, -1, keepdims=True) + RMS_EPS)
    return (y * g.astype(jnp.float32)).astype(x.dtype)


def layer_norm(x, g, b, eps):
    xf = x.astype(jnp.float32)
    mu = jnp.mean(xf, -1, keepdims=True)
    var = jnp.mean(jnp.square(xf - mu), -1, keepdims=True)
    y = (xf - mu) * lax.rsqrt(var + eps) * g.astype(jnp.float32) + b.astype(jnp.float32)
    return y.astype(x.dtype)


def swiglu(h, w_up, w_down):
    gate, up = jnp.split(h @ w_up, 2, axis=-1)
    return (jax.nn.silu(gate) * up) @ w_down


def wkv7_scan(S0, r, decay, k, v, a_vec, b_vec):
    def step(S, inp):
        r_t, w_t, k_t, v_t, a_t, b_t = inp
        sa = jnp.einsum('bhvk,bhk->bhv', S, a_t)
        S = (S * w_t[:, :, None, :] + sa[..., None] * b_t[:, :, None, :]
             + v_t[..., None] * k_t[:, :, None, :])
        return S, jnp.einsum('bhvk,bhk->bhv', S, r_t)
    seq = tuple(jnp.moveaxis(z, 1, 0) for z in (r, decay, k, v, a_vec, b_vec))
    S, y = lax.scan(step, S0, seq)
    return S, jnp.moveaxis(y, 0, 1)


def token_mix(h, shift_prev, conv_prev, wkv_prev, p):
    b, t, _ = h.shape
    f32 = jnp.float32
    proj = h @ p['w_in']
    p_sh, p_cv = proj[..., :N_SHIFT], proj[..., N_SHIFT:]
    prev = jnp.concatenate([shift_prev.astype(proj.dtype), p_sh[:, :-1]], axis=1)
    xs = p_sh + (prev - p_sh) * p['mu_shift']
    new_shift = p_sh[:, -1:]
    o1, o2, o3 = D_RWKV, 2 * D_RWKV, 3 * D_RWKV
    o4, o5 = o3 + DECAY_RANK, o3 + DECAY_RANK + ICLR_RANK
    r, k, v = xs[..., :o1], xs[..., o1:o2], xs[..., o2:o3]
    dw, da, dg = xs[..., o3:o4], xs[..., o4:o5], xs[..., o5:]
    logw = -jax.nn.softplus(-(p['w0'] + jnp.tanh(dw) @ p['w_decay_up']).astype(f32)) - 0.5
    decay = jnp.exp(-jnp.exp(logw))
    iclr = jax.nn.sigmoid((p['a0'] + da @ p['w_iclr_up']).astype(f32))
    gate = jax.nn.sigmoid(dg) @ p['w_gate_up']
    heads = lambda z: z.astype(f32).reshape(b, t, N_HEADS, HEAD_DIM)
    kf = k.astype(f32)
    kk = heads(kf * p['k_k'].astype(f32))
    kk = kk / jnp.maximum(jnp.sqrt(jnp.sum(kk * kk, -1, keepdims=True)), 1e-12)
    kmod = kf * (1.0 + (iclr - 1.0) * p['k_a'].astype(f32))
    rh, kh, vh, wh, ah = heads(r), heads(kmod), heads(v), heads(decay), heads(iclr)
    S, y = wkv7_scan(wkv_prev.astype(f32), rh, wh, kh, vh, -kk, kk * ah)
    mu = jnp.mean(y, -1, keepdims=True)
    var = jnp.mean(jnp.square(y - mu), -1, keepdims=True)
    y = ((y - mu) * lax.rsqrt(var + GN_EPS)).reshape(b, t, D_RWKV)
    y = y * p['ln_x_g'].astype(f32) + p['ln_x_b'].astype(f32)
    bonus = jnp.sum(rh * kh * p['r_k'].astype(f32), -1, keepdims=True) * vh
    y_rwkv = (y + bonus.reshape(b, t, D_RWKV)).astype(h.dtype) * gate
    u_a, u_g = jnp.split(p_cv, 2, axis=-1)
    u = u_a * jax.nn.sigmoid(u_g)
    full = jnp.concatenate([conv_prev.astype(u.dtype), u], axis=1)
    new_conv = full[:, -(CONV_WIDTH - 1):]
    z = lax.conv_general_dilated(full, p['conv_w'].astype(full.dtype)[:, None, :],
                                 window_strides=(1,), padding='VALID',
                                 dimension_numbers=('NWC', 'WIO', 'NWC'),
                                 feature_group_count=D_CONV) + p['conv_b']
    z = jax.nn.silu(layer_norm(z, p['conv_ln_g'], p['conv_ln_b'], LN_EPS))
    mixed = jnp.concatenate([y_rwkv, z], axis=-1) @ p['w_out']
    return mixed, new_shift, new_conv, S.astype(wkv_prev.dtype)


def decoder_layer(x, c, shift_prev, conv_prev, wkv_prev, p):
    bsz = c.shape[0]
    ada = (jax.nn.silu(c) @ p['w_ada'] + p['b_ada']).reshape(bsz, N_SUB, 3, 1, D_MODEL)

    def modulated(x, i):
        return rms_norm(x, p['g_pre'][i]) * (1 + ada[:, i, 1]) + ada[:, i, 0]

    def residual(x, i, out, res_w):
        return x + res_w * ada[:, i, 2] * rms_norm(out, p['g_post'][i])

    x = residual(x, 0, swiglu(modulated(x, 0), p['w_ffn1_up'], p['w_ffn1_down']), 0.5)
    mixed, new_shift, new_conv, new_wkv = token_mix(modulated(x, 1), shift_prev, conv_prev, wkv_prev, p)
    x = residual(x, 1, mixed, 1.0)
    x = residual(x, 2, swiglu(modulated(x, 2), p['w_ffn2_up'], p['w_ffn2_down']), 0.5)
    return x, new_wkv, new_shift, new_conv


def setup_inputs(seed: int = 0) -> dict:
    key = jax.random.key(seed)
    ks = jax.random.split(key, 32)
    nrm = lambda i, shape, s: jax.random.normal(ks[i], shape, jnp.float32) * s
    return {
        'x_prompt': nrm(0, (BATCH, SEQ, D_MODEL), 1.0),
        'x_sample': nrm(1, (DEC_BATCH, DEC_SEQ, D_MODEL), 1.0),
        'c_prompt': nrm(2, (BATCH, D_MODEL), 1.0),
        'c_sample': nrm(3, (DEC_BATCH, D_MODEL), 1.0),
        'state_wkv': nrm(4, (DEC_BATCH, N_HEADS, HEAD_DIM, HEAD_DIM), 0.1),
        'state_shift': nrm(5, (DEC_BATCH, 1, N_SHIFT), 1.0),
        'state_conv': nrm(6, (DEC_BATCH, CONV_WIDTH - 1, D_CONV), 0.5),
        'w_ada': nrm(7, (D_MODEL, N_SUB * 3 * D_MODEL), D_MODEL ** -0.5),
        'b_ada': nrm(8, (N_SUB * 3 * D_MODEL,), 0.1),
        'g_pre': 1.0 + nrm(9, (N_SUB, D_MODEL), 0.05),
        'g_post': 1.0 + nrm(10, (N_SUB, D_MODEL), 0.05),
        'w_ffn1_up': nrm(11, (D_MODEL, 2 * D_FF), D_MODEL ** -0.5),
        'w_ffn1_down': nrm(12, (D_FF, D_MODEL), D_FF ** -0.5),
        'w_in': nrm(13, (D_MODEL, D_IN), D_MODEL ** -0.5),
        'mu_shift': jax.random.uniform(ks[14], (N_SHIFT,), jnp.float32),
        'w0': nrm(15, (D_RWKV,), 0.5) - 0.5,
        'w_decay_up': nrm(16, (DECAY_RANK, D_RWKV), 0.1 * DECAY_RANK ** -0.5),
        'a0': nrm(17, (D_RWKV,), 0.3),
        'w_iclr_up': nrm(18, (ICLR_RANK, D_RWKV), 0.5 * ICLR_RANK ** -0.5),
        'w_gate_up': nrm(19, (GATE_RANK, D_RWKV), GATE_RANK ** -0.5),
        'k_k': 0.85 + nrm(20, (D_RWKV,), 0.05),
        'k_a': 1.0 + nrm(21, (D_RWKV,), 0.05),
        'r_k': nrm(22, (N_HEADS, HEAD_DIM), 0.1),
        'ln_x_g': 1.0 + nrm(23, (D_RWKV,), 0.05),
        'ln_x_b': nrm(24, (D_RWKV,), 0.02),
        'conv_w': nrm(25, (CONV_WIDTH, D_CONV), CONV_WIDTH ** -0.5),
        'conv_b': nrm(26, (D_CONV,), 0.02),
        'conv_ln_g': 1.0 + nrm(27, (D_CONV,), 0.05),
        'conv_ln_b': nrm(28, (D_CONV,), 0.02),
        'w_out': nrm(29, (D_MODEL, D_MODEL), D_MODEL ** -0.5),
        'w_ffn2_up': nrm(30, (D_MODEL, 2 * D_FF), D_MODEL ** -0.5),
        'w_ffn2_down': nrm(31, (D_FF, D_MODEL), D_FF ** -0.5),
    }


def reference(x_prompt, x_sample, c_prompt, c_sample, state_wkv, state_shift, state_conv,
              w_ada, b_ada, g_pre, g_post, w_ffn1_up, w_ffn1_down, w_in, mu_shift, w0,
              w_decay_up, a0, w_iclr_up, w_gate_up, k_k, k_a, r_k, ln_x_g, ln_x_b,
              conv_w, conv_b, conv_ln_g, conv_ln_b, w_out, w_ffn2_up, w_ffn2_down):
    p = dict(w_ada=w_ada, b_ada=b_ada, g_pre=g_pre, g_post=g_post,
             w_ffn1_up=w_ffn1_up, w_ffn1_down=w_ffn1_down, w_in=w_in, mu_shift=mu_shift,
             w0=w0, w_decay_up=w_decay_up, a0=a0, w_iclr_up=w_iclr_up, w_gate_up=w_gate_up,
             k_k=k_k, k_a=k_a, r_k=r_k, ln_x_g=ln_x_g, ln_x_b=ln_x_b, conv_w=conv_w,
             conv_b=conv_b, conv_ln_g=conv_ln_g, conv_ln_b=conv_ln_b, w_out=w_out,
             w_ffn2_up=w_ffn2_up, w_ffn2_down=w_ffn2_down)
    y_p = x_prompt
    wkv_p = jnp.zeros((BATCH, N_HEADS, HEAD_DIM, HEAD_DIM), state_wkv.dtype)
    shift_p = jnp.zeros((BATCH, 1, N_SHIFT), x_prompt.dtype)
    conv_p = jnp.zeros((BATCH, CONV_WIDTH - 1, D_CONV), x_prompt.dtype)
    y_s, wkv_s, shift_s, conv_s = x_sample, state_wkv, state_shift, state_conv
    for _ in range(DEPTH):
        y_p, wkv_p, shift_p, conv_p = decoder_layer(y_p, c_prompt, shift_p, conv_p, wkv_p, p)
        y_s, wkv_s, shift_s, conv_s = decoder_layer(y_s, c_sample, shift_s, conv_s, wkv_s, p)
    return (y_p, y_s, wkv_p, shift_p, conv_p, wkv_s, shift_s, conv_s)
```

```python
import functools

import jax
import jax.numpy as jnp
from jax import lax
from jax.experimental import pallas as pl
from jax.experimental.pallas import tpu as pltpu

F32 = jnp.float32
BF16 = jnp.bfloat16

D_MODEL = 1024
D_RWKV = 512
HEAD = 64
N_HEADS = D_RWKV // HEAD
D_CONV = D_MODEL - D_RWKV
CONV_W = 31
DECAY_RANK = 64
ICLR_RANK = 64
GATE_RANK = 128
D_FF = 2816
N_SHIFT = 3 * D_RWKV + DECAY_RANK + ICLR_RANK + GATE_RANK
D_IN = N_SHIFT + 2 * D_CONV
N_SUB = 3
RMS_EPS = 1e-6
LN_EPS = 1e-5
GN_EPS = 64e-5

FF_CHUNK = 256
WKV_ROWS = 256
HALO = 32
VMEM_LIMIT = 56 * 1024 * 1024

NN = (((1,), (0,)), ((), ()))
NT = (((1,), (1,)), ((), ()))
B_NT = (((2,), (2,)), ((0,), (0,)))
B_TN = (((1,), (1,)), ((0,), (0,)))


def _splits(x, n):
    if x.dtype == BF16:
        return [x]
    parts, rem = [], x
    for i in range(n):
        p = rem.astype(BF16)
        parts.append(p)
        if i + 1 < n:
            rem = rem - p.astype(F32)
    return parts


def _mm(a, b, dn, na=2, nb=2):
    ap, bp = _splits(a, na), _splits(b, nb)
    depth = max(len(ap), len(bp))
    out = None
    for i, x in enumerate(ap):
        for j, y in enumerate(bp):
            if i + j >= depth:
                continue
            t = lax.dot_general(x, y, dn, preferred_element_type=F32)
            out = t if out is None else out + t
    return out


def _rms(x, g):
    return x * lax.rsqrt(jnp.mean(x * x, axis=-1, keepdims=True) + RMS_EPS) * g


def _sigmoid(x):
    return jax.nn.sigmoid(x)


def _head_sum(x, ebd):
    return _mm(x, ebd, NN, na=2, nb=1)


def _ada_body(c_ref, w_ref, b_ref, o_ref):
    c = c_ref[...]
    s = (c * _sigmoid(c)).astype(BF16)
    o_ref[...] = jnp.dot(s, w_ref[...].astype(BF16), preferred_element_type=F32) + b_ref[...]


def _ada_call(c_all, w_ada, b_ada):
    n, tn = c_all.shape[0], 1152
    width = w_ada.shape[1]
    return pl.pallas_call(
        _ada_body,
        grid=(width // tn,),
        in_specs=[pl.BlockSpec((n, D_MODEL), lambda i: (0, 0)),
                  pl.BlockSpec((D_MODEL, tn), lambda i: (0, i)),
                  pl.BlockSpec((1, tn), lambda i: (0, i))],
        out_specs=pl.BlockSpec((n, tn), lambda i: (0, i)),
        out_shape=jax.ShapeDtypeStruct((n, width), F32),
        compiler_params=pltpu.CompilerParams(dimension_semantics=("arbitrary",),
                                             vmem_limit_bytes=VMEM_LIMIT),
        name="ada",
    )(c_all, w_ada, b_ada.reshape(1, width))


def _const_spec(shape):
    nd = len(shape)
    return pl.BlockSpec(shape, lambda b, j: (0,) * nd, pipeline_mode=pl.Buffered(1))


def _tok_spec(tm, width, tiles):
    return pl.BlockSpec((tm, width), lambda b, j: (b * tiles + j, 0))


def _mod_spec(arr, tm, tiles):
    if arr.ndim == 3:
        return pl.BlockSpec((None, 1, D_MODEL), lambda b, j: (b, 0, 0))
    return _tok_spec(tm, D_MODEL, tiles)


def _params():
    return pltpu.CompilerParams(dimension_semantics=("arbitrary", "arbitrary"),
                                vmem_limit_bytes=VMEM_LIMIT)


def _ffn_body(x_ref, sh_ref, sc_ref, gt_ref, gpre_ref, gpost_ref, wup_ref, wdn_ref, o_ref, act_scr, *, res_w):
    x = x_ref[...]
    h = _rms(x, gpre_ref[...]) * (1.0 + sc_ref[...]) + sh_ref[...]
    hb = h.astype(BF16)
    for j in range(D_FF // FF_CHUNK):
        lo = j * FF_CHUNK
        g = jnp.dot(hb, wup_ref[:, lo:lo + FF_CHUNK], preferred_element_type=F32)
        u = jnp.dot(hb, wup_ref[:, D_FF + lo:D_FF + lo + FF_CHUNK], preferred_element_type=F32)
        act_scr[:, lo:lo + FF_CHUNK] = (g * _sigmoid(g) * u).astype(BF16)
    out = jnp.dot(act_scr[...], wdn_ref[...], preferred_element_type=F32)
    o_ref[...] = x + res_w * gt_ref[...] * _rms(out, gpost_ref[...])


def _ffn_call(x, mods, gpre, gpost, wup, wdn, *, nb, tiles, tm, res_w, name):
    sh, sc, gt = mods
    return pl.pallas_call(
        functools.partial(_ffn_body, res_w=res_w),
        grid=(nb, tiles),
        in_specs=[_tok_spec(tm, D_MODEL, tiles),
                  _mod_spec(sh, tm, tiles), _mod_spec(sc, tm, tiles), _mod_spec(gt, tm, tiles),
                  _const_spec((1, D_MODEL)), _const_spec((1, D_MODEL)),
                  _const_spec((D_MODEL, 2 * D_FF)), _const_spec((D_FF, D_MODEL))],
        out_specs=_tok_spec(tm, D_MODEL, tiles),
        out_shape=jax.ShapeDtypeStruct(x.shape, F32),
        scratch_shapes=[pltpu.VMEM((tm, D_FF), BF16)],
        compiler_params=_params(),
        name=name,
    )(x, sh, sc, gt, gpre, gpost, wup, wdn)


def _pre_body(x_ref, sh_ref, sc_ref, gpre_ref, win_ref, mu_ref, w0_ref, wd_ref, a0_ref, wi_ref, wg_ref,
              kk_ref, ka_ref, rk_ref, ebd_ref, shift0_ref,
              r_ref, k_ref, v_ref, a_ref, b_ref, ld_ref, gate_ref, u_ref, bonus_ref, nshift_ref,
              carry_scr, *, tm, stride, head_major):
    x = x_ref[...]
    h = _rms(x, gpre_ref[...]) * (1.0 + sc_ref[...]) + sh_ref[...]
    proj = jnp.dot(h.astype(BF16), win_ref[...], preferred_element_type=F32)
    p_sh = proj[:, :N_SHIFT]
    p_cv = proj[:, N_SHIFT:]

    if stride == 1:
        @pl.when(pl.program_id(1) == 0)
        def _():
            carry_scr[...] = shift0_ref[...]
        row = lax.broadcasted_iota(jnp.int32, (tm, 1), 0)
        prev = jnp.where(row == 0, carry_scr[...], pltpu.roll(p_sh, 1, axis=0))
        carry_scr[...] = p_sh[tm - 1:tm, :]
    else:
        prev = jnp.concatenate([shift0_ref[...], p_sh[:tm - stride, :]], axis=0)
    nshift_ref[...] = p_sh[tm - stride:tm, :]
    xs = p_sh + (prev - p_sh) * mu_ref[...]

    o1, o2, o3 = D_RWKV, 2 * D_RWKV, 3 * D_RWKV
    r, k, v = xs[:, :o1], xs[:, o1:o2], xs[:, o2:o3]
    dwda = xs[:, o3:o3 + DECAY_RANK + ICLR_RANK]
    dg = xs[:, o3 + DECAY_RANK + ICLR_RANK:]

    zdec = -(w0_ref[...] + jnp.dot(jnp.tanh(dwda).astype(BF16), wd_ref[...], preferred_element_type=F32))
    softplus = jnp.maximum(zdec, 0.0) + jnp.log1p(jnp.exp(-jnp.abs(zdec)))
    ld = -jnp.exp(-softplus - 0.5)
    iclr = _sigmoid(a0_ref[...] + jnp.dot(dwda.astype(BF16), wi_ref[...], preferred_element_type=F32))
    gate = jnp.dot(_sigmoid(dg).astype(BF16), wg_ref[...], preferred_element_type=F32)

    ebd = ebd_ref[...]
    kk = k * kk_ref[...]
    kk = kk / jnp.maximum(jnp.sqrt(_head_sum(kk * kk, ebd)), 1e-12)
    kmod = k * (1.0 + (iclr - 1.0) * ka_ref[...])
    bonus = _head_sum(r * kmod * rk_ref[...], ebd) * v

    gate_ref[...] = gate
    bonus_ref[...] = bonus
    u_ref[...] = p_cv[:, :D_CONV] * _sigmoid(p_cv[:, D_CONV:])
    outs = ((r_ref, r), (k_ref, kmod), (v_ref, v), (a_ref, -kk), (b_ref, kk * iclr), (ld_ref, ld))
    for ref, val in outs:
        if head_major:
            for hh in range(N_HEADS):
                ref[hh] = val[:, hh * HEAD:(hh + 1) * HEAD]
        else:
            ref[...] = val


def _pre_call(x1, mods, gpre, w, shift0, *, nb, tiles, tm, stride, head_major, name):
    sh, sc = mods
    n = x1.shape[0]
    if head_major:
        seq = tiles * tm
        hm_spec = pl.BlockSpec((None, N_HEADS, tm, HEAD), lambda b, j: (b, 0, j, 0))
        hm_shape = jax.ShapeDtypeStruct((nb, N_HEADS, seq, HEAD), F32)
    else:
        hm_spec = _tok_spec(tm, D_RWKV, tiles)
        hm_shape = jax.ShapeDtypeStruct((n, D_RWKV), F32)
    ld_spec = _tok_spec(tm, D_RWKV, tiles)
    ld_shape = jax.ShapeDtypeStruct((n, D_RWKV), F32)
    if stride == 1:
        shift_spec = pl.BlockSpec((None, 1, N_SHIFT), lambda b, j: (b, 0, 0))
        nshift_shape = jax.ShapeDtypeStruct((nb, 1, N_SHIFT), F32)
    else:
        shift_spec = pl.BlockSpec((stride, N_SHIFT), lambda b, j: (0, 0))
        nshift_shape = jax.ShapeDtypeStruct((stride, N_SHIFT), F32)
    return pl.pallas_call(
        functools.partial(_pre_body, tm=tm, stride=stride, head_major=head_major),
        grid=(nb, tiles),
        in_specs=[_tok_spec(tm, D_MODEL, tiles), _mod_spec(sh, tm, tiles), _mod_spec(sc, tm, tiles),
                  _const_spec((1, D_MODEL)), _const_spec((D_MODEL, D_IN)), _const_spec((1, N_SHIFT)),
                  _const_spec((1, D_RWKV)), _const_spec((DECAY_RANK + ICLR_RANK, D_RWKV)),
                  _const_spec((1, D_RWKV)), _const_spec((DECAY_RANK + ICLR_RANK, D_RWKV)),
                  _const_spec((GATE_RANK, D_RWKV)),
                  _const_spec((1, D_RWKV)), _const_spec((1, D_RWKV)), _const_spec((1, D_RWKV)),
                  _const_spec((D_RWKV, D_RWKV)), shift_spec],
        out_specs=[hm_spec] * 6 + [ld_spec] * 3 + [shift_spec],
        out_shape=[hm_shape] * 6 + [ld_shape] * 3 + [nshift_shape],
        scratch_shapes=[pltpu.VMEM((1, N_SHIFT), F32)],
        compiler_params=_params(),
        name=name,
    )(x1, sh, sc, gpre, w["w_in"], w["mu"], w["w0"], w["wd"], w["a0"], w["wi"], w["wg"],
      w["k_k"], w["k_a"], w["r_k"], w["ebd"], shift0)


def _wkv_body(r_ref, k_ref, v_ref, a_ref, b_ref, ld_ref, s0_ref, y_ref, st_ref, s_scr, *, chunk, n_sub):
    c = pl.program_id(1)
    rows = WKV_ROWS
    gs = rows // chunk
    shift = chunk.bit_length() - 1

    @pl.when(c == 0)
    def _():
        s_scr[...] = s0_ref[...]

    ri = lax.broadcasted_iota(jnp.int32, (rows, rows), 0)
    ci = lax.broadcasted_iota(jnp.int32, (rows, rows), 1)
    same = jnp.right_shift(ri, shift) == jnp.right_shift(ci, shift)
    strict = same & (ci < ri)
    incl = same & (ci <= ri)
    tri = incl.astype(BF16)
    eye = (ri == ci).astype(F32)

    def sub(sb, carry):
        sl = pl.ds(pl.multiple_of(sb * gs, gs), gs)
        flat = lambda ref: ref[sl].reshape(rows, HEAD)
        r, k, v, a, b, ld = (flat(ref) for ref in (r_ref, k_ref, v_ref, a_ref, b_ref, ld_ref))
        cs = _mm(tri, ld, NN, na=1, nb=3)
        e_in = jnp.exp(cs)
        e_inv = jnp.exp(-cs)
        at = a * jnp.exp(cs - ld)
        rt = r * e_in
        bt = b * e_inv
        kt = k * e_inv
        gram = _mm(jnp.concatenate([at, rt], axis=0), jnp.concatenate([bt, kt], axis=0), NT)
        t_ab = jnp.where(strict, gram[:rows, :rows], 0.0)
        t_ak = jnp.where(strict, gram[:rows, rows:], 0.0)
        m_rb = jnp.where(incl, gram[rows:, :rows], 0.0)
        m_rk = jnp.where(incl, gram[rows:, rows:], 0.0)
        inv = eye + t_ab
        pw = t_ab
        for _ in range(shift - 1):
            pw = _mm(pw, pw, NN)
            inv = inv + _mm(pw, inv, NN)
        s0 = s_scr[sl]
        three = lambda z: z.reshape(gs, chunk, HEAD)
        ar = jnp.concatenate([three(at), three(rt)], axis=1)
        uy0 = _mm(ar, s0, B_NT)
        u0 = uy0[:, :chunk, :].reshape(rows, HEAD)
        y0 = uy0[:, chunk:, :].reshape(rows, HEAD)
        u = _mm(inv, u0 + _mm(t_ak, v, NN), NN)
        y = y0 + _mm(m_rb, u, NN) + _mm(m_rk, v, NN)
        y_ref[sl] = three(y)
        cs3 = three(cs)
        c_last = cs3[:, chunk - 1:chunk, :]
        tail = jnp.exp(c_last - cs3)
        uv = jnp.concatenate([three(u), three(v)], axis=1)
        bk = jnp.concatenate([three(b) * tail, three(k) * tail], axis=1)
        s_scr[sl] = s0 * jnp.exp(c_last) + _mm(uv, bk, B_TN)
        return carry

    lax.fori_loop(0, n_sub, sub, 0)

    @pl.when(c == pl.num_programs(1) - 1)
    def _():
        st_ref[...] = s_scr[...]


def _wkv_call(ops, s0, *, gblk, chunk, name):
    ng, t_total, _ = ops[0].shape
    gs = WKV_ROWS // chunk
    op_spec = pl.BlockSpec((gblk, chunk, HEAD), lambda g, c: (g, c, 0))
    st_spec = pl.BlockSpec((gblk, HEAD, HEAD), lambda g, c: (g, 0, 0))
    return pl.pallas_call(
        functools.partial(_wkv_body, chunk=chunk, n_sub=gblk // gs),
        grid=(ng // gblk, t_total // chunk),
        in_specs=[op_spec] * 6 + [st_spec],
        out_specs=[op_spec, st_spec],
        out_shape=[jax.ShapeDtypeStruct((ng, t_total, HEAD), F32),
                   jax.ShapeDtypeStruct((ng, HEAD, HEAD), F32)],
        scratch_shapes=[pltpu.VMEM((gblk, HEAD, HEAD), F32)],
        compiler_params=_params(),
        name=name,
    )(*ops, s0)


def _post_body(y_ref, bonus_ref, gate_ref, u_ref, hist_ref, x_ref, gt_ref,
               lng_ref, lnb_ref, cw_ref, cb_ref, clg_ref, clb_ref, wout_ref, gpost_ref, ebd_ref,
               o_ref, full_scr, *, tm, stride, head_major):
    if head_major:
        y = jnp.concatenate([y_ref[hh] for hh in range(N_HEADS)], axis=-1)
    else:
        y = y_ref[...]
    ebd = ebd_ref[...]
    mu = _head_sum(y, ebd) * (1.0 / HEAD)
    d = y - mu
    var = _head_sum(d * d, ebd) * (1.0 / HEAD)
    yn = d * lax.rsqrt(var + GN_EPS) * lng_ref[...] + lnb_ref[...]
    y_rwkv = (yn + bonus_ref[...]) * gate_ref[...]

    hist = full_scr.shape[0] - tm
    if stride == 1:
        @pl.when(pl.program_id(1) == 0)
        def _():
            full_scr[:hist, :] = hist_ref[...]
    else:
        full_scr[:hist, :] = hist_ref[...]
    u = u_ref[...]
    full_scr[hist:, :] = u
    base = hist - (CONV_W - 1) * stride
    z = jnp.zeros((tm, D_CONV), F32)
    for j in range(CONV_W):
        z = z + full_scr[pl.ds(base + j * stride, tm), :] * cw_ref[j:j + 1, :]
    if stride == 1:
        full_scr[:hist, :] = u[tm - hist:, :]
    z = z + cb_ref[...]
    zm = jnp.mean(z, axis=-1, keepdims=True)
    zc = z - zm
    zv = jnp.mean(zc * zc, axis=-1, keepdims=True)
    zn = zc * lax.rsqrt(zv + LN_EPS) * clg_ref[...] + clb_ref[...]
    zn = zn * _sigmoid(zn)

    mixed = (jnp.dot(y_rwkv.astype(BF16), wout_ref[:D_RWKV, :], preferred_element_type=F32)
             + jnp.dot(zn.astype(BF16), wout_ref[D_RWKV:, :], preferred_element_type=F32))
    o_ref[...] = x_ref[...] + gt_ref[...] * _rms(mixed, gpost_ref[...])


def _post_call(y, bonus, gate, u, hist, x1, gt, w, *, nb, tiles, tm, stride, head_major, name):
    if head_major:
        y_spec = pl.BlockSpec((None, N_HEADS, tm, HEAD), lambda b, j: (b, 0, j, 0))
        hist_spec = pl.BlockSpec((None, HALO, D_CONV), lambda b, j: (b, 0, 0))
        hist_rows = HALO
    else:
        y_spec = _tok_spec(tm, D_RWKV, tiles)
        hist_rows = hist.shape[0]
        hist_spec = pl.BlockSpec((hist_rows, D_CONV), lambda b, j: (0, 0))
    tok512 = _tok_spec(tm, D_RWKV, tiles)
    row512 = _const_spec((1, D_RWKV))
    return pl.pallas_call(
        functools.partial(_post_body, tm=tm, stride=stride, head_major=head_major),
        grid=(nb, tiles),
        in_specs=[y_spec, tok512, tok512, tok512, hist_spec, _tok_spec(tm, D_MODEL, tiles),
                  _mod_spec(gt, tm, tiles),
                  row512, row512, _const_spec((CONV_W, D_CONV)), row512, row512, row512,
                  _const_spec((D_MODEL, D_MODEL)), _const_spec((1, D_MODEL)),
                  _const_spec((D_RWKV, D_RWKV))],
        out_specs=_tok_spec(tm, D_MODEL, tiles),
        out_shape=jax.ShapeDtypeStruct(x1.shape, F32),
        scratch_shapes=[pltpu.VMEM((hist_rows + tm, D_CONV), F32)],
        compiler_params=_params(),
        name=name,
    )(y, bonus, gate, u, hist, x1, gt, w["ln_x_g"], w["ln_x_b"], w["conv_w"], w["conv_b"],
      w["conv_ln_g"], w["conv_ln_b"], w["w_out"], w["g_post1"], w["ebd"])


def _layer(x, mods, w, shift0, hist, s0, *, nb, seq, tm, tm_pre, stride, head_major, tag):
    tiles = (seq // tm) if head_major else 1
    tiles_pre = (seq // tm_pre) if head_major else 1
    x1 = _ffn_call(x, mods[0:3], w["g_pre0"], w["g_post0"], w["wup1"], w["wdn1"],
                   nb=nb, tiles=tiles, tm=tm, res_w=0.5, name=tag + "_ffn1")
    r, k, v, a, b, ld, gate, u, bonus, nshift = _pre_call(
        x1, mods[3:5], w["g_pre1"], w, shift0, nb=nb, tiles=tiles_pre, tm=tm_pre, stride=stride,
        head_major=head_major, name=tag + "_mixpre")
    if head_major:
        ops = [z.reshape(nb * N_HEADS, seq, HEAD) for z in (r, k, v, a, b, ld)]
        y, s_new = _wkv_call(ops, s0, gblk=nb * N_HEADS, chunk=32, name=tag + "_wkv")
        y = y.reshape(nb, N_HEADS, seq, HEAD)
    else:
        t_len = x.shape[0] // stride

        def to_groups(z):
            z = z.reshape(t_len, stride, N_HEADS, HEAD).transpose(1, 2, 0, 3)
            z = jnp.pad(z, ((0, 0), (0, 0), (0, 8 - t_len), (0, 0)))
            return z.reshape(stride * N_HEADS, 8, HEAD)

        ops = [to_groups(z) for z in (r, k, v, a, b, ld)]
        y, s_new = _wkv_call(ops, s0, gblk=128, chunk=8, name=tag + "_wkv")
        y = y.reshape(stride, N_HEADS, 8, HEAD)[:, :, :t_len].transpose(2, 0, 1, 3).reshape(x.shape[0], D_RWKV)
    x2 = _post_call(y, bonus, gate, u, hist, x1, mods[5], w, nb=nb, tiles=tiles, tm=tm, stride=stride,
                    head_major=head_major, name=tag + "_mixpost")
    out = _ffn_call(x2, mods[6:9], w["g_pre2"], w["g_post2"], w["wup2"], w["wdn2"],
                    nb=nb, tiles=tiles, tm=tm, res_w=0.5, name=tag + "_ffn2")
    return out, s_new, nshift, u


def kernel(x_prompt, x_sample, c_prompt, c_sample, state_wkv, state_shift, state_conv, w_ada, b_ada, g_pre, g_post, w_ffn1_up, w_ffn1_down, w_in, mu_shift, w0, w_decay_up, a0, w_iclr_up, w_gate_up, k_k, k_a, r_k, ln_x_g, ln_x_b, conv_w, conv_b, conv_ln_g, conv_ln_b, w_out, w_ffn2_up, w_ffn2_down):
    bp, seq, _ = x_prompt.shape
    bs, dec, _ = x_sample.shape
    row = lambda z: z.reshape(1, -1).astype(F32)
    zeros64 = jnp.zeros((ICLR_RANK, D_RWKV), BF16)
    head_of_lane = jnp.arange(D_RWKV) // HEAD
    w = dict(
        wup1=w_ffn1_up.astype(BF16), wdn1=w_ffn1_down.astype(BF16),
        wup2=w_ffn2_up.astype(BF16), wdn2=w_ffn2_down.astype(BF16),
        w_in=w_in.astype(BF16), w_out=w_out.astype(BF16),
        wd=jnp.concatenate([w_decay_up.astype(BF16), zeros64], axis=0),
        wi=jnp.concatenate([zeros64, w_iclr_up.astype(BF16)], axis=0),
        wg=w_gate_up.astype(BF16),
        mu=row(mu_shift), w0=row(w0), a0=row(a0), k_k=row(k_k), k_a=row(k_a), r_k=row(r_k),
        ln_x_g=row(ln_x_g), ln_x_b=row(ln_x_b), conv_w=conv_w, conv_b=row(conv_b),
        conv_ln_g=row(conv_ln_g), conv_ln_b=row(conv_ln_b),
        g_pre0=g_pre[0:1], g_pre1=g_pre[1:2], g_pre2=g_pre[2:3],
        g_post0=g_post[0:1], g_post1=g_post[1:2], g_post2=g_post[2:3],
        ebd=(head_of_lane[:, None] == head_of_lane[None, :]).astype(BF16),
    )

    ada = _ada_call(jnp.concatenate([c_prompt, c_sample], axis=0), w_ada, b_ada)
    ada = ada.reshape(bp + bs, N_SUB * 3, D_MODEL)
    mods_p = [ada[:bp, i][:, None, :] for i in range(N_SUB * 3)]
    mods_s = [jnp.tile(ada[bp:, i], (dec, 1)) for i in range(N_SUB * 3)]

    yp, wkv_p, shift_p, u_p = _layer(
        x_prompt.reshape(bp * seq, D_MODEL), mods_p, w,
        jnp.zeros((bp, 1, N_SHIFT), F32), jnp.zeros((bp, HALO, D_CONV), F32),
        jnp.zeros((bp * N_HEADS, HEAD, HEAD), F32),
        nb=bp, seq=seq, tm=512, tm_pre=256, stride=1, head_major=True, tag="p")
    y_p = yp.reshape(bp, seq, D_MODEL)
    wkv_p = wkv_p.reshape(bp, N_HEADS, HEAD, HEAD)
    conv_p = u_p.reshape(bp, seq, D_CONV)[:, seq - (CONV_W - 1):, :]

    xs_tok = x_sample.transpose(1, 0, 2).reshape(dec * bs, D_MODEL)
    hist_s = state_conv.transpose(1, 0, 2).reshape((CONV_W - 1) * bs, D_CONV)
    ys, wkv_s, shift_s, u_s = _layer(
        xs_tok, mods_s, w, state_shift.reshape(bs, N_SHIFT), hist_s,
        state_wkv.reshape(bs * N_HEADS, HEAD, HEAD),
        nb=1, seq=dec * bs, tm=dec * bs, tm_pre=dec * bs, stride=bs, head_major=False, tag="s")
    y_s = ys.reshape(dec, bs, D_MODEL).transpose(1, 0, 2)
    wkv_s = wkv_s.reshape(bs, N_HEADS, HEAD, HEAD)
    shift_s = shift_s.reshape(bs, 1, N_SHIFT)
    conv_s = jnp.concatenate([state_conv, u_s.reshape(dec, bs, D_CONV).transpose(1, 0, 2)],
                             axis=1)[:, dec:, :]
    return (y_p, y_s, wkv_p, shift_p, conv_p, wkv_s, shift_s, conv_s)
```

```python
import functools

import jax
import jax.numpy as jnp
from jax import lax
from jax.experimental import pallas as pl
from jax.experimental.pallas import tpu as pltpu

F32 = jnp.float32
BF16 = jnp.bfloat16

D_MODEL = 1024
D_RWKV = 512
HEAD = 64
N_HEADS = D_RWKV // HEAD
D_CONV = D_MODEL - D_RWKV
CONV_W = 31
DECAY_RANK = 64
ICLR_RANK = 64
GATE_RANK = 128
D_FF = 2816
N_SHIFT = 3 * D_RWKV + DECAY_RANK + ICLR_RANK + GATE_RANK
D_IN = N_SHIFT + 2 * D_CONV
N_SUB = 3
RMS_EPS = 1e-6
LN_EPS = 1e-5
GN_EPS = 64e-5

FF_CHUNK = 256
WKV_ROWS = 256
WKV_WAYS = 4
HALO = 32
VMEM_LIMIT = 56 * 1024 * 1024

NN = (((1,), (0,)), ((), ()))
NT = (((1,), (1,)), ((), ()))
B_NT = (((2,), (2,)), ((0,), (0,)))
B_TN = (((1,), (1,)), ((0,), (0,)))


def _splits(x, n):
    if x.dtype == BF16:
        return [x]
    parts, rem = [], x
    for i in range(n):
        p = rem.astype(BF16)
        parts.append(p)
        if i + 1 < n:
            rem = rem - p.astype(F32)
    return parts


def _mm(a, b, dn, na=2, nb=2):
    ap, bp = _splits(a, na), _splits(b, nb)
    depth = max(len(ap), len(bp))
    out = None
    for i, x in enumerate(ap):
        for j, y in enumerate(bp):
            if i + j >= depth:
                continue
            t = lax.dot_general(x, y, dn, preferred_element_type=F32)
            out = t if out is None else out + t
    return out


def _rms(x, g):
    return x * lax.rsqrt(jnp.mean(x * x, axis=-1, keepdims=True) + RMS_EPS) * g


def _sigmoid(x):
    return jax.nn.sigmoid(x)


def _head_sum(x, ebd):
    return _mm(x, ebd, NN, na=2, nb=1)


def _ada_body(c_ref, w_ref, b_ref, o_ref):
    c = c_ref[...]
    s = (c * _sigmoid(c)).astype(BF16)
    o_ref[...] = jnp.dot(s, w_ref[...].astype(BF16), preferred_element_type=F32) + b_ref[...]


def _ada_call(c_all, w_ada, b_ada):
    n, tn = c_all.shape[0], 1152
    width = w_ada.shape[1]
    return pl.pallas_call(
        _ada_body,
        grid=(width // tn,),
        in_specs=[pl.BlockSpec((n, D_MODEL), lambda i: (0, 0)),
                  pl.BlockSpec((D_MODEL, tn), lambda i: (0, i)),
                  pl.BlockSpec((1, tn), lambda i: (0, i))],
        out_specs=pl.BlockSpec((n, tn), lambda i: (0, i)),
        out_shape=jax.ShapeDtypeStruct((n, width), F32),
        compiler_params=pltpu.CompilerParams(dimension_semantics=("arbitrary",),
                                             vmem_limit_bytes=VMEM_LIMIT),
        name="ada",
    )(c_all, w_ada, b_ada.reshape(1, width))


def _const_spec(shape):
    nd = len(shape)
    return pl.BlockSpec(shape, lambda b, j: (0,) * nd, pipeline_mode=pl.Buffered(1))


def _tok_spec(tm, width, tiles):
    return pl.BlockSpec((tm, width), lambda b, j: (b * tiles + j, 0))


def _mod_spec(arr, tm, tiles):
    if arr.ndim == 3:
        return pl.BlockSpec((None, 1, D_MODEL), lambda b, j: (b, 0, 0))
    return _tok_spec(tm, D_MODEL, tiles)


def _params():
    return pltpu.CompilerParams(dimension_semantics=("arbitrary", "arbitrary"),
                                vmem_limit_bytes=VMEM_LIMIT)


def _ffn_body(x_ref, sh_ref, sc_ref, gt_ref, gpre_ref, gpost_ref, wup_ref, wdn_ref, o_ref, act_scr, *, res_w):
    x = x_ref[...]
    h = _rms(x, gpre_ref[...]) * (1.0 + sc_ref[...]) + sh_ref[...]
    hb = h.astype(BF16)
    for j in range(D_FF // FF_CHUNK):
        lo = j * FF_CHUNK
        g = jnp.dot(hb, wup_ref[:, lo:lo + FF_CHUNK], preferred_element_type=F32)
        u = jnp.dot(hb, wup_ref[:, D_FF + lo:D_FF + lo + FF_CHUNK], preferred_element_type=F32)
        act_scr[:, lo:lo + FF_CHUNK] = (g * _sigmoid(g) * u).astype(BF16)
    out = jnp.dot(act_scr[...], wdn_ref[...], preferred_element_type=F32)
    o_ref[...] = x + res_w * gt_ref[...] * _rms(out, gpost_ref[...])


def _ffn_call(x, mods, gpre, gpost, wup, wdn, *, nb, tiles, tm, res_w, name):
    sh, sc, gt = mods
    return pl.pallas_call(
        functools.partial(_ffn_body, res_w=res_w),
        grid=(nb, tiles),
        in_specs=[_tok_spec(tm, D_MODEL, tiles),
                  _mod_spec(sh, tm, tiles), _mod_spec(sc, tm, tiles), _mod_spec(gt, tm, tiles),
                  _const_spec((1, D_MODEL)), _const_spec((1, D_MODEL)),
                  _const_spec((D_MODEL, 2 * D_FF)), _const_spec((D_FF, D_MODEL))],
        out_specs=_tok_spec(tm, D_MODEL, tiles),
        out_shape=jax.ShapeDtypeStruct(x.shape, F32),
        scratch_shapes=[pltpu.VMEM((tm, D_FF), BF16)],
        compiler_params=_params(),
        name=name,
    )(x, sh, sc, gt, gpre, gpost, wup, wdn)


def _pre_body(x_ref, sh_ref, sc_ref, gpre_ref, win_ref, mu_ref, w0_ref, wd_ref, a0_ref, wi_ref, wg_ref,
              kk_ref, ka_ref, rk_ref, ebd_ref, shift0_ref,
              r_ref, k_ref, v_ref, a_ref, b_ref, ld_ref, gate_ref, u_ref, bonus_ref, nshift_ref,
              carry_scr, *, tm, stride, head_major):
    x = x_ref[...]
    h = _rms(x, gpre_ref[...]) * (1.0 + sc_ref[...]) + sh_ref[...]
    proj = jnp.dot(h.astype(BF16), win_ref[...], preferred_element_type=F32)
    p_sh = proj[:, :N_SHIFT]
    p_cv = proj[:, N_SHIFT:]

    if stride == 1:
        @pl.when(pl.program_id(1) == 0)
        def _():
            carry_scr[...] = shift0_ref[...]
        row = lax.broadcasted_iota(jnp.int32, (tm, 1), 0)
        prev = jnp.where(row == 0, carry_scr[...], pltpu.roll(p_sh, 1, axis=0))
        carry_scr[...] = p_sh[tm - 1:tm, :]
    else:
        prev = jnp.concatenate([shift0_ref[...], p_sh[:tm - stride, :]], axis=0)
    nshift_ref[...] = p_sh[tm - stride:tm, :]
    xs = p_sh + (prev - p_sh) * mu_ref[...]

    o1, o2, o3 = D_RWKV, 2 * D_RWKV, 3 * D_RWKV
    r, k, v = xs[:, :o1], xs[:, o1:o2], xs[:, o2:o3]
    dwda = xs[:, o3:o3 + DECAY_RANK + ICLR_RANK]
    dg = xs[:, o3 + DECAY_RANK + ICLR_RANK:]

    zdec = -(w0_ref[...] + jnp.dot(jnp.tanh(dwda).astype(BF16), wd_ref[...], preferred_element_type=F32))
    softplus = jnp.maximum(zdec, 0.0) + jnp.log1p(jnp.exp(-jnp.abs(zdec)))
    ld = -jnp.exp(-softplus - 0.5)
    iclr = _sigmoid(a0_ref[...] + jnp.dot(dwda.astype(BF16), wi_ref[...], preferred_element_type=F32))
    gate = jnp.dot(_sigmoid(dg).astype(BF16), wg_ref[...], preferred_element_type=F32)

    ebd = ebd_ref[...]
    kk = k * kk_ref[...]
    kk = kk / jnp.maximum(jnp.sqrt(_head_sum(kk * kk, ebd)), 1e-12)
    kmod = k * (1.0 + (iclr - 1.0) * ka_ref[...])
    bonus = _head_sum(r * kmod * rk_ref[...], ebd) * v

    gate_ref[...] = gate
    bonus_ref[...] = bonus
    u_ref[...] = p_cv[:, :D_CONV] * _sigmoid(p_cv[:, D_CONV:])
    outs = ((r_ref, r), (k_ref, kmod), (v_ref, v), (a_ref, -kk), (b_ref, kk * iclr), (ld_ref, ld))
    for ref, val in outs:
        if head_major:
            for hh in range(N_HEADS):
                ref[hh] = val[:, hh * HEAD:(hh + 1) * HEAD]
        else:
            ref[...] = val


def _pre_call(x1, mods, gpre, w, shift0, *, nb, tiles, tm, stride, head_major, name):
    sh, sc = mods
    n = x1.shape[0]
    if head_major:
        seq = tiles * tm
        hm_spec = pl.BlockSpec((None, N_HEADS, tm, HEAD), lambda b, j: (b, 0, j, 0))
        hm_shape = jax.ShapeDtypeStruct((nb, N_HEADS, seq, HEAD), F32)
    else:
        hm_spec = _tok_spec(tm, D_RWKV, tiles)
        hm_shape = jax.ShapeDtypeStruct((n, D_RWKV), F32)
    ld_spec = _tok_spec(tm, D_RWKV, tiles)
    ld_shape = jax.ShapeDtypeStruct((n, D_RWKV), F32)
    if stride == 1:
        shift_spec = pl.BlockSpec((None, 1, N_SHIFT), lambda b, j: (b, 0, 0))
        nshift_shape = jax.ShapeDtypeStruct((nb, 1, N_SHIFT), F32)
    else:
        shift_spec = pl.BlockSpec((stride, N_SHIFT), lambda b, j: (0, 0))
        nshift_shape = jax.ShapeDtypeStruct((stride, N_SHIFT), F32)
    return pl.pallas_call(
        functools.partial(_pre_body, tm=tm, stride=stride, head_major=head_major),
        grid=(nb, tiles),
        in_specs=[_tok_spec(tm, D_MODEL, tiles), _mod_spec(sh, tm, tiles), _mod_spec(sc, tm, tiles),
                  _const_spec((1, D_MODEL)), _const_spec((D_MODEL, D_IN)), _const_spec((1, N_SHIFT)),
                  _const_spec((1, D_RWKV)), _const_spec((DECAY_RANK + ICLR_RANK, D_RWKV)),
                  _const_spec((1, D_RWKV)), _const_spec((DECAY_RANK + ICLR_RANK, D_RWKV)),
                  _const_spec((GATE_RANK, D_RWKV)),
                  _const_spec((1, D_RWKV)), _const_spec((1, D_RWKV)), _const_spec((1, D_RWKV)),
                  _const_spec((D_RWKV, D_RWKV)), shift_spec],
        out_specs=[hm_spec] * 6 + [ld_spec] * 3 + [shift_spec],
        out_shape=[hm_shape] * 6 + [ld_shape] * 3 + [nshift_shape],
        scratch_shapes=[pltpu.VMEM((1, N_SHIFT), F32)],
        compiler_params=_params(),
        name=name,
    )(x1, sh, sc, gpre, w["w_in"], w["mu"], w["w0"], w["wd"], w["a0"], w["wi"], w["wg"],
      w["k_k"], w["k_a"], w["r_k"], w["ebd"], shift0)


def _wkv_body(r_ref, k_ref, v_ref, a_ref, b_ref, ld_ref, s0_ref, y_ref, st_ref, s_scr, *, chunk, n_sub, ways):
    c = pl.program_id(1)
    rows = WKV_ROWS
    gs = rows // chunk
    shift = chunk.bit_length() - 1

    @pl.when(c == 0)
    def _():
        s_scr[...] = s0_ref[...]

    ri = lax.broadcasted_iota(jnp.int32, (rows, rows), 0)
    ci = lax.broadcasted_iota(jnp.int32, (rows, rows), 1)
    same = jnp.right_shift(ri, shift) == jnp.right_shift(ci, shift)
    strict = same & (ci < ri)
    incl = same & (ci <= ri)
    tri = incl.astype(BF16)
    eye = (ri == ci).astype(F32)

    three = lambda z: z.reshape(gs, chunk, HEAD)
    each = lambda f, *xs: [f(*t) for t in zip(*xs)]

    def sub(it, carry):
        sls = [pl.ds(pl.multiple_of((it * ways + q) * gs, gs), gs) for q in range(ways)]
        load = lambda ref: [ref[sl].reshape(rows, HEAD) for sl in sls]
        r, k, v, a, b, ld = (load(ref) for ref in (r_ref, k_ref, v_ref, a_ref, b_ref, ld_ref))
        cs = each(lambda x: _mm(tri, x, NN, na=1, nb=2), ld)
        e_inv = each(lambda x: jnp.exp(-x), cs)
        at = each(lambda x, c_, l_: x * jnp.exp(c_ - l_), a, cs, ld)
        rt = each(lambda x, c_: x * jnp.exp(c_), r, cs)
        bt = each(lambda x, e: x * e, b, e_inv)
        kt = each(lambda x, e: x * e, k, e_inv)
        gram = each(lambda p, q, m, n: _mm(jnp.concatenate([p, q], axis=0), jnp.concatenate([m, n], axis=0),
                                           NT, 1, 1), at, rt, bt, kt)
        t_ab = each(lambda g: jnp.where(strict, g[:rows, :rows], 0.0), gram)
        t_ak = each(lambda g: jnp.where(strict, g[:rows, rows:], 0.0), gram)
        m_rb = each(lambda g: jnp.where(incl, g[rows:, :rows], 0.0), gram)
        m_rk = each(lambda g: jnp.where(incl, g[rows:, rows:], 0.0), gram)
        inv = each(lambda t: eye + t, t_ab)
        pw = t_ab
        for _ in range(shift - 1):
            pw = each(lambda p: _mm(p, p, NN, 1, 1), pw)
            inv = each(lambda x, p: x + _mm(p, x, NN, 1, 1), inv, pw)
        s0 = [s_scr[sl] for sl in sls]
        uy0 = each(lambda p, q, s: _mm(jnp.concatenate([three(p), three(q)], axis=1), s, B_NT, 1, 1), at, rt, s0)
        z = each(lambda o, t, x: o[:, :chunk, :].reshape(rows, HEAD) + _mm(t, x, NN, 1, 1), uy0, t_ak, v)
        u = each(lambda x, z_: _mm(x, z_, NN, 1, 1), inv, z)
        res = each(lambda z_, u_, t: z_ - u_ + _mm(t, u_, NN), z, u, t_ab)
        u = each(lambda u_, x, r_: u_ + _mm(x, r_, NN, 1, 1), u, inv, res)
        y = each(lambda o, m, u_, n, x: o[:, chunk:, :].reshape(rows, HEAD) + _mm(m, u_, NN, 1, 1)
                 + _mm(n, x, NN, 1, 1), uy0, m_rb, u, m_rk, v)
        for sl, y_ in zip(sls, y):
            y_ref[sl] = three(y_)
        cs3 = each(three, cs)
        c_last = each(lambda x: x[:, chunk - 1:chunk, :], cs3)
        tail = each(lambda cl, x: jnp.exp(cl - x), c_last, cs3)
        upd = each(lambda u_, x, b_, k_, t: _mm(jnp.concatenate([three(u_), three(x)], axis=1),
                                                jnp.concatenate([three(b_) * t, three(k_) * t], axis=1), B_TN),
                   u, v, b, k, tail)
        for sl, s, cl, d in zip(sls, s0, c_last, upd):
            s_scr[sl] = s * jnp.exp(cl) + d
        return carry

    lax.fori_loop(0, n_sub // ways, sub, 0)

    @pl.when(c == pl.num_programs(1) - 1)
    def _():
        st_ref[...] = s_scr[...]


def _wkv_call(ops, s0, *, gblk, chunk, name):
    ng, t_total, _ = ops[0].shape
    gs = WKV_ROWS // chunk
    op_spec = pl.BlockSpec((gblk, chunk, HEAD), lambda g, c: (g, c, 0))
    st_spec = pl.BlockSpec((gblk, HEAD, HEAD), lambda g, c: (g, 0, 0))
    return pl.pallas_call(
        functools.partial(_wkv_body, chunk=chunk, n_sub=gblk // gs, ways=WKV_WAYS),
        grid=(ng // gblk, t_total // chunk),
        in_specs=[op_spec] * 6 + [st_spec],
        out_specs=[op_spec, st_spec],
        out_shape=[jax.ShapeDtypeStruct((ng, t_total, HEAD), F32),
                   jax.ShapeDtypeStruct((ng, HEAD, HEAD), F32)],
        scratch_shapes=[pltpu.VMEM((gblk, HEAD, HEAD), F32)],
        compiler_params=_params(),
        name=name,
    )(*ops, s0)


def _post_body(y_ref, bonus_ref, gate_ref, u_ref, hist_ref, x_ref, gt_ref,
               lng_ref, lnb_ref, cw_ref, cb_ref, clg_ref, clb_ref, wout_ref, gpost_ref, ebd_ref,
               o_ref, full_scr, *, tm, stride, head_major):
    if head_major:
        y = jnp.concatenate([y_ref[hh] for hh in range(N_HEADS)], axis=-1)
    else:
        y = y_ref[...]
    ebd = ebd_ref[...]
    mu = _head_sum(y, ebd) * (1.0 / HEAD)
    d = y - mu
    var = _head_sum(d * d, ebd) * (1.0 / HEAD)
    yn = d * lax.rsqrt(var + GN_EPS) * lng_ref[...] + lnb_ref[...]
    y_rwkv = (yn + bonus_ref[...]) * gate_ref[...]

    hist = full_scr.shape[0] - tm
    if stride == 1:
        @pl.when(pl.program_id(1) == 0)
        def _():
            full_scr[:hist, :] = hist_ref[...]
    else:
        full_scr[:hist, :] = hist_ref[...]
    u = u_ref[...]
    full_scr[hist:, :] = u
    base = hist - (CONV_W - 1) * stride
    z = jnp.zeros((tm, D_CONV), F32)
    for j in range(CONV_W):
        z = z + full_scr[pl.ds(base + j * stride, tm), :] * cw_ref[j:j + 1, :]
    if stride == 1:
        full_scr[:hist, :] = u[tm - hist:, :]
    z = z + cb_ref[...]
    zm = jnp.mean(z, axis=-1, keepdims=True)
    zc = z - zm
    zv = jnp.mean(zc * zc, axis=-1, keepdims=True)
    zn = zc * lax.rsqrt(zv + LN_EPS) * clg_ref[...] + clb_ref[...]
    zn = zn * _sigmoid(zn)

    mixed = (jnp.dot(y_rwkv.astype(BF16), wout_ref[:D_RWKV, :], preferred_element_type=F32)
             + jnp.dot(zn.astype(BF16), wout_ref[D_RWKV:, :], preferred_element_type=F32))
    o_ref[...] = x_ref[...] + gt_ref[...] * _rms(mixed, gpost_ref[...])


def _post_call(y, bonus, gate, u, hist, x1, gt, w, *, nb, tiles, tm, stride, head_major, name):
    if head_major:
        y_spec = pl.BlockSpec((None, N_HEADS, tm, HEAD), lambda b, j: (b, 0, j, 0))
        hist_spec = pl.BlockSpec((None, HALO, D_CONV), lambda b, j: (b, 0, 0))
        hist_rows = HALO
    else:
        y_spec = _tok_spec(tm, D_RWKV, tiles)
        hist_rows = hist.shape[0]
        hist_spec = pl.BlockSpec((hist_rows, D_CONV), lambda b, j: (0, 0))
    tok512 = _tok_spec(tm, D_RWKV, tiles)
    row512 = _const_spec((1, D_RWKV))
    return pl.pallas_call(
        functools.partial(_post_body, tm=tm, stride=stride, head_major=head_major),
        grid=(nb, tiles),
        in_specs=[y_spec, tok512, tok512, tok512, hist_spec, _tok_spec(tm, D_MODEL, tiles),
                  _mod_spec(gt, tm, tiles),
                  row512, row512, _const_spec((CONV_W, D_CONV)), row512, row512, row512,
                  _const_spec((D_MODEL, D_MODEL)), _const_spec((1, D_MODEL)),
                  _const_spec((D_RWKV, D_RWKV))],
        out_specs=_tok_spec(tm, D_MODEL, tiles),
        out_shape=jax.ShapeDtypeStruct(x1.shape, F32),
        scratch_shapes=[pltpu.VMEM((hist_rows + tm, D_CONV), F32)],
        compiler_params=_params(),
        name=name,
    )(y, bonus, gate, u, hist, x1, gt, w["ln_x_g"], w["ln_x_b"], w["conv_w"], w["conv_b"],
      w["conv_ln_g"], w["conv_ln_b"], w["w_out"], w["g_post1"], w["ebd"])


def _layer(x, mods, w, shift0, hist, s0, *, nb, seq, tm, tm_pre, stride, head_major, tag):
    tiles = (seq // tm) if head_major else 1
    tiles_pre = (seq // tm_pre) if head_major else 1
    x1 = _ffn_call(x, mods[0:3], w["g_pre0"], w["g_post0"], w["wup1"], w["wdn1"],
                   nb=nb, tiles=tiles, tm=tm, res_w=0.5, name=tag + "_ffn1")
    r, k, v, a, b, ld, gate, u, bonus, nshift = _pre_call(
        x1, mods[3:5], w["g_pre1"], w, shift0, nb=nb, tiles=tiles_pre, tm=tm_pre, stride=stride,
        head_major=head_major, name=tag + "_mixpre")
    if head_major:
        ops = [z.reshape(nb * N_HEADS, seq, HEAD) for z in (r, k, v, a, b, ld)]
        y, s_new = _wkv_call(ops, s0, gblk=nb * N_HEADS, chunk=32, name=tag + "_wkv")
        y = y.reshape(nb, N_HEADS, seq, HEAD)
    else:
        t_len = x.shape[0] // stride

        def to_groups(z):
            z = z.reshape(t_len, stride, N_HEADS, HEAD).transpose(1, 2, 0, 3)
            z = jnp.pad(z, ((0, 0), (0, 0), (0, 8 - t_len), (0, 0)))
            return z.reshape(stride * N_HEADS, 8, HEAD)

        ops = [to_groups(z) for z in (r, k, v, a, b, ld)]
        y, s_new = _wkv_call(ops, s0, gblk=128, chunk=8, name=tag + "_wkv")
        y = y.reshape(stride, N_HEADS, 8, HEAD)[:, :, :t_len].transpose(2, 0, 1, 3).reshape(x.shape[0], D_RWKV)
    x2 = _post_call(y, bonus, gate, u, hist, x1, mods[5], w, nb=nb, tiles=tiles, tm=tm, stride=stride,
                    head_major=head_major, name=tag + "_mixpost")
    out = _ffn_call(x2, mods[6:9], w["g_pre2"], w["g_post2"], w["wup2"], w["wdn2"],
                    nb=nb, tiles=tiles, tm=tm, res_w=0.5, name=tag + "_ffn2")
    return out, s_new, nshift, u


def kernel(x_prompt, x_sample, c_prompt, c_sample, state_wkv, state_shift, state_conv, w_ada, b_ada, g_pre, g_post, w_ffn1_up, w_ffn1_down, w_in, mu_shift, w0, w_decay_up, a0, w_iclr_up, w_gate_up, k_k, k_a, r_k, ln_x_g, ln_x_b, conv_w, conv_b, conv_ln_g, conv_ln_b, w_out, w_ffn2_up, w_ffn2_down):
    bp, seq, _ = x_prompt.shape
    bs, dec, _ = x_sample.shape
    row = lambda z: z.reshape(1, -1).astype(F32)
    zeros64 = jnp.zeros((ICLR_RANK, D_RWKV), BF16)
    head_of_lane = jnp.arange(D_RWKV) // HEAD
    w = dict(
        wup1=w_ffn1_up.astype(BF16), wdn1=w_ffn1_down.astype(BF16),
        wup2=w_ffn2_up.astype(BF16), wdn2=w_ffn2_down.astype(BF16),
        w_in=w_in.astype(BF16), w_out=w_out.astype(BF16),
        wd=jnp.concatenate([w_decay_up.astype(BF16), zeros64], axis=0),
        wi=jnp.concatenate([zeros64, w_iclr_up.astype(BF16)], axis=0),
        wg=w_gate_up.astype(BF16),
        mu=row(mu_shift), w0=row(w0), a0=row(a0), k_k=row(k_k), k_a=row(k_a), r_k=row(r_k),
        ln_x_g=row(ln_x_g), ln_x_b=row(ln_x_b), conv_w=conv_w, conv_b=row(conv_b),
        conv_ln_g=row(conv_ln_g), conv_ln_b=row(conv_ln_b),
        g_pre0=g_pre[0:1], g_pre1=g_pre[1:2], g_pre2=g_pre[2:3],
        g_post0=g_post[0:1], g_post1=g_post[1:2], g_post2=g_post[2:3],
        ebd=(head_of_lane[:, None] == head_of_lane[None, :]).astype(BF16),
    )

    ada = _ada_call(jnp.concatenate([c_prompt, c_sample], axis=0), w_ada, b_ada)
    ada = ada.reshape(bp + bs, N_SUB * 3, D_MODEL)
    mods_p = [ada[:bp, i][:, None, :] for i in range(N_SUB * 3)]
    mods_s = [jnp.tile(ada[bp:, i], (dec, 1)) for i in range(N_SUB * 3)]

    yp, wkv_p, shift_p, u_p = _layer(
        x_prompt.reshape(bp * seq, D_MODEL), mods_p, w,
        jnp.zeros((bp, 1, N_SHIFT), F32), jnp.zeros((bp, HALO, D_CONV), F32),
        jnp.zeros((bp * N_HEADS, HEAD, HEAD), F32),
        nb=bp, seq=seq, tm=512, tm_pre=256, stride=1, head_major=True, tag="p")
    y_p = yp.reshape(bp, seq, D_MODEL)
    wkv_p = wkv_p.reshape(bp, N_HEADS, HEAD, HEAD)
    conv_p = u_p.reshape(bp, seq, D_CONV)[:, seq - (CONV_W - 1):, :]

    xs_tok = x_sample.transpose(1, 0, 2).reshape(dec * bs, D_MODEL)
    hist_s = state_conv.transpose(1, 0, 2).reshape((CONV_W - 1) * bs, D_CONV)
    ys, wkv_s, shift_s, u_s = _layer(
        xs_tok, mods_s, w, state_shift.reshape(bs, N_SHIFT), hist_s,
        state_wkv.reshape(bs * N_HEADS, HEAD, HEAD),
        nb=1, seq=dec * bs, tm=dec * bs, tm_pre=dec * bs, stride=bs, head_major=False, tag="s")
    y_s = ys.reshape(dec, bs, D_MODEL).transpose(1, 0, 2)
    wkv_s = wkv_s.reshape(bs, N_HEADS, HEAD, HEAD)
    shift_s = shift_s.reshape(bs, 1, N_SHIFT)
    conv_s = jnp.concatenate([state_conv, u_s.reshape(dec, bs, D_CONV).transpose(1, 0, 2)],
                             axis=1)[:, dec:, :]
    return (y_p, y_s, wkv_p, shift_p, conv_p, wkv_s, shift_s, conv_s)
```

```python
import functools

import jax
import jax.numpy as jnp
from jax import lax
from jax.experimental import pallas as pl
from jax.experimental.pallas import tpu as pltpu

F32 = jnp.float32
BF16 = jnp.bfloat16

D_MODEL = 1024
D_RWKV = 512
HEAD = 64
N_HEADS = D_RWKV // HEAD
D_CONV = D_MODEL - D_RWKV
CONV_W = 31
DECAY_RANK = 64
ICLR_RANK = 64
GATE_RANK = 128
D_FF = 2816
N_SHIFT = 3 * D_RWKV + DECAY_RANK + ICLR_RANK + GATE_RANK
D_IN = N_SHIFT + 2 * D_CONV
N_SUB = 3
RMS_EPS = 1e-6
LN_EPS = 1e-5
GN_EPS = 64e-5

FF_CHUNK = 256
WKV_ROWS = 256
CONV_ROWS = 128
WKV_WAYS = 4
HALO = 32
VMEM_LIMIT = 56 * 1024 * 1024

NN = (((1,), (0,)), ((), ()))
NT = (((1,), (1,)), ((), ()))
B_NT = (((2,), (2,)), ((0,), (0,)))
B_TN = (((1,), (1,)), ((0,), (0,)))


def _splits(x, n):
    if x.dtype == BF16:
        return [x]
    parts, rem = [], x
    for i in range(n):
        p = rem.astype(BF16)
        parts.append(p)
        if i + 1 < n:
            rem = rem - p.astype(F32)
    return parts


def _mm(a, b, dn, na=2, nb=2):
    ap, bp = _splits(a, na), _splits(b, nb)
    depth = max(len(ap), len(bp))
    out = None
    for i, x in enumerate(ap):
        for j, y in enumerate(bp):
            if i + j >= depth:
                continue
            t = lax.dot_general(x, y, dn, preferred_element_type=F32)
            out = t if out is None else out + t
    return out


def _rms(x, g):
    return x * lax.rsqrt(jnp.mean(x * x, axis=-1, keepdims=True) + RMS_EPS) * g


def _sigmoid(x):
    return jax.nn.sigmoid(x)


def _head_sum(x, ebd):
    return _mm(x, ebd, NN, na=2, nb=1)


def _ada_body(c_ref, w_ref, b_ref, o_ref):
    c = c_ref[...]
    s = (c * _sigmoid(c)).astype(BF16)
    o_ref[...] = jnp.dot(s, w_ref[...].astype(BF16), preferred_element_type=F32) + b_ref[...]


def _ada_call(c_all, w_ada, b_ada):
    n, tn = c_all.shape[0], 1152
    width = w_ada.shape[1]
    return pl.pallas_call(
        _ada_body,
        grid=(width // tn,),
        in_specs=[pl.BlockSpec((n, D_MODEL), lambda i: (0, 0)),
                  pl.BlockSpec((D_MODEL, tn), lambda i: (0, i)),
                  pl.BlockSpec((1, tn), lambda i: (0, i))],
        out_specs=pl.BlockSpec((n, tn), lambda i: (0, i)),
        out_shape=jax.ShapeDtypeStruct((n, width), F32),
        compiler_params=pltpu.CompilerParams(dimension_semantics=("arbitrary",),
                                             vmem_limit_bytes=VMEM_LIMIT),
        name="ada",
    )(c_all, w_ada, b_ada.reshape(1, width))


def _const_spec(shape):
    nd = len(shape)
    return pl.BlockSpec(shape, lambda b, j: (0,) * nd, pipeline_mode=pl.Buffered(1))


def _tok_spec(tm, width, tiles):
    return pl.BlockSpec((tm, width), lambda b, j: (b * tiles + j, 0))


def _mod_spec(arr, tm, tiles):
    if arr.ndim == 3:
        return pl.BlockSpec((None, 1, D_MODEL), lambda b, j: (b, 0, 0))
    return pl.BlockSpec(arr.shape, lambda b, j: (0, 0))


def _mod_rows(ref, tm):
    m = ref[...]
    reps = tm // m.shape[0] if m.shape[0] > 1 else 1
    return jnp.concatenate([m] * reps, axis=0) if reps > 1 else m


def _params():
    return pltpu.CompilerParams(dimension_semantics=("arbitrary", "arbitrary"),
                                vmem_limit_bytes=VMEM_LIMIT)


def _ffn_body(x_ref, sh_ref, sc_ref, gt_ref, gpre_ref, gpost_ref, wup_ref, wdn_ref, o_ref, act_scr, *, res_w):
    x = x_ref[...]
    tm = x.shape[0]
    h = _rms(x, gpre_ref[...]) * (1.0 + _mod_rows(sc_ref, tm)) + _mod_rows(sh_ref, tm)
    hb = h.astype(BF16)
    for j in range(D_FF // FF_CHUNK):
        lo = j * FF_CHUNK
        g = jnp.dot(hb, wup_ref[:, lo:lo + FF_CHUNK], preferred_element_type=F32)
        u = jnp.dot(hb, wup_ref[:, D_FF + lo:D_FF + lo + FF_CHUNK], preferred_element_type=F32)
        act_scr[:, lo:lo + FF_CHUNK] = (g * _sigmoid(g) * u).astype(BF16)
    out = jnp.dot(act_scr[...], wdn_ref[...], preferred_element_type=F32)
    o_ref[...] = x + res_w * _mod_rows(gt_ref, tm) * _rms(out, gpost_ref[...])


def _ffn_call(x, mods, gpre, gpost, wup, wdn, *, nb, tiles, tm, res_w, name):
    sh, sc, gt = mods
    return pl.pallas_call(
        functools.partial(_ffn_body, res_w=res_w),
        grid=(nb, tiles),
        in_specs=[_tok_spec(tm, D_MODEL, tiles),
                  _mod_spec(sh, tm, tiles), _mod_spec(sc, tm, tiles), _mod_spec(gt, tm, tiles),
                  _const_spec((1, D_MODEL)), _const_spec((1, D_MODEL)),
                  _const_spec((D_MODEL, 2 * D_FF)), _const_spec((D_FF, D_MODEL))],
        out_specs=_tok_spec(tm, D_MODEL, tiles),
        out_shape=jax.ShapeDtypeStruct(x.shape, F32),
        scratch_shapes=[pltpu.VMEM((tm, D_FF), BF16)],
        compiler_params=_params(),
        name=name,
    )(x, sh, sc, gt, gpre, gpost, wup, wdn)


def _pre_body(x_ref, sh_ref, sc_ref, gpre_ref, win_ref, mu_ref, w0_ref, wd_ref, a0_ref, wi_ref, wg_ref,
              kk_ref, ka_ref, rk_ref, ebd_ref, shift0_ref,
              r_ref, k_ref, v_ref, a_ref, b_ref, ld_ref, gate_ref, u_ref, bonus_ref, nshift_ref,
              carry_scr, *, tm, stride, head_major):
    x = x_ref[...]
    h = _rms(x, gpre_ref[...]) * (1.0 + _mod_rows(sc_ref, tm)) + _mod_rows(sh_ref, tm)
    proj = jnp.dot(h.astype(BF16), win_ref[...], preferred_element_type=F32)
    p_sh = proj[:, :N_SHIFT]
    p_cv = proj[:, N_SHIFT:]

    if stride == 1:
        @pl.when(pl.program_id(1) == 0)
        def _():
            carry_scr[...] = shift0_ref[...]
        row = lax.broadcasted_iota(jnp.int32, (tm, 1), 0)
        prev = jnp.where(row == 0, carry_scr[...], pltpu.roll(p_sh, 1, axis=0))
        carry_scr[...] = p_sh[tm - 1:tm, :]
    else:
        prev = jnp.concatenate([shift0_ref[...], p_sh[:tm - stride, :]], axis=0)
    nshift_ref[...] = p_sh[tm - stride:tm, :]
    xs = p_sh + (prev - p_sh) * mu_ref[...]

    o1, o2, o3 = D_RWKV, 2 * D_RWKV, 3 * D_RWKV
    r, k, v = xs[:, :o1], xs[:, o1:o2], xs[:, o2:o3]
    dwda = xs[:, o3:o3 + DECAY_RANK + ICLR_RANK]
    dg = xs[:, o3 + DECAY_RANK + ICLR_RANK:]

    zdec = -(w0_ref[...] + jnp.dot(jnp.tanh(dwda).astype(BF16), wd_ref[...], preferred_element_type=F32))
    softplus = jnp.maximum(zdec, 0.0) + jnp.log1p(jnp.exp(-jnp.abs(zdec)))
    ld = -jnp.exp(-softplus - 0.5)
    iclr = _sigmoid(a0_ref[...] + jnp.dot(dwda.astype(BF16), wi_ref[...], preferred_element_type=F32))
    gate = jnp.dot(_sigmoid(dg).astype(BF16), wg_ref[...], preferred_element_type=F32)

    ebd = ebd_ref[...]
    kk = k * kk_ref[...]
    kk = kk / jnp.maximum(jnp.sqrt(_head_sum(kk * kk, ebd)), 1e-12)
    kmod = k * (1.0 + (iclr - 1.0) * ka_ref[...])
    bonus = _head_sum(r * kmod * rk_ref[...], ebd) * v

    gate_ref[...] = gate
    bonus_ref[...] = bonus
    u_ref[...] = p_cv[:, :D_CONV] * _sigmoid(p_cv[:, D_CONV:])
    outs = ((r_ref, r), (k_ref, kmod), (v_ref, v), (a_ref, -kk), (b_ref, kk * iclr), (ld_ref, ld))
    for ref, val in outs:
        if head_major:
            for hh in range(N_HEADS):
                ref[hh] = val[:, hh * HEAD:(hh + 1) * HEAD]
        else:
            ref[...] = val


def _pre_call(x1, mods, gpre, w, shift0, *, nb, tiles, tm, stride, head_major, name):
    sh, sc = mods
    n = x1.shape[0]
    if head_major:
        seq = tiles * tm
        hm_spec = pl.BlockSpec((None, N_HEADS, tm, HEAD), lambda b, j: (b, 0, j, 0))
        hm_shape = jax.ShapeDtypeStruct((nb, N_HEADS, seq, HEAD), F32)
    else:
        hm_spec = _tok_spec(tm, D_RWKV, tiles)
        hm_shape = jax.ShapeDtypeStruct((n, D_RWKV), F32)
    ld_spec = _tok_spec(tm, D_RWKV, tiles)
    ld_shape = jax.ShapeDtypeStruct((n, D_RWKV), F32)
    if stride == 1:
        shift_spec = pl.BlockSpec((None, 1, N_SHIFT), lambda b, j: (b, 0, 0))
        nshift_shape = jax.ShapeDtypeStruct((nb, 1, N_SHIFT), F32)
    else:
        shift_spec = pl.BlockSpec((stride, N_SHIFT), lambda b, j: (0, 0))
        nshift_shape = jax.ShapeDtypeStruct((stride, N_SHIFT), F32)
    return pl.pallas_call(
        functools.partial(_pre_body, tm=tm, stride=stride, head_major=head_major),
        grid=(nb, tiles),
        in_specs=[_tok_spec(tm, D_MODEL, tiles), _mod_spec(sh, tm, tiles), _mod_spec(sc, tm, tiles),
                  _const_spec((1, D_MODEL)), _const_spec((D_MODEL, D_IN)), _const_spec((1, N_SHIFT)),
                  _const_spec((1, D_RWKV)), _const_spec((DECAY_RANK + ICLR_RANK, D_RWKV)),
                  _const_spec((1, D_RWKV)), _const_spec((DECAY_RANK + ICLR_RANK, D_RWKV)),
                  _const_spec((GATE_RANK, D_RWKV)),
                  _const_spec((1, D_RWKV)), _const_spec((1, D_RWKV)), _const_spec((1, D_RWKV)),
                  _const_spec((D_RWKV, D_RWKV)), shift_spec],
        out_specs=[hm_spec] * 6 + [ld_spec] * 3 + [shift_spec],
        out_shape=[hm_shape] * 6 + [ld_shape] * 3 + [nshift_shape],
        scratch_shapes=[pltpu.VMEM((1, N_SHIFT), F32)],
        compiler_params=_params(),
        name=name,
    )(x1, sh, sc, gpre, w["w_in"], w["mu"], w["w0"], w["wd"], w["a0"], w["wi"], w["wg"],
      w["k_k"], w["k_a"], w["r_k"], w["ebd"], shift0)


def _wkv_body(r_ref, k_ref, v_ref, a_ref, b_ref, ld_ref, s0_ref, y_ref, st_ref, s_scr, *, chunk, n_sub, ways):
    c = pl.program_id(1)
    rows = WKV_ROWS
    gs = rows // chunk
    shift = chunk.bit_length() - 1

    @pl.when(c == 0)
    def _():
        s_scr[...] = s0_ref[...]

    ri = lax.broadcasted_iota(jnp.int32, (rows, rows), 0)
    ci = lax.broadcasted_iota(jnp.int32, (rows, rows), 1)
    same = jnp.right_shift(ri, shift) == jnp.right_shift(ci, shift)
    strict = same & (ci < ri)
    incl = same & (ci <= ri)
    tri = incl.astype(BF16)
    eye = (ri == ci).astype(F32)

    three = lambda z: z.reshape(gs, chunk, HEAD)
    each = lambda f, *xs: [f(*t) for t in zip(*xs)]

    def sub(it, carry):
        sls = [pl.ds(pl.multiple_of((it * ways + q) * gs, gs), gs) for q in range(ways)]
        load = lambda ref: [ref[sl].reshape(rows, HEAD) for sl in sls]
        r, k, v, a, b, ld = (load(ref) for ref in (r_ref, k_ref, v_ref, a_ref, b_ref, ld_ref))
        cs = each(lambda x: _mm(tri, x, NN, na=1, nb=2), ld)
        e_inv = each(lambda x: jnp.exp(-x), cs)
        at = each(lambda x, c_, l_: x * jnp.exp(c_ - l_), a, cs, ld)
        rt = each(lambda x, c_: x * jnp.exp(c_), r, cs)
        bt = each(lambda x, e: x * e, b, e_inv)
        kt = each(lambda x, e: x * e, k, e_inv)
        gram = each(lambda p, q, m, n: _mm(jnp.concatenate([p, q], axis=0), jnp.concatenate([m, n], axis=0),
                                           NT, 1, 1), at, rt, bt, kt)
        t_ab = each(lambda g: jnp.where(strict, g[:rows, :rows], 0.0), gram)
        t_ak = each(lambda g: jnp.where(strict, g[:rows, rows:], 0.0), gram)
        m_rb = each(lambda g: jnp.where(incl, g[rows:, :rows], 0.0), gram)
        m_rk = each(lambda g: jnp.where(incl, g[rows:, rows:], 0.0), gram)
        inv = each(lambda t: eye + t, t_ab)
        pw = t_ab
        for _ in range(shift - 1):
            pw = each(lambda p: _mm(p, p, NN, 1, 1), pw)
            inv = each(lambda x, p: x + _mm(p, x, NN, 1, 1), inv, pw)
        s0 = [s_scr[sl] for sl in sls]
        uy0 = each(lambda p, q, s: _mm(jnp.concatenate([three(p), three(q)], axis=1), s, B_NT, 1, 1), at, rt, s0)
        z = each(lambda o, t, x: o[:, :chunk, :].reshape(rows, HEAD) + _mm(t, x, NN, 1, 1), uy0, t_ak, v)
        u = each(lambda x, z_: _mm(x, z_, NN, 1, 1), inv, z)
        res = each(lambda z_, u_, t: z_ - u_ + _mm(t, u_, NN), z, u, t_ab)
        u = each(lambda u_, x, r_: u_ + _mm(x, r_, NN, 1, 1), u, inv, res)
        y = each(lambda o, m, u_, n, x: o[:, chunk:, :].reshape(rows, HEAD) + _mm(m, u_, NN, 1, 1)
                 + _mm(n, x, NN, 1, 1), uy0, m_rb, u, m_rk, v)
        for sl, y_ in zip(sls, y):
            y_ref[sl] = three(y_)
        cs3 = each(three, cs)
        c_last = each(lambda x: x[:, chunk - 1:chunk, :], cs3)
        tail = each(lambda cl, x: jnp.exp(cl - x), c_last, cs3)
        upd = each(lambda u_, x, b_, k_, t: _mm(jnp.concatenate([three(u_), three(x)], axis=1),
                                                jnp.concatenate([three(b_) * t, three(k_) * t], axis=1), B_TN, 1, 1),
                   u, v, b, k, tail)
        for sl, s, cl, d in zip(sls, s0, c_last, upd):
            s_scr[sl] = s * jnp.exp(cl) + d
        return carry

    lax.fori_loop(0, n_sub // ways, sub, 0)

    @pl.when(c == pl.num_programs(1) - 1)
    def _():
        st_ref[...] = s_scr[...]


def _wkv_call(ops, s0, *, gblk, chunk, name):
    ng, t_total, _ = ops[0].shape
    gs = WKV_ROWS // chunk
    op_spec = pl.BlockSpec((gblk, chunk, HEAD), lambda g, c: (g, c, 0))
    st_spec = pl.BlockSpec((gblk, HEAD, HEAD), lambda g, c: (g, 0, 0))
    return pl.pallas_call(
        functools.partial(_wkv_body, chunk=chunk, n_sub=gblk // gs, ways=WKV_WAYS),
        grid=(ng // gblk, t_total // chunk),
        in_specs=[op_spec] * 6 + [st_spec],
        out_specs=[op_spec, st_spec],
        out_shape=[jax.ShapeDtypeStruct((ng, t_total, HEAD), F32),
                   jax.ShapeDtypeStruct((ng, HEAD, HEAD), F32)],
        scratch_shapes=[pltpu.VMEM((gblk, HEAD, HEAD), F32)],
        compiler_params=_params(),
        name=name,
    )(*ops, s0)


def _post_body(y_ref, bonus_ref, gate_ref, u_ref, hist_ref, x_ref, gt_ref,
               lng_ref, lnb_ref, cw_ref, cb_ref, clg_ref, clb_ref, wout_ref, gpost_ref, ebd_ref,
               o_ref, full_scr, *, tm, stride, head_major):
    if head_major:
        y = jnp.concatenate([y_ref[hh] for hh in range(N_HEADS)], axis=-1)
    else:
        y = y_ref[...]
    ebd = ebd_ref[...]
    mu = _head_sum(y, ebd) * (1.0 / HEAD)
    d = y - mu
    var = _head_sum(d * d, ebd) * (1.0 / HEAD)
    yn = d * lax.rsqrt(var + GN_EPS) * lng_ref[...] + lnb_ref[...]
    y_rwkv = (yn + bonus_ref[...]) * gate_ref[...]

    hist = full_scr.shape[0] - tm
    if stride == 1:
        @pl.when(pl.program_id(1) == 0)
        def _():
            full_scr[:hist, :] = hist_ref[...]
    else:
        full_scr[:hist, :] = hist_ref[...]
    u = u_ref[...]
    full_scr[hist:, :] = u
    base = hist - (CONV_W - 1) * stride
    offs = [base + j * stride for j in range(CONV_W)]
    if stride % 8 == 0:
        z = jnp.zeros((tm, D_CONV), F32)
        for j in range(CONV_W):
            z = z + full_scr[pl.ds(offs[j], tm), :] * cw_ref[j:j + 1, :]
    else:
        cols = []
        for lo in range(0, D_CONV, 128):
            pieces = []
            for t0 in range(0, tm, CONV_ROWS):
                zc = None
                for rsd in range(8):
                    n = CONV_ROWS + (8 if rsd else 0)
                    part = None
                    for j in [j for j in range(CONV_W) if offs[j] % 8 == rsd]:
                        term = (full_scr[pl.ds(t0 + offs[j] - rsd, n), lo:lo + 128]
                                * cw_ref[j:j + 1, lo:lo + 128])
                        part = term if part is None else part + term
                    if rsd:
                        part = pltpu.roll(part, n - rsd, axis=0)[:CONV_ROWS, :]
                    zc = part if zc is None else zc + part
                pieces.append(zc)
            cols.append(jnp.concatenate(pieces, axis=0))
        z = jnp.concatenate(cols, axis=1)
    if stride == 1:
        full_scr[:hist, :] = u[tm - hist:, :]
    z = z + cb_ref[...]
    zm = jnp.mean(z, axis=-1, keepdims=True)
    zc = z - zm
    zv = jnp.mean(zc * zc, axis=-1, keepdims=True)
    zn = zc * lax.rsqrt(zv + LN_EPS) * clg_ref[...] + clb_ref[...]
    zn = zn * _sigmoid(zn)

    mixed = (jnp.dot(y_rwkv.astype(BF16), wout_ref[:D_RWKV, :], preferred_element_type=F32)
             + jnp.dot(zn.astype(BF16), wout_ref[D_RWKV:, :], preferred_element_type=F32))
    o_ref[...] = x_ref[...] + _mod_rows(gt_ref, tm) * _rms(mixed, gpost_ref[...])


def _post_call(y, bonus, gate, u, hist, x1, gt, w, *, nb, tiles, tm, stride, head_major, name):
    if head_major:
        y_spec = pl.BlockSpec((None, N_HEADS, tm, HEAD), lambda b, j: (b, 0, j, 0))
        hist_spec = pl.BlockSpec((None, HALO, D_CONV), lambda b, j: (b, 0, 0))
        hist_rows = HALO
    else:
        y_spec = _tok_spec(tm, D_RWKV, tiles)
        hist_rows = hist.shape[0]
        hist_spec = pl.BlockSpec((hist_rows, D_CONV), lambda b, j: (0, 0))
    tok512 = _tok_spec(tm, D_RWKV, tiles)
    row512 = _const_spec((1, D_RWKV))
    return pl.pallas_call(
        functools.partial(_post_body, tm=tm, stride=stride, head_major=head_major),
        grid=(nb, tiles),
        in_specs=[y_spec, tok512, tok512, tok512, hist_spec, _tok_spec(tm, D_MODEL, tiles),
                  _mod_spec(gt, tm, tiles),
                  row512, row512, _const_spec((CONV_W, D_CONV)), row512, row512, row512,
                  _const_spec((D_MODEL, D_MODEL)), _const_spec((1, D_MODEL)),
                  _const_spec((D_RWKV, D_RWKV))],
        out_specs=_tok_spec(tm, D_MODEL, tiles),
        out_shape=jax.ShapeDtypeStruct(x1.shape, F32),
        scratch_shapes=[pltpu.VMEM((hist_rows + tm, D_CONV), F32)],
        compiler_params=_params(),
        name=name,
    )(y, bonus, gate, u, hist, x1, gt, w["ln_x_g"], w["ln_x_b"], w["conv_w"], w["conv_b"],
      w["conv_ln_g"], w["conv_ln_b"], w["w_out"], w["g_post1"], w["ebd"])


def _layer(x, mods, w, shift0, hist, s0, *, nb, seq, tm, tm_pre, stride, head_major, tag):
    tiles = (seq // tm) if head_major else 1
    tiles_pre = (seq // tm_pre) if head_major else 1
    x1 = _ffn_call(x, mods[0:3], w["g_pre0"], w["g_post0"], w["wup1"], w["wdn1"],
                   nb=nb, tiles=tiles, tm=tm, res_w=0.5, name=tag + "_ffn1")
    r, k, v, a, b, ld, gate, u, bonus, nshift = _pre_call(
        x1, mods[3:5], w["g_pre1"], w, shift0, nb=nb, tiles=tiles_pre, tm=tm_pre, stride=stride,
        head_major=head_major, name=tag + "_mixpre")
    if head_major:
        ops = [z.reshape(nb * N_HEADS, seq, HEAD) for z in (r, k, v, a, b, ld)]
        y, s_new = _wkv_call(ops, s0, gblk=nb * N_HEADS, chunk=32, name=tag + "_wkv")
        y = y.reshape(nb, N_HEADS, seq, HEAD)
    else:
        t_len = x.shape[0] // stride

        def to_groups(z):
            z = z.reshape(t_len, stride, N_HEADS, HEAD).transpose(1, 2, 0, 3)
            z = jnp.pad(z, ((0, 0), (0, 0), (0, 8 - t_len), (0, 0)))
            return z.reshape(stride * N_HEADS, 8, HEAD)

        ops = [to_groups(z) for z in (r, k, v, a, b, ld)]
        y, s_new = _wkv_call(ops, s0, gblk=128, chunk=8, name=tag + "_wkv")
        y = y.reshape(stride, N_HEADS, 8, HEAD)[:, :, :t_len].transpose(2, 0, 1, 3).reshape(x.shape[0], D_RWKV)
    x2 = _post_call(y, bonus, gate, u, hist, x1, mods[5], w, nb=nb, tiles=tiles, tm=tm, stride=stride,
                    head_major=head_major, name=tag + "_mixpost")
    out = _ffn_call(x2, mods[6:9], w["g_pre2"], w["g_post2"], w["wup2"], w["wdn2"],
                    nb=nb, tiles=tiles, tm=tm, res_w=0.5, name=tag + "_ffn2")
    return out, s_new, nshift, u


def kernel(x_prompt, x_sample, c_prompt, c_sample, state_wkv, state_shift, state_conv, w_ada, b_ada, g_pre, g_post, w_ffn1_up, w_ffn1_down, w_in, mu_shift, w0, w_decay_up, a0, w_iclr_up, w_gate_up, k_k, k_a, r_k, ln_x_g, ln_x_b, conv_w, conv_b, conv_ln_g, conv_ln_b, w_out, w_ffn2_up, w_ffn2_down):
    bp, seq, _ = x_prompt.shape
    bs, dec, _ = x_sample.shape
    row = lambda z: z.reshape(1, -1).astype(F32)
    zeros64 = jnp.zeros((ICLR_RANK, D_RWKV), BF16)
    head_of_lane = jnp.arange(D_RWKV) // HEAD
    w = dict(
        wup1=w_ffn1_up.astype(BF16), wdn1=w_ffn1_down.astype(BF16),
        wup2=w_ffn2_up.astype(BF16), wdn2=w_ffn2_down.astype(BF16),
        w_in=w_in.astype(BF16), w_out=w_out.astype(BF16),
        wd=jnp.concatenate([w_decay_up.astype(BF16), zeros64], axis=0),
        wi=jnp.concatenate([zeros64, w_iclr_up.astype(BF16)], axis=0),
        wg=w_gate_up.astype(BF16),
        mu=row(mu_shift), w0=row(w0), a0=row(a0), k_k=row(k_k), k_a=row(k_a), r_k=row(r_k),
        ln_x_g=row(ln_x_g), ln_x_b=row(ln_x_b), conv_w=conv_w, conv_b=row(conv_b),
        conv_ln_g=row(conv_ln_g), conv_ln_b=row(conv_ln_b),
        g_pre0=g_pre[0:1], g_pre1=g_pre[1:2], g_pre2=g_pre[2:3],
        g_post0=g_post[0:1], g_post1=g_post[1:2], g_post2=g_post[2:3],
        ebd=(head_of_lane[:, None] == head_of_lane[None, :]).astype(BF16),
    )

    ada = _ada_call(jnp.concatenate([c_prompt, c_sample], axis=0), w_ada, b_ada)
    ada = ada.reshape(bp + bs, N_SUB * 3, D_MODEL)
    mods_p = [ada[:bp, i][:, None, :] for i in range(N_SUB * 3)]
    mods_s = [ada[bp:, i] for i in range(N_SUB * 3)]

    yp, wkv_p, shift_p, u_p = _layer(
        x_prompt.reshape(bp * seq, D_MODEL), mods_p, w,
        jnp.zeros((bp, 1, N_SHIFT), F32), jnp.zeros((bp, HALO, D_CONV), F32),
        jnp.zeros((bp * N_HEADS, HEAD, HEAD), F32),
        nb=bp, seq=seq, tm=512, tm_pre=256, stride=1, head_major=True, tag="p")
    y_p = yp.reshape(bp, seq, D_MODEL)
    wkv_p = wkv_p.reshape(bp, N_HEADS, HEAD, HEAD)
    conv_p = u_p.reshape(bp, seq, D_CONV)[:, seq - (CONV_W - 1):, :]

    xs_tok = x_sample.transpose(1, 0, 2).reshape(dec * bs, D_MODEL)
    hist_s = state_conv.transpose(1, 0, 2).reshape((CONV_W - 1) * bs, D_CONV)
    ys, wkv_s, shift_s, u_s = _layer(
        xs_tok, mods_s, w, state_shift.reshape(bs, N_SHIFT), hist_s,
        state_wkv.reshape(bs * N_HEADS, HEAD, HEAD),
        nb=1, seq=dec * bs, tm=dec * bs, tm_pre=dec * bs, stride=bs, head_major=False, tag="s")
    y_s = ys.reshape(dec, bs, D_MODEL).transpose(1, 0, 2)
    wkv_s = wkv_s.reshape(bs, N_HEADS, HEAD, HEAD)
    shift_s = shift_s.reshape(bs, 1, N_SHIFT)
    conv_s = jnp.concatenate([state_conv, u_s.reshape(dec, bs, D_CONV).transpose(1, 0, 2)],
                             axis=1)[:, dec:, :]
    return (y_p, y_s, wkv_p, shift_p, conv_p, wkv_s, shift_s, conv_s)
```

```python
import functools

import jax
import jax.numpy as jnp
from jax import lax
from jax.experimental import pallas as pl
from jax.experimental.pallas import tpu as pltpu

F32 = jnp.float32
BF16 = jnp.bfloat16

D_MODEL = 1024
D_RWKV = 512
HEAD = 64
PAIR = 2 * HEAD
N_PAIRS = D_RWKV // PAIR
N_HEADS = D_RWKV // HEAD
D_CONV = D_MODEL - D_RWKV
CONV_W = 31
DECAY_RANK = 64
ICLR_RANK = 64
GATE_RANK = 128
D_FF = 2816
N_SHIFT = 3 * D_RWKV + DECAY_RANK + ICLR_RANK + GATE_RANK
D_IN = N_SHIFT + 2 * D_CONV
N_SUB = 3
RMS_EPS = 1e-6
LN_EPS = 1e-5
GN_EPS = 64e-5

FF_CHUNK = 256
WKV_ROWS = 256
CONV_ROWS = 128
WKV_WAYS = 4
HALO = 32
VMEM_LIMIT = 56 * 1024 * 1024

NN = (((1,), (0,)), ((), ()))
NT = (((1,), (1,)), ((), ()))
B_NT = (((2,), (2,)), ((0,), (0,)))
B_TN = (((1,), (1,)), ((0,), (0,)))


def _splits(x, n):
    if x.dtype == BF16:
        return [x]
    parts, rem = [], x
    for i in range(n):
        p = rem.astype(BF16)
        parts.append(p)
        if i + 1 < n:
            rem = rem - p.astype(F32)
    return parts


def _mm(a, b, dn, na=2, nb=2):
    ap, bp = _splits(a, na), _splits(b, nb)
    depth = max(len(ap), len(bp))
    out = None
    for i, x in enumerate(ap):
        for j, y in enumerate(bp):
            if i + j >= depth:
                continue
            t = lax.dot_general(x, y, dn, preferred_element_type=F32)
            out = t if out is None else out + t
    return out


def _rms(x, g):
    return x * lax.rsqrt(jnp.mean(x * x, axis=-1, keepdims=True) + RMS_EPS) * g


def _sigmoid(x):
    return jax.nn.sigmoid(x)


def _head_sum(x, ebd):
    return _mm(x, ebd, NN, na=2, nb=1)


def _ada_body(c_ref, w_ref, b_ref, o_ref):
    c = c_ref[...]
    s = (c * _sigmoid(c)).astype(BF16)
    o_ref[...] = jnp.dot(s, w_ref[...].astype(BF16), preferred_element_type=F32) + b_ref[...]


def _ada_call(c_all, w_ada, b_ada):
    n, tn = c_all.shape[0], 1152
    width = w_ada.shape[1]
    return pl.pallas_call(
        _ada_body,
        grid=(width // tn,),
        in_specs=[pl.BlockSpec((n, D_MODEL), lambda i: (0, 0)),
                  pl.BlockSpec((D_MODEL, tn), lambda i: (0, i)),
                  pl.BlockSpec((1, tn), lambda i: (0, i))],
        out_specs=pl.BlockSpec((n, tn), lambda i: (0, i)),
        out_shape=jax.ShapeDtypeStruct((n, width), F32),
        compiler_params=pltpu.CompilerParams(dimension_semantics=("arbitrary",),
                                             vmem_limit_bytes=VMEM_LIMIT),
        name="ada",
    )(c_all, w_ada, b_ada.reshape(1, width))


def _const_spec(shape):
    nd = len(shape)
    return pl.BlockSpec(shape, lambda b, j: (0,) * nd, pipeline_mode=pl.Buffered(1))


def _tok_spec(tm, width, tiles):
    return pl.BlockSpec((tm, width), lambda b, j: (b * tiles + j, 0))


def _mod_spec(arr, tm, tiles):
    if arr.ndim == 3:
        return pl.BlockSpec((None, 1, D_MODEL), lambda b, j: (b, 0, 0))
    return pl.BlockSpec(arr.shape, lambda b, j: (0, 0))


def _mod_rows(ref, tm):
    m = ref[...]
    reps = tm // m.shape[0] if m.shape[0] > 1 else 1
    return jnp.concatenate([m] * reps, axis=0) if reps > 1 else m


def _params():
    return pltpu.CompilerParams(dimension_semantics=("arbitrary", "arbitrary"),
                                vmem_limit_bytes=VMEM_LIMIT)


def _ffn_body(x_ref, sh_ref, sc_ref, gt_ref, gpre_ref, gpost_ref, wup_ref, wdn_ref, o_ref, act_scr, *, res_w):
    x = x_ref[...]
    tm = x.shape[0]
    h = _rms(x, gpre_ref[...]) * (1.0 + _mod_rows(sc_ref, tm)) + _mod_rows(sh_ref, tm)
    hb = h.astype(BF16)
    for j in range(D_FF // FF_CHUNK):
        lo = j * FF_CHUNK
        g = jnp.dot(hb, wup_ref[:, lo:lo + FF_CHUNK], preferred_element_type=F32)
        u = jnp.dot(hb, wup_ref[:, D_FF + lo:D_FF + lo + FF_CHUNK], preferred_element_type=F32)
        act_scr[:, lo:lo + FF_CHUNK] = (g * _sigmoid(g) * u).astype(BF16)
    out = jnp.dot(act_scr[...], wdn_ref[...], preferred_element_type=F32)
    o_ref[...] = x + res_w * _mod_rows(gt_ref, tm) * _rms(out, gpost_ref[...])


def _ffn_call(x, mods, gpre, gpost, wup, wdn, *, nb, tiles, tm, res_w, name):
    sh, sc, gt = mods
    return pl.pallas_call(
        functools.partial(_ffn_body, res_w=res_w),
        grid=(nb, tiles),
        in_specs=[_tok_spec(tm, D_MODEL, tiles),
                  _mod_spec(sh, tm, tiles), _mod_spec(sc, tm, tiles), _mod_spec(gt, tm, tiles),
                  _const_spec((1, D_MODEL)), _const_spec((1, D_MODEL)),
                  _const_spec((D_MODEL, 2 * D_FF)), _const_spec((D_FF, D_MODEL))],
        out_specs=_tok_spec(tm, D_MODEL, tiles),
        out_shape=jax.ShapeDtypeStruct(x.shape, F32),
        scratch_shapes=[pltpu.VMEM((tm, D_FF), BF16)],
        compiler_params=_params(),
        name=name,
    )(x, sh, sc, gt, gpre, gpost, wup, wdn)


def _pre_body(x_ref, sh_ref, sc_ref, gpre_ref, win_ref, mu_ref, w0_ref, wd_ref, a0_ref, wi_ref, wg_ref,
              kk_ref, ka_ref, rk_ref, ebd_ref, shift0_ref,
              r_ref, k_ref, v_ref, a_ref, b_ref, ld_ref, gate_ref, u_ref, bonus_ref, nshift_ref,
              carry_scr, *, tm, stride, head_major):
    x = x_ref[...]
    h = _rms(x, gpre_ref[...]) * (1.0 + _mod_rows(sc_ref, tm)) + _mod_rows(sh_ref, tm)
    proj = jnp.dot(h.astype(BF16), win_ref[...], preferred_element_type=F32)
    p_sh = proj[:, :N_SHIFT]
    p_cv = proj[:, N_SHIFT:]

    if stride == 1:
        @pl.when(pl.program_id(1) == 0)
        def _():
            carry_scr[...] = shift0_ref[...]
        row = lax.broadcasted_iota(jnp.int32, (tm, 1), 0)
        prev = jnp.where(row == 0, carry_scr[...], pltpu.roll(p_sh, 1, axis=0))
        carry_scr[...] = p_sh[tm - 1:tm, :]
    else:
        prev = jnp.concatenate([shift0_ref[...], p_sh[:tm - stride, :]], axis=0)
    nshift_ref[...] = p_sh[tm - stride:tm, :]
    xs = p_sh + (prev - p_sh) * mu_ref[...]

    o1, o2, o3 = D_RWKV, 2 * D_RWKV, 3 * D_RWKV
    r, k, v = xs[:, :o1], xs[:, o1:o2], xs[:, o2:o3]
    dwda = xs[:, o3:o3 + DECAY_RANK + ICLR_RANK]
    dg = xs[:, o3 + DECAY_RANK + ICLR_RANK:]

    zdec = -(w0_ref[...] + jnp.dot(jnp.tanh(dwda).astype(BF16), wd_ref[...], preferred_element_type=F32))
    softplus = jnp.maximum(zdec, 0.0) + jnp.log1p(jnp.exp(-jnp.abs(zdec)))
    ld = -jnp.exp(-softplus - 0.5)
    iclr = _sigmoid(a0_ref[...] + jnp.dot(dwda.astype(BF16), wi_ref[...], preferred_element_type=F32))
    gate = jnp.dot(_sigmoid(dg).astype(BF16), wg_ref[...], preferred_element_type=F32)

    ebd = ebd_ref[...]
    kk = k * kk_ref[...]
    kk = kk / jnp.maximum(jnp.sqrt(_head_sum(kk * kk, ebd)), 1e-12)
    kmod = k * (1.0 + (iclr - 1.0) * ka_ref[...])
    bonus = _head_sum(r * kmod * rk_ref[...], ebd) * v

    gate_ref[...] = gate
    bonus_ref[...] = bonus
    u_ref[...] = p_cv[:, :D_CONV] * _sigmoid(p_cv[:, D_CONV:])
    outs = ((r_ref, r), (k_ref, kmod), (v_ref, v), (a_ref, -kk), (b_ref, kk * iclr), (ld_ref, ld))
    for ref, val in outs:
        if head_major:
            for pr in range(N_PAIRS):
                ref[pr] = val[:, pr * PAIR:(pr + 1) * PAIR]
        else:
            ref[...] = val


def _pre_call(x1, mods, gpre, w, shift0, *, nb, tiles, tm, stride, head_major, name):
    sh, sc = mods
    n = x1.shape[0]
    if head_major:
        seq = tiles * tm
        hm_spec = pl.BlockSpec((None, N_PAIRS, tm, PAIR), lambda b, j: (b, 0, j, 0))
        hm_shape = jax.ShapeDtypeStruct((nb, N_PAIRS, seq, PAIR), F32)
    else:
        hm_spec = _tok_spec(tm, D_RWKV, tiles)
        hm_shape = jax.ShapeDtypeStruct((n, D_RWKV), F32)
    ld_spec = _tok_spec(tm, D_RWKV, tiles)
    ld_shape = jax.ShapeDtypeStruct((n, D_RWKV), F32)
    if stride == 1:
        shift_spec = pl.BlockSpec((None, 1, N_SHIFT), lambda b, j: (b, 0, 0))
        nshift_shape = jax.ShapeDtypeStruct((nb, 1, N_SHIFT), F32)
    else:
        shift_spec = pl.BlockSpec((stride, N_SHIFT), lambda b, j: (0, 0))
        nshift_shape = jax.ShapeDtypeStruct((stride, N_SHIFT), F32)
    return pl.pallas_call(
        functools.partial(_pre_body, tm=tm, stride=stride, head_major=head_major),
        grid=(nb, tiles),
        in_specs=[_tok_spec(tm, D_MODEL, tiles), _mod_spec(sh, tm, tiles), _mod_spec(sc, tm, tiles),
                  _const_spec((1, D_MODEL)), _const_spec((D_MODEL, D_IN)), _const_spec((1, N_SHIFT)),
                  _const_spec((1, D_RWKV)), _const_spec((DECAY_RANK + ICLR_RANK, D_RWKV)),
                  _const_spec((1, D_RWKV)), _const_spec((DECAY_RANK + ICLR_RANK, D_RWKV)),
                  _const_spec((GATE_RANK, D_RWKV)),
                  _const_spec((1, D_RWKV)), _const_spec((1, D_RWKV)), _const_spec((1, D_RWKV)),
                  _const_spec((D_RWKV, D_RWKV)), shift_spec],
        out_specs=[hm_spec] * 6 + [ld_spec] * 3 + [shift_spec],
        out_shape=[hm_shape] * 6 + [ld_shape] * 3 + [nshift_shape],
        scratch_shapes=[pltpu.VMEM((1, N_SHIFT), F32)],
        compiler_params=_params(),
        name=name,
    )(x1, sh, sc, gpre, w["w_in"], w["mu"], w["w0"], w["wd"], w["a0"], w["wi"], w["wg"],
      w["k_k"], w["k_a"], w["r_k"], w["ebd"], shift0)


def _wkv_body(r_ref, k_ref, v_ref, a_ref, b_ref, ld_ref, s0_ref, y_ref, st_ref, s_scr, *, chunk, n_sub, ways):
    c = pl.program_id(1)
    rows = WKV_ROWS
    pp = rows // (2 * chunk)
    shift = chunk.bit_length() - 1

    @pl.when(c == 0)
    def _():
        zero = jnp.zeros(s0_ref.shape[:1] + (HEAD, HEAD), F32)
        s_scr[:, :HEAD, :] = jnp.concatenate([s0_ref[:, 0], zero], axis=-1)
        s_scr[:, HEAD:, :] = jnp.concatenate([zero, s0_ref[:, 1]], axis=-1)

    ri = lax.broadcasted_iota(jnp.int32, (rows, rows), 0)
    ci = lax.broadcasted_iota(jnp.int32, (rows, rows), 1)
    same = jnp.right_shift(ri, shift) == jnp.right_shift(ci, shift)
    strict = same & (ci < ri)
    incl = same & (ci <= ri)
    eye = (ri == ci).astype(F32)
    first = lax.broadcasted_iota(jnp.int32, (1, 1, PAIR), 2) < HEAD
    tpos = lax.broadcasted_iota(jnp.int32, (pp * chunk, 1), 0) & (chunk - 1)

    def cumsum(x3):
        x = x3.reshape(pp * chunk, PAIR)
        for s in range(shift):
            d = 1 << s
            x = x + jnp.where(tpos >= d, pltpu.roll(x, d, axis=0), 0.0)
        return x.reshape(pp, chunk, PAIR)

    stack = lambda x3: jnp.concatenate([jnp.where(first, x3, 0.0), jnp.where(first, 0.0, x3)], axis=1)
    flat = lambda x3: x3.reshape(rows, PAIR)
    three = lambda x: x.reshape(pp, 2 * chunk, PAIR)
    each = lambda f, *xs: [f(*t) for t in zip(*xs)]

    def sub(it, carry):
        sls = [pl.ds(pl.multiple_of((it * ways + q) * pp, pp), pp) for q in range(ways)]
        r, k, v, a, b, ld = ([ref[sl] for sl in sls] for ref in (r_ref, k_ref, v_ref, a_ref, b_ref, ld_ref))
        cs = each(cumsum, ld)
        e_inv = each(lambda x: jnp.exp(-x), cs)
        at = each(lambda x, c_, l_: stack(x * jnp.exp(c_ - l_)), a, cs, ld)
        rt = each(lambda x, c_: stack(x * jnp.exp(c_)), r, cs)
        bt = each(lambda x, e: stack(x * e), b, e_inv)
        kt = each(lambda x, e: stack(x * e), k, e_inv)
        vs = each(stack, v)
        gram = each(lambda p, q, m, n: _mm(jnp.concatenate([flat(p), flat(q)], axis=0),
                                           jnp.concatenate([flat(m), flat(n)], axis=0), NT, 1, 1), at, rt, bt, kt)
        t_ab = each(lambda g: jnp.where(strict, g[:rows, :rows], 0.0).astype(BF16), gram)
        t_ak = each(lambda g: jnp.where(strict, g[:rows, rows:], 0.0), gram)
        m_rb = each(lambda g: jnp.where(incl, g[rows:, :rows], 0.0), gram)
        m_rk = each(lambda g: jnp.where(incl, g[rows:, rows:], 0.0), gram)
        inv = each(lambda t: eye + t.astype(F32), t_ab)
        pw = t_ab
        for _ in range(shift - 1):
            pw = each(lambda p: _mm(p, p, NN, 1, 1), pw)
            inv = each(lambda x, p: x + _mm(p, x, NN, 1, 1), inv, pw)
        s0 = [s_scr[sl] for sl in sls]
        uy0 = each(lambda p, q, s: _mm(jnp.concatenate([p, q], axis=1), s, B_NT, 1, 1), at, rt, s0)
        z = each(lambda o, t, x: flat(o[:, :2 * chunk, :]) + _mm(t, flat(x), NN, 1, 1), uy0, t_ak, vs)
        u = each(lambda x, z_: _mm(x, z_, NN, 1, 1), inv, z)
        res = each(lambda z_, u_, t: z_ - u_ + _mm(t, u_, NN, 1, 2), z, u, t_ab)
        u = each(lambda u_, x, r_: u_ + _mm(x, r_, NN, 1, 1), u, inv, res)
        y = each(lambda o, m, u_, n, x: three(flat(o[:, 2 * chunk:, :]) + _mm(m, u_, NN, 1, 1)
                                              + _mm(n, flat(x), NN, 1, 1)), uy0, m_rb, u, m_rk, vs)
        for sl, y_ in zip(sls, y):
            y_ref[sl] = y_[:, :chunk, :] + y_[:, chunk:, :]
        c_last = each(lambda x: x[:, chunk - 1:chunk, :], cs)
        tail = each(lambda cl, x: jnp.exp(cl - x), c_last, cs)
        upd = each(lambda u_, x, b_, k_, t: _mm(jnp.concatenate([three(u_), x], axis=1),
                                                jnp.concatenate([stack(b_ * t), stack(k_ * t)], axis=1), B_TN, 1, 1),
                   u, vs, b, k, tail)
        for sl, s, cl, d in zip(sls, s0, c_last, upd):
            s_scr[sl] = s * jnp.exp(cl) + d
        return carry

    lax.fori_loop(0, n_sub // ways, sub, 0)

    @pl.when(c == pl.num_programs(1) - 1)
    def _():
        st_ref[:, 0] = s_scr[:, :HEAD, :HEAD]
        st_ref[:, 1] = s_scr[:, HEAD:, HEAD:]


def _wkv_call(ops, s0, *, pblk, chunk, name):
    n_pairs, t_total, _ = ops[0].shape
    pp = WKV_ROWS // (2 * chunk)
    op_spec = pl.BlockSpec((pblk, chunk, PAIR), lambda g, c: (g, c, 0))
    st_spec = pl.BlockSpec((pblk, 2, HEAD, HEAD), lambda g, c: (g, 0, 0, 0))
    return pl.pallas_call(
        functools.partial(_wkv_body, chunk=chunk, n_sub=pblk // pp, ways=WKV_WAYS),
        grid=(n_pairs // pblk, t_total // chunk),
        in_specs=[op_spec] * 6 + [st_spec],
        out_specs=[op_spec, st_spec],
        out_shape=[jax.ShapeDtypeStruct((n_pairs, t_total, PAIR), F32),
                   jax.ShapeDtypeStruct((n_pairs, 2, HEAD, HEAD), F32)],
        scratch_shapes=[pltpu.VMEM((pblk, PAIR, PAIR), F32)],
        compiler_params=_params(),
        name=name,
    )(*ops, s0)


def _post_body(y_ref, bonus_ref, gate_ref, u_ref, hist_ref, x_ref, gt_ref,
               lng_ref, lnb_ref, cw_ref, cb_ref, clg_ref, clb_ref, wout_ref, gpost_ref, ebd_ref,
               o_ref, full_scr, *, tm, stride, head_major):
    if head_major:
        y = jnp.concatenate([y_ref[pr] for pr in range(N_PAIRS)], axis=-1)
    else:
        y = y_ref[...]
    ebd = ebd_ref[...]
    mu = _head_sum(y, ebd) * (1.0 / HEAD)
    d = y - mu
    var = _head_sum(d * d, ebd) * (1.0 / HEAD)
    yn = d * lax.rsqrt(var + GN_EPS) * lng_ref[...] + lnb_ref[...]
    y_rwkv = (yn + bonus_ref[...]) * gate_ref[...]

    hist = full_scr.shape[0] - tm
    if stride == 1:
        @pl.when(pl.program_id(1) == 0)
        def _():
            full_scr[:hist, :] = hist_ref[...]
    else:
        full_scr[:hist, :] = hist_ref[...]
    u = u_ref[...]
    full_scr[hist:, :] = u
    base = hist - (CONV_W - 1) * stride
    offs = [base + j * stride for j in range(CONV_W)]
    if stride % 8 == 0:
        z = jnp.zeros((tm, D_CONV), F32)
        for j in range(CONV_W):
            z = z + full_scr[pl.ds(offs[j], tm), :] * cw_ref[j:j + 1, :]
    else:
        cols = []
        for lo in range(0, D_CONV, 128):
            pieces = []
            for t0 in range(0, tm, CONV_ROWS):
                zc = None
                for rsd in range(8):
                    n = CONV_ROWS + (8 if rsd else 0)
                    part = None
                    for j in [j for j in range(CONV_W) if offs[j] % 8 == rsd]:
                        term = (full_scr[pl.ds(t0 + offs[j] - rsd, n), lo:lo + 128]
                                * cw_ref[j:j + 1, lo:lo + 128])
                        part = term if part is None else part + term
                    if rsd:
                        part = pltpu.roll(part, n - rsd, axis=0)[:CONV_ROWS, :]
                    zc = part if zc is None else zc + part
                pieces.append(zc)
            cols.append(jnp.concatenate(pieces, axis=0))
        z = jnp.concatenate(cols, axis=1)
    if stride == 1:
        full_scr[:hist, :] = u[tm - hist:, :]
    z = z + cb_ref[...]
    zm = jnp.mean(z, axis=-1, keepdims=True)
    zc = z - zm
    zv = jnp.mean(zc * zc, axis=-1, keepdims=True)
    zn = zc * lax.rsqrt(zv + LN_EPS) * clg_ref[...] + clb_ref[...]
    zn = zn * _sigmoid(zn)

    mixed = (jnp.dot(y_rwkv.astype(BF16), wout_ref[:D_RWKV, :], preferred_element_type=F32)
             + jnp.dot(zn.astype(BF16), wout_ref[D_RWKV:, :], preferred_element_type=F32))
    o_ref[...] = x_ref[...] + _mod_rows(gt_ref, tm) * _rms(mixed, gpost_ref[...])


def _post_call(y, bonus, gate, u, hist, x1, gt, w, *, nb, tiles, tm, stride, head_major, name):
    if head_major:
        y_spec = pl.BlockSpec((None, N_PAIRS, tm, PAIR), lambda b, j: (b, 0, j, 0))
        hist_spec = pl.BlockSpec((None, HALO, D_CONV), lambda b, j: (b, 0, 0))
        hist_rows = HALO
    else:
        y_spec = _tok_spec(tm, D_RWKV, tiles)
        hist_rows = hist.shape[0]
        hist_spec = pl.BlockSpec((hist_rows, D_CONV), lambda b, j: (0, 0))
    tok512 = _tok_spec(tm, D_RWKV, tiles)
    row512 = _const_spec((1, D_RWKV))
    return pl.pallas_call(
        functools.partial(_post_body, tm=tm, stride=stride, head_major=head_major),
        grid=(nb, tiles),
        in_specs=[y_spec, tok512, tok512, tok512, hist_spec, _tok_spec(tm, D_MODEL, tiles),
                  _mod_spec(gt, tm, tiles),
                  row512, row512, _const_spec((CONV_W, D_CONV)), row512, row512, row512,
                  _const_spec((D_MODEL, D_MODEL)), _const_spec((1, D_MODEL)),
                  _const_spec((D_RWKV, D_RWKV))],
        out_specs=_tok_spec(tm, D_MODEL, tiles),
        out_shape=jax.ShapeDtypeStruct(x1.shape, F32),
        scratch_shapes=[pltpu.VMEM((hist_rows + tm, D_CONV), F32)],
        compiler_params=_params(),
        name=name,
    )(y, bonus, gate, u, hist, x1, gt, w["ln_x_g"], w["ln_x_b"], w["conv_w"], w["conv_b"],
      w["conv_ln_g"], w["conv_ln_b"], w["w_out"], w["g_post1"], w["ebd"])


def _layer(x, mods, w, shift0, hist, s0, *, nb, seq, tm, tm_pre, stride, head_major, tag):
    tiles = (seq // tm) if head_major else 1
    tiles_pre = (seq // tm_pre) if head_major else 1
    x1 = _ffn_call(x, mods[0:3], w["g_pre0"], w["g_post0"], w["wup1"], w["wdn1"],
                   nb=nb, tiles=tiles, tm=tm, res_w=0.5, name=tag + "_ffn1")
    r, k, v, a, b, ld, gate, u, bonus, nshift = _pre_call(
        x1, mods[3:5], w["g_pre1"], w, shift0, nb=nb, tiles=tiles_pre, tm=tm_pre, stride=stride,
        head_major=head_major, name=tag + "_mixpre")
    if head_major:
        ops = [z.reshape(nb * N_PAIRS, seq, PAIR) for z in (r, k, v, a, b, ld)]
        y, s_new = _wkv_call(ops, s0, pblk=nb * N_PAIRS, chunk=32, name=tag + "_wkv")
        y = y.reshape(nb, N_PAIRS, seq, PAIR)
    else:
        t_len = x.shape[0] // stride

        def to_groups(z):
            z = z.reshape(t_len, stride, N_PAIRS, PAIR).transpose(1, 2, 0, 3)
            z = jnp.pad(z, ((0, 0), (0, 0), (0, 8 - t_len), (0, 0)))
            return z.reshape(stride * N_PAIRS, 8, PAIR)

        ops = [to_groups(z) for z in (r, k, v, a, b, ld)]
        y, s_new = _wkv_call(ops, s0, pblk=64, chunk=8, name=tag + "_wkv")
        y = y.reshape(stride, N_PAIRS, 8, PAIR)[:, :, :t_len].transpose(2, 0, 1, 3).reshape(x.shape[0], D_RWKV)
    x2 = _post_call(y, bonus, gate, u, hist, x1, mods[5], w, nb=nb, tiles=tiles, tm=tm, stride=stride,
                    head_major=head_major, name=tag + "_mixpost")
    out = _ffn_call(x2, mods[6:9], w["g_pre2"], w["g_post2"], w["wup2"], w["wdn2"],
                    nb=nb, tiles=tiles, tm=tm, res_w=0.5, name=tag + "_ffn2")
    return out, s_new, nshift, u


def kernel(x_prompt, x_sample, c_prompt, c_sample, state_wkv, state_shift, state_conv, w_ada, b_ada, g_pre, g_post, w_ffn1_up, w_ffn1_down, w_in, mu_shift, w0, w_decay_up, a0, w_iclr_up, w_gate_up, k_k, k_a, r_k, ln_x_g, ln_x_b, conv_w, conv_b, conv_ln_g, conv_ln_b, w_out, w_ffn2_up, w_ffn2_down):
    bp, seq, _ = x_prompt.shape
    bs, dec, _ = x_sample.shape
    row = lambda z: z.reshape(1, -1).astype(F32)
    zeros64 = jnp.zeros((ICLR_RANK, D_RWKV), BF16)
    head_of_lane = jnp.arange(D_RWKV) // HEAD
    w = dict(
        wup1=w_ffn1_up.astype(BF16), wdn1=w_ffn1_down.astype(BF16),
        wup2=w_ffn2_up.astype(BF16), wdn2=w_ffn2_down.astype(BF16),
        w_in=w_in.astype(BF16), w_out=w_out.astype(BF16),
        wd=jnp.concatenate([w_decay_up.astype(BF16), zeros64], axis=0),
        wi=jnp.concatenate([zeros64, w_iclr_up.astype(BF16)], axis=0),
        wg=w_gate_up.astype(BF16),
        mu=row(mu_shift), w0=row(w0), a0=row(a0), k_k=row(k_k), k_a=row(k_a), r_k=row(r_k),
        ln_x_g=row(ln_x_g), ln_x_b=row(ln_x_b), conv_w=conv_w, conv_b=row(conv_b),
        conv_ln_g=row(conv_ln_g), conv_ln_b=row(conv_ln_b),
        g_pre0=g_pre[0:1], g_pre1=g_pre[1:2], g_pre2=g_pre[2:3],
        g_post0=g_post[0:1], g_post1=g_post[1:2], g_post2=g_post[2:3],
        ebd=(head_of_lane[:, None] == head_of_lane[None, :]).astype(BF16),
    )

    ada = _ada_call(jnp.concatenate([c_prompt, c_sample], axis=0), w_ada, b_ada)
    ada = ada.reshape(bp + bs, N_SUB * 3, D_MODEL)
    mods_p = [ada[:bp, i][:, None, :] for i in range(N_SUB * 3)]
    mods_s = [ada[bp:, i] for i in range(N_SUB * 3)]

    yp, wkv_p, shift_p, u_p = _layer(
        x_prompt.reshape(bp * seq, D_MODEL), mods_p, w,
        jnp.zeros((bp, 1, N_SHIFT), F32), jnp.zeros((bp, HALO, D_CONV), F32),
        jnp.zeros((bp * N_PAIRS, 2, HEAD, HEAD), F32),
        nb=bp, seq=seq, tm=512, tm_pre=512, stride=1, head_major=True, tag="p")
    y_p = yp.reshape(bp, seq, D_MODEL)
    wkv_p = wkv_p.reshape(bp, N_HEADS, HEAD, HEAD)
    conv_p = u_p.reshape(bp, seq, D_CONV)[:, seq - (CONV_W - 1):, :]

    xs_tok = x_sample.transpose(1, 0, 2).reshape(dec * bs, D_MODEL)
    hist_s = state_conv.transpose(1, 0, 2).reshape((CONV_W - 1) * bs, D_CONV)
    ys, wkv_s, shift_s, u_s = _layer(
        xs_tok, mods_s, w, state_shift.reshape(bs, N_SHIFT), hist_s,
        state_wkv.reshape(bs * N_PAIRS, 2, HEAD, HEAD),
        nb=1, seq=dec * bs, tm=dec * bs, tm_pre=dec * bs, stride=bs, head_major=False, tag="s")
    y_s = ys.reshape(dec, bs, D_MODEL).transpose(1, 0, 2)
    wkv_s = wkv_s.reshape(bs, N_HEADS, HEAD, HEAD)
    shift_s = shift_s.reshape(bs, 1, N_SHIFT)
    conv_s = jnp.concatenate([state_conv, u_s.reshape(dec, bs, D_CONV).transpose(1, 0, 2)],
                             axis=1)[:, dec:, :]
    return (y_p, y_s, wkv_p, shift_p, conv_p, wkv_s, shift_s, conv_s)
```

```python
import functools

import jax
import jax.numpy as jnp
from jax import lax
from jax.experimental import pallas as pl
from jax.experimental.pallas import tpu as pltpu

F32 = jnp.float32
BF16 = jnp.bfloat16

D_MODEL = 1024
D_RWKV = 512
HEAD = 64
PAIR = 2 * HEAD
N_PAIRS = D_RWKV // PAIR
N_HEADS = D_RWKV // HEAD
D_CONV = D_MODEL - D_RWKV
CONV_W = 31
DECAY_RANK = 64
ICLR_RANK = 64
GATE_RANK = 128
D_FF = 2816
N_SHIFT = 3 * D_RWKV + DECAY_RANK + ICLR_RANK + GATE_RANK
D_IN = N_SHIFT + 2 * D_CONV
N_SUB = 3
RMS_EPS = 1e-6
LN_EPS = 1e-5
GN_EPS = 64e-5

FF_CHUNK = 256
WKV_ROWS = 256
CONV_ROWS = 128
WKV_WAYS = 4
HALO = 32
VMEM_LIMIT = 56 * 1024 * 1024

NN = (((1,), (0,)), ((), ()))
NT = (((1,), (1,)), ((), ()))
B_NT = (((2,), (2,)), ((0,), (0,)))
B_TN = (((1,), (1,)), ((0,), (0,)))


def _splits(x, n):
    if x.dtype == BF16:
        return [x]
    parts, rem = [], x
    for i in range(n):
        p = rem.astype(BF16)
        parts.append(p)
        if i + 1 < n:
            rem = rem - p.astype(F32)
    return parts


def _mm(a, b, dn, na=2, nb=2):
    ap, bp = _splits(a, na), _splits(b, nb)
    depth = max(len(ap), len(bp))
    out = None
    for i, x in enumerate(ap):
        for j, y in enumerate(bp):
            if i + j >= depth:
                continue
            t = lax.dot_general(x, y, dn, preferred_element_type=F32)
            out = t if out is None else out + t
    return out


def _rms(x, g):
    return x * lax.rsqrt(jnp.mean(x * x, axis=-1, keepdims=True) + RMS_EPS) * g


def _sigmoid(x):
    return jax.nn.sigmoid(x)


def _head_sum(x, ebd):
    return _mm(x, ebd, NN, na=2, nb=1)


def _ada_body(c_ref, w_ref, b_ref, o_ref):
    c = c_ref[...]
    s = (c * _sigmoid(c)).astype(BF16)
    o_ref[...] = jnp.dot(s, w_ref[...].astype(BF16), preferred_element_type=F32) + b_ref[...]


def _ada_call(c_all, w_ada, b_ada):
    n, tn = c_all.shape[0], 1152
    width = w_ada.shape[1]
    return pl.pallas_call(
        _ada_body,
        grid=(width // tn,),
        in_specs=[pl.BlockSpec((n, D_MODEL), lambda i: (0, 0)),
                  pl.BlockSpec((D_MODEL, tn), lambda i: (0, i)),
                  pl.BlockSpec((1, tn), lambda i: (0, i))],
        out_specs=pl.BlockSpec((n, tn), lambda i: (0, i)),
        out_shape=jax.ShapeDtypeStruct((n, width), F32),
        compiler_params=pltpu.CompilerParams(dimension_semantics=("arbitrary",),
                                             vmem_limit_bytes=VMEM_LIMIT),
        name="ada",
    )(c_all, w_ada, b_ada.reshape(1, width))


def _const_spec(shape):
    nd = len(shape)
    return pl.BlockSpec(shape, lambda b, j: (0,) * nd, pipeline_mode=pl.Buffered(1))


def _tok_spec(tm, width, tiles):
    return pl.BlockSpec((tm, width), lambda b, j: (b * tiles + j, 0))


def _mod_spec(arr, tm, tiles):
    if arr.ndim == 3:
        return pl.BlockSpec((None, 1, D_MODEL), lambda b, j: (b, 0, 0))
    return pl.BlockSpec(arr.shape, lambda b, j: (0, 0))


def _mod_rows(ref, tm):
    m = ref[...]
    reps = tm // m.shape[0] if m.shape[0] > 1 else 1
    return jnp.concatenate([m] * reps, axis=0) if reps > 1 else m


def _params():
    return pltpu.CompilerParams(dimension_semantics=("arbitrary", "arbitrary"),
                                vmem_limit_bytes=VMEM_LIMIT)


def _ffn_body(x_ref, sh_ref, sc_ref, gt_ref, gpre_ref, gpost_ref, wup_ref, wdn_ref, o_ref, act_scr, *, res_w):
    x = x_ref[...]
    tm = x.shape[0]
    h = _rms(x, gpre_ref[...]) * (1.0 + _mod_rows(sc_ref, tm)) + _mod_rows(sh_ref, tm)
    hb = h.astype(BF16)
    for j in range(D_FF // FF_CHUNK):
        lo = j * FF_CHUNK
        g = jnp.dot(hb, wup_ref[:, lo:lo + FF_CHUNK].astype(BF16), preferred_element_type=F32)
        u = jnp.dot(hb, wup_ref[:, D_FF + lo:D_FF + lo + FF_CHUNK].astype(BF16), preferred_element_type=F32)
        act_scr[:, lo:lo + FF_CHUNK] = (g * _sigmoid(g) * u).astype(BF16)
    out = None
    for j in range(D_FF // FF_CHUNK):
        lo = j * FF_CHUNK
        part = jnp.dot(act_scr[:, lo:lo + FF_CHUNK], wdn_ref[lo:lo + FF_CHUNK, :].astype(BF16), preferred_element_type=F32)
        out = part if out is None else out + part
    o_ref[...] = x + res_w * _mod_rows(gt_ref, tm) * _rms(out, gpost_ref[...])


def _ffn_call(x, mods, gpre, gpost, wup, wdn, *, nb, tiles, tm, res_w, name):
    sh, sc, gt = mods
    return pl.pallas_call(
        functools.partial(_ffn_body, res_w=res_w),
        grid=(nb, tiles),
        in_specs=[_tok_spec(tm, D_MODEL, tiles),
                  _mod_spec(sh, tm, tiles), _mod_spec(sc, tm, tiles), _mod_spec(gt, tm, tiles),
                  _const_spec((1, D_MODEL)), _const_spec((1, D_MODEL)),
                  _const_spec((D_MODEL, 2 * D_FF)), _const_spec((D_FF, D_MODEL))],
        out_specs=_tok_spec(tm, D_MODEL, tiles),
        out_shape=jax.ShapeDtypeStruct(x.shape, F32),
        scratch_shapes=[pltpu.VMEM((tm, D_FF), BF16)],
        compiler_params=_params(),
        name=name,
    )(x, sh, sc, gt, gpre, gpost, wup, wdn)


def _pre_body(x_ref, sh_ref, sc_ref, gpre_ref, win_ref, mu_ref, w0_ref, wd_ref, a0_ref, wi_ref, wg_ref,
              kk_ref, ka_ref, rk_ref, ebd_ref, shift0_ref,
              r_ref, k_ref, v_ref, a_ref, b_ref, ld_ref, gate_ref, u_ref, bonus_ref, nshift_ref,
              carry_scr, *, tm, stride, head_major):
    x = x_ref[...]
    h = _rms(x, gpre_ref[...]) * (1.0 + _mod_rows(sc_ref, tm)) + _mod_rows(sh_ref, tm)
    proj = jnp.dot(h.astype(BF16), win_ref[...].astype(BF16), preferred_element_type=F32)
    p_sh = proj[:, :N_SHIFT]
    p_cv = proj[:, N_SHIFT:]

    if stride == 1:
        @pl.when(pl.program_id(1) == 0)
        def _():
            carry_scr[...] = shift0_ref[...]
        row = lax.broadcasted_iota(jnp.int32, (tm, 1), 0)
        prev = jnp.where(row == 0, carry_scr[...], pltpu.roll(p_sh, 1, axis=0))
        carry_scr[...] = p_sh[tm - 1:tm, :]
    else:
        prev = jnp.concatenate([shift0_ref[...], p_sh[:tm - stride, :]], axis=0)
    nshift_ref[...] = p_sh[tm - stride:tm, :]
    xs = p_sh + (prev - p_sh) * mu_ref[...]

    o1, o2, o3 = D_RWKV, 2 * D_RWKV, 3 * D_RWKV
    r, k, v = xs[:, :o1], xs[:, o1:o2], xs[:, o2:o3]
    dwda = xs[:, o3:o3 + DECAY_RANK + ICLR_RANK]
    dg = xs[:, o3 + DECAY_RANK + ICLR_RANK:]

    zdec = -(w0_ref[...] + jnp.dot(jnp.tanh(dwda).astype(BF16), wd_ref[...], preferred_element_type=F32))
    softplus = jnp.maximum(zdec, 0.0) + jnp.log1p(jnp.exp(-jnp.abs(zdec)))
    ld = -jnp.exp(-softplus - 0.5)
    iclr = _sigmoid(a0_ref[...] + jnp.dot(dwda.astype(BF16), wi_ref[...], preferred_element_type=F32))
    gate = jnp.dot(_sigmoid(dg).astype(BF16), wg_ref[...], preferred_element_type=F32)

    ebd = ebd_ref[...]
    kk = k * kk_ref[...]
    kk = kk / jnp.maximum(jnp.sqrt(_head_sum(kk * kk, ebd)), 1e-12)
    kmod = k * (1.0 + (iclr - 1.0) * ka_ref[...])
    bonus = _head_sum(r * kmod * rk_ref[...], ebd) * v

    gate_ref[...] = gate
    bonus_ref[...] = bonus
    u_ref[...] = p_cv[:, :D_CONV] * _sigmoid(p_cv[:, D_CONV:])
    outs = ((r_ref, r), (k_ref, kmod), (v_ref, v), (a_ref, -kk), (b_ref, kk * iclr), (ld_ref, ld))
    for ref, val in outs:
        if head_major:
            for pr in range(N_PAIRS):
                ref[pr] = val[:, pr * PAIR:(pr + 1) * PAIR]
        else:
            ref[...] = val


def _pre_call(x1, mods, gpre, w, shift0, *, nb, tiles, tm, stride, head_major, name):
    sh, sc = mods
    n = x1.shape[0]
    if head_major:
        seq = tiles * tm
        hm_spec = pl.BlockSpec((None, N_PAIRS, tm, PAIR), lambda b, j: (b, 0, j, 0))
        hm_shape = jax.ShapeDtypeStruct((nb, N_PAIRS, seq, PAIR), F32)
    else:
        hm_spec = _tok_spec(tm, D_RWKV, tiles)
        hm_shape = jax.ShapeDtypeStruct((n, D_RWKV), F32)
    ld_spec = _tok_spec(tm, D_RWKV, tiles)
    ld_shape = jax.ShapeDtypeStruct((n, D_RWKV), F32)
    if stride == 1:
        shift_spec = pl.BlockSpec((None, 1, N_SHIFT), lambda b, j: (b, 0, 0))
        nshift_shape = jax.ShapeDtypeStruct((nb, 1, N_SHIFT), F32)
    else:
        shift_spec = pl.BlockSpec((stride, N_SHIFT), lambda b, j: (0, 0))
        nshift_shape = jax.ShapeDtypeStruct((stride, N_SHIFT), F32)
    return pl.pallas_call(
        functools.partial(_pre_body, tm=tm, stride=stride, head_major=head_major),
        grid=(nb, tiles),
        in_specs=[_tok_spec(tm, D_MODEL, tiles), _mod_spec(sh, tm, tiles), _mod_spec(sc, tm, tiles),
                  _const_spec((1, D_MODEL)), _const_spec((D_MODEL, D_IN)), _const_spec((1, N_SHIFT)),
                  _const_spec((1, D_RWKV)), _const_spec((DECAY_RANK + ICLR_RANK, D_RWKV)),
                  _const_spec((1, D_RWKV)), _const_spec((DECAY_RANK + ICLR_RANK, D_RWKV)),
                  _const_spec((GATE_RANK, D_RWKV)),
                  _const_spec((1, D_RWKV)), _const_spec((1, D_RWKV)), _const_spec((1, D_RWKV)),
                  _const_spec((D_RWKV, D_RWKV)), shift_spec],
        out_specs=[hm_spec] * 6 + [ld_spec] * 3 + [shift_spec],
        out_shape=[hm_shape] * 6 + [ld_shape] * 3 + [nshift_shape],
        scratch_shapes=[pltpu.VMEM((1, N_SHIFT), F32)],
        compiler_params=_params(),
        name=name,
    )(x1, sh, sc, gpre, w["w_in"], w["mu"], w["w0"], w["wd"], w["a0"], w["wi"], w["wg"],
      w["k_k"], w["k_a"], w["r_k"], w["ebd"], shift0)


def _wkv_body(r_ref, k_ref, v_ref, a_ref, b_ref, ld_ref, s0_ref, y_ref, st_ref, s_scr, *, chunk, n_sub, ways):
    c = pl.program_id(1)
    rows = WKV_ROWS
    pp = rows // (2 * chunk)
    shift = chunk.bit_length() - 1

    @pl.when(c == 0)
    def _():
        zero = jnp.zeros(s0_ref.shape[:1] + (HEAD, HEAD), F32)
        s_scr[:, :HEAD, :] = jnp.concatenate([s0_ref[:, 0], zero], axis=-1)
        s_scr[:, HEAD:, :] = jnp.concatenate([zero, s0_ref[:, 1]], axis=-1)

    ri = lax.broadcasted_iota(jnp.int32, (rows, rows), 0)
    ci = lax.broadcasted_iota(jnp.int32, (rows, rows), 1)
    same = jnp.right_shift(ri, shift) == jnp.right_shift(ci, shift)
    strict = same & (ci < ri)
    incl = same & (ci <= ri)
    eye = (ri == ci).astype(F32)
    first = lax.broadcasted_iota(jnp.int32, (1, 1, PAIR), 2) < HEAD
    tpos = lax.broadcasted_iota(jnp.int32, (pp * chunk, 1), 0) & (chunk - 1)

    def cumsum(x3):
        x = x3.reshape(pp * chunk, PAIR)
        for s in range(shift):
            d = 1 << s
            x = x + jnp.where(tpos >= d, pltpu.roll(x, d, axis=0), 0.0)
        return x.reshape(pp, chunk, PAIR)

    stack = lambda x3: jnp.concatenate([jnp.where(first, x3, 0.0), jnp.where(first, 0.0, x3)], axis=1)
    flat = lambda x3: x3.reshape(rows, PAIR)
    three = lambda x: x.reshape(pp, 2 * chunk, PAIR)
    each = lambda f, *xs: [f(*t) for t in zip(*xs)]

    def sub(it, carry):
        sls = [pl.ds(pl.multiple_of((it * ways + q) * pp, pp), pp) for q in range(ways)]
        r, k, v, a, b, ld = ([ref[sl] for sl in sls] for ref in (r_ref, k_ref, v_ref, a_ref, b_ref, ld_ref))
        cs = each(cumsum, ld)
        e_inv = each(lambda x: jnp.exp(-x), cs)
        at = each(lambda x, c_, l_: stack(x * jnp.exp(c_ - l_)), a, cs, ld)
        rt = each(lambda x, c_: stack(x * jnp.exp(c_)), r, cs)
        bt = each(lambda x, e: stack(x * e), b, e_inv)
        kt = each(lambda x, e: stack(x * e), k, e_inv)
        vs = each(stack, v)
        gram = each(lambda p, q, m, n: _mm(jnp.concatenate([flat(p), flat(q)], axis=0),
                                           jnp.concatenate([flat(m), flat(n)], axis=0), NT, 1, 1), at, rt, bt, kt)
        t_ab = each(lambda g: jnp.where(strict, g[:rows, :rows], 0.0).astype(BF16), gram)
        t_ak = each(lambda g: jnp.where(strict, g[:rows, rows:], 0.0), gram)
        m_rb = each(lambda g: jnp.where(incl, g[rows:, :rows], 0.0), gram)
        m_rk = each(lambda g: jnp.where(incl, g[rows:, rows:], 0.0), gram)
        inv = each(lambda t: eye + t.astype(F32), t_ab)
        pw = t_ab
        for _ in range(shift - 1):
            pw = each(lambda p: _mm(p, p, NN, 1, 1), pw)
            inv = each(lambda x, p: x + _mm(p, x, NN, 1, 1), inv, pw)
        s0 = [s_scr[sl] for sl in sls]
        uy0 = each(lambda p, q, s: _mm(jnp.concatenate([p, q], axis=1), s, B_NT, 1, 1), at, rt, s0)
        z = each(lambda o, t, x: flat(o[:, :2 * chunk, :]) + _mm(t, flat(x), NN, 1, 1), uy0, t_ak, vs)
        u = each(lambda x, z_: _mm(x, z_, NN, 1, 1), inv, z)
        res = each(lambda z_, u_, t: z_ - u_ + _mm(t, u_, NN, 1, 2), z, u, t_ab)
        u = each(lambda u_, x, r_: u_ + _mm(x, r_, NN, 1, 1), u, inv, res)
        y = each(lambda o, m, u_, n, x: three(flat(o[:, 2 * chunk:, :]) + _mm(m, u_, NN, 1, 1)
                                              + _mm(n, flat(x), NN, 1, 1)), uy0, m_rb, u, m_rk, vs)
        for sl, y_ in zip(sls, y):
            y_ref[sl] = y_[:, :chunk, :] + y_[:, chunk:, :]
        c_last = each(lambda x: x[:, chunk - 1:chunk, :], cs)
        tail = each(lambda cl, x: jnp.exp(cl - x), c_last, cs)
        upd = each(lambda u_, x, b_, k_, t: _mm(jnp.concatenate([three(u_), x], axis=1),
                                                jnp.concatenate([stack(b_ * t), stack(k_ * t)], axis=1), B_TN, 1, 1),
                   u, vs, b, k, tail)
        for sl, s, cl, d in zip(sls, s0, c_last, upd):
            s_scr[sl] = s * jnp.exp(cl) + d
        return carry

    lax.fori_loop(0, n_sub // ways, sub, 0)

    @pl.when(c == pl.num_programs(1) - 1)
    def _():
        st_ref[:, 0] = s_scr[:, :HEAD, :HEAD]
        st_ref[:, 1] = s_scr[:, HEAD:, HEAD:]


def _wkv_call(ops, s0, *, pblk, chunk, name):
    n_pairs, t_total, _ = ops[0].shape
    pp = WKV_ROWS // (2 * chunk)
    op_spec = pl.BlockSpec((pblk, chunk, PAIR), lambda g, c: (g, c, 0))
    st_spec = pl.BlockSpec((pblk, 2, HEAD, HEAD), lambda g, c: (g, 0, 0, 0))
    return pl.pallas_call(
        functools.partial(_wkv_body, chunk=chunk, n_sub=pblk // pp, ways=WKV_WAYS),
        grid=(n_pairs // pblk, t_total // chunk),
        in_specs=[op_spec] * 6 + [st_spec],
        out_specs=[op_spec, st_spec],
        out_shape=[jax.ShapeDtypeStruct((n_pairs, t_total, PAIR), F32),
                   jax.ShapeDtypeStruct((n_pairs, 2, HEAD, HEAD), F32)],
        scratch_shapes=[pltpu.VMEM((pblk, PAIR, PAIR), F32)],
        compiler_params=_params(),
        name=name,
    )(*ops, s0)


def _wkvseq_body(r_ref, k_ref, v_ref, a_ref, b_ref, ld_ref, s0_ref, y_ref, st_ref, t_scr, yt_scr, *, steps, nb):
    for qi, ref in enumerate((r_ref, k_ref, v_ref, a_ref, b_ref, ld_ref)):
        for t in range(steps):
            xt = ref[t * nb:(t + 1) * nb, :].T
            t_scr[qi, t] = jnp.exp(xt) if qi == 5 else xt
    for s in range(2):
        lo = s * HEAD

        def body(vi, carry):
            st = s0_ref[s, vi]
            for t in range(steps):
                col = lambda qi: t_scr[qi, t, lo:lo + HEAD, :]
                sa = jnp.sum(st * col(3), axis=0, keepdims=True)
                vv = t_scr[2, t, pl.ds(lo + vi, 1), :]
                st = st * col(5) + sa * col(4) + vv * col(1)
                yt_scr[t, pl.ds(lo + vi, 1), :] = jnp.sum(st * col(0), axis=0, keepdims=True)
            st_ref[s, vi] = st
            return carry

        lax.fori_loop(0, HEAD, body, 0)
    for t in range(steps):
        y_ref[t * nb:(t + 1) * nb, :] = yt_scr[t].T


def _wkvseq_call(ops, s0, *, steps, nb, name):
    n = steps * nb
    op_spec = pl.BlockSpec((n, PAIR), lambda p: (0, p))
    st_spec = pl.BlockSpec((2, HEAD, HEAD, nb), lambda p: (p, 0, 0, 0))
    return pl.pallas_call(
        functools.partial(_wkvseq_body, steps=steps, nb=nb),
        grid=(N_PAIRS,),
        in_specs=[op_spec] * 6 + [st_spec],
        out_specs=[op_spec, st_spec],
        out_shape=[jax.ShapeDtypeStruct((n, D_RWKV), F32),
                   jax.ShapeDtypeStruct((N_HEADS, HEAD, HEAD, nb), F32)],
        scratch_shapes=[pltpu.VMEM((6, steps, PAIR, nb), F32), pltpu.VMEM((steps, PAIR, nb), F32)],
        compiler_params=pltpu.CompilerParams(dimension_semantics=("arbitrary",),
                                             vmem_limit_bytes=VMEM_LIMIT),
        name=name,
    )(*ops, s0)


def _post_body(y_ref, bonus_ref, gate_ref, u_ref, hist_ref, x_ref, gt_ref,
               lng_ref, lnb_ref, cw_ref, cb_ref, clg_ref, clb_ref, wout_ref, gpost_ref, ebd_ref,
               o_ref, full_scr, *, tm, stride, head_major):
    if head_major:
        y = jnp.concatenate([y_ref[pr] for pr in range(N_PAIRS)], axis=-1)
    else:
        y = y_ref[...]
    ebd = ebd_ref[...]
    mu = _head_sum(y, ebd) * (1.0 / HEAD)
    d = y - mu
    var = _head_sum(d * d, ebd) * (1.0 / HEAD)
    yn = d * lax.rsqrt(var + GN_EPS) * lng_ref[...] + lnb_ref[...]
    y_rwkv = (yn + bonus_ref[...]) * gate_ref[...]

    hist = full_scr.shape[0] - tm
    if stride == 1:
        @pl.when(pl.program_id(1) == 0)
        def _():
            full_scr[:hist, :] = hist_ref[...]
    else:
        full_scr[:hist, :] = hist_ref[...]
    u = u_ref[...]
    full_scr[hist:, :] = u
    base = hist - (CONV_W - 1) * stride
    offs = [base + j * stride for j in range(CONV_W)]
    if stride % 8 == 0:
        z = jnp.zeros((tm, D_CONV), F32)
        for j in range(CONV_W):
            z = z + full_scr[pl.ds(offs[j], tm), :] * cw_ref[j:j + 1, :]
    else:
        cols = []
        for lo in range(0, D_CONV, 128):
            pieces = []
            for t0 in range(0, tm, CONV_ROWS):
                zc = None
                for rsd in range(8):
                    n = CONV_ROWS + (8 if rsd else 0)
                    part = None
                    for j in [j for j in range(CONV_W) if offs[j] % 8 == rsd]:
                        term = (full_scr[pl.ds(t0 + offs[j] - rsd, n), lo:lo + 128]
                                * cw_ref[j:j + 1, lo:lo + 128])
                        part = term if part is None else part + term
                    if rsd:
                        part = pltpu.roll(part, n - rsd, axis=0)[:CONV_ROWS, :]
                    zc = part if zc is None else zc + part
                pieces.append(zc)
            cols.append(jnp.concatenate(pieces, axis=0))
        z = jnp.concatenate(cols, axis=1)
    if stride == 1:
        full_scr[:hist, :] = u[tm - hist:, :]
    z = z + cb_ref[...]
    zm = jnp.mean(z, axis=-1, keepdims=True)
    zc = z - zm
    zv = jnp.mean(zc * zc, axis=-1, keepdims=True)
    zn = zc * lax.rsqrt(zv + LN_EPS) * clg_ref[...] + clb_ref[...]
    zn = zn * _sigmoid(zn)

    mixed = (jnp.dot(y_rwkv.astype(BF16), wout_ref[:D_RWKV, :].astype(BF16), preferred_element_type=F32)
             + jnp.dot(zn.astype(BF16), wout_ref[D_RWKV:, :].astype(BF16), preferred_element_type=F32))
    o_ref[...] = x_ref[...] + _mod_rows(gt_ref, tm) * _rms(mixed, gpost_ref[...])


def _post_call(y, bonus, gate, u, hist, x1, gt, w, *, nb, tiles, tm, stride, head_major, name):
    if head_major:
        y_spec = pl.BlockSpec((None, N_PAIRS, tm, PAIR), lambda b, j: (b, 0, j, 0))
        hist_spec = pl.BlockSpec((None, HALO, D_CONV), lambda b, j: (b, 0, 0))
        hist_rows = HALO
    else:
        y_spec = _tok_spec(tm, D_RWKV, tiles)
        hist_rows = hist.shape[0]
        hist_spec = pl.BlockSpec((hist_rows, D_CONV), lambda b, j: (0, 0))
    tok512 = _tok_spec(tm, D_RWKV, tiles)
    row512 = _const_spec((1, D_RWKV))
    return pl.pallas_call(
        functools.partial(_post_body, tm=tm, stride=stride, head_major=head_major),
        grid=(nb, tiles),
        in_specs=[y_spec, tok512, tok512, tok512, hist_spec, _tok_spec(tm, D_MODEL, tiles),
                  _mod_spec(gt, tm, tiles),
                  row512, row512, _const_spec((CONV_W, D_CONV)), row512, row512, row512,
                  _const_spec((D_MODEL, D_MODEL)), _const_spec((1, D_MODEL)),
                  _const_spec((D_RWKV, D_RWKV))],
        out_specs=_tok_spec(tm, D_MODEL, tiles),
        out_shape=jax.ShapeDtypeStruct(x1.shape, F32),
        scratch_shapes=[pltpu.VMEM((hist_rows + tm, D_CONV), F32)],
        compiler_params=_params(),
        name=name,
    )(y, bonus, gate, u, hist, x1, gt, w["ln_x_g"], w["ln_x_b"], w["conv_w"], w["conv_b"],
      w["conv_ln_g"], w["conv_ln_b"], w["w_out"], w["g_post1"], w["ebd"])


def _layer(x, mods, w, shift0, hist, s0, *, nb, seq, tm, tm_pre, stride, head_major, tag):
    tiles = (seq // tm) if head_major else 1
    tiles_pre = (seq // tm_pre) if head_major else 1
    x1 = _ffn_call(x, mods[0:3], w["g_pre0"], w["g_post0"], w["wup1"], w["wdn1"],
                   nb=nb, tiles=tiles, tm=tm, res_w=0.5, name=tag + "_ffn1")
    r, k, v, a, b, ld, gate, u, bonus, nshift = _pre_call(
        x1, mods[3:5], w["g_pre1"], w, shift0, nb=nb, tiles=tiles_pre, tm=tm_pre, stride=stride,
        head_major=head_major, name=tag + "_mixpre")
    if head_major:
        ops = [z.reshape(nb * N_PAIRS, seq, PAIR) for z in (r, k, v, a, b, ld)]
        y, s_new = _wkv_call(ops, s0, pblk=nb * N_PAIRS, chunk=32, name=tag + "_wkv")
        y = y.reshape(nb, N_PAIRS, seq, PAIR)
    else:
        y, s_new = _wkvseq_call((r, k, v, a, b, ld), s0, steps=x.shape[0] // stride, nb=stride, name=tag + "_wkv")
    x2 = _post_call(y, bonus, gate, u, hist, x1, mods[5], w, nb=nb, tiles=tiles, tm=tm, stride=stride,
                    head_major=head_major, name=tag + "_mixpost")
    out = _ffn_call(x2, mods[6:9], w["g_pre2"], w["g_post2"], w["wup2"], w["wdn2"],
                    nb=nb, tiles=tiles, tm=tm, res_w=0.5, name=tag + "_ffn2")
    return out, s_new, nshift, u


def kernel(x_prompt, x_sample, c_prompt, c_sample, state_wkv, state_shift, state_conv, w_ada, b_ada, g_pre, g_post, w_ffn1_up, w_ffn1_down, w_in, mu_shift, w0, w_decay_up, a0, w_iclr_up, w_gate_up, k_k, k_a, r_k, ln_x_g, ln_x_b, conv_w, conv_b, conv_ln_g, conv_ln_b, w_out, w_ffn2_up, w_ffn2_down):
    bp, seq, _ = x_prompt.shape
    bs, dec, _ = x_sample.shape
    row = lambda z: z.reshape(1, -1).astype(F32)
    zeros64 = jnp.zeros((ICLR_RANK, D_RWKV), BF16)
    head_of_lane = jnp.arange(D_RWKV) // HEAD
    w = dict(
        wup1=w_ffn1_up, wdn1=w_ffn1_down, wup2=w_ffn2_up, wdn2=w_ffn2_down,
        w_in=w_in, w_out=w_out,
        wd=jnp.concatenate([w_decay_up.astype(BF16), zeros64], axis=0),
        wi=jnp.concatenate([zeros64, w_iclr_up.astype(BF16)], axis=0),
        wg=w_gate_up.astype(BF16),
        mu=row(mu_shift), w0=row(w0), a0=row(a0), k_k=row(k_k), k_a=row(k_a), r_k=row(r_k),
        ln_x_g=row(ln_x_g), ln_x_b=row(ln_x_b), conv_w=conv_w, conv_b=row(conv_b),
        conv_ln_g=row(conv_ln_g), conv_ln_b=row(conv_ln_b),
        g_pre0=g_pre[0:1], g_pre1=g_pre[1:2], g_pre2=g_pre[2:3],
        g_post0=g_post[0:1], g_post1=g_post[1:2], g_post2=g_post[2:3],
        ebd=(head_of_lane[:, None] == head_of_lane[None, :]).astype(BF16),
    )

    ada = _ada_call(jnp.concatenate([c_prompt, c_sample], axis=0), w_ada, b_ada)
    ada = ada.reshape(bp + bs, N_SUB * 3, D_MODEL)
    mods_p = [ada[:bp, i][:, None, :] for i in range(N_SUB * 3)]
    mods_s = [ada[bp:, i] for i in range(N_SUB * 3)]

    yp, wkv_p, shift_p, u_p = _layer(
        x_prompt.reshape(bp * seq, D_MODEL), mods_p, w,
        jnp.zeros((bp, 1, N_SHIFT), F32), jnp.zeros((bp, HALO, D_CONV), F32),
        jnp.zeros((bp * N_PAIRS, 2, HEAD, HEAD), F32),
        nb=bp, seq=seq, tm=512, tm_pre=512, stride=1, head_major=True, tag="p")
    y_p = yp.reshape(bp, seq, D_MODEL)
    wkv_p = wkv_p.reshape(bp, N_HEADS, HEAD, HEAD)
    conv_p = u_p.reshape(bp, seq, D_CONV)[:, seq - (CONV_W - 1):, :]

    xs_tok = x_sample.transpose(1, 0, 2).reshape(dec * bs, D_MODEL)
    hist_s = state_conv.transpose(1, 0, 2).reshape((CONV_W - 1) * bs, D_CONV)
    ys, wkv_s, shift_s, u_s = _layer(
        xs_tok, mods_s, w, state_shift.reshape(bs, N_SHIFT), hist_s,
        state_wkv.transpose(1, 2, 3, 0),
        nb=1, seq=dec * bs, tm=dec * bs, tm_pre=dec * bs, stride=bs, head_major=False, tag="s")
    y_s = ys.reshape(dec, bs, D_MODEL).transpose(1, 0, 2)
    wkv_s = wkv_s.transpose(3, 0, 1, 2)
    shift_s = shift_s.reshape(bs, 1, N_SHIFT)
    conv_s = jnp.concatenate([state_conv, u_s.reshape(dec, bs, D_CONV).transpose(1, 0, 2)],
                             axis=1)[:, dec:, :]
    return (y_p, y_s, wkv_p, shift_p, conv_p, wkv_s, shift_s, conv_s)
```

```python
import functools

import jax
import jax.numpy as jnp
from jax import lax
from jax.experimental import pallas as pl
from jax.experimental.pallas import tpu as pltpu

F32 = jnp.float32
BF16 = jnp.bfloat16

D_MODEL = 1024
D_RWKV = 512
HEAD = 64
PAIR = 2 * HEAD
N_PAIRS = D_RWKV // PAIR
N_HEADS = D_RWKV // HEAD
D_CONV = D_MODEL - D_RWKV
CONV_W = 31
DECAY_RANK = 64
ICLR_RANK = 64
GATE_RANK = 128
D_FF = 2816
N_SHIFT = 3 * D_RWKV + DECAY_RANK + ICLR_RANK + GATE_RANK
D_IN = N_SHIFT + 2 * D_CONV
N_SUB = 3
RMS_EPS = 1e-6
LN_EPS = 1e-5
GN_EPS = 64e-5

FF_CHUNK = 256
WKV_ROWS = 256
CONV_ROWS = 128
WKV_WAYS = 8
HALO = 32
VMEM_LIMIT = 56 * 1024 * 1024

NN = (((1,), (0,)), ((), ()))
NT = (((1,), (1,)), ((), ()))
B_NT = (((2,), (2,)), ((0,), (0,)))
B_TN = (((1,), (1,)), ((0,), (0,)))


def _splits(x, n):
    if x.dtype == BF16:
        return [x]
    parts, rem = [], x
    for i in range(n):
        p = rem.astype(BF16)
        parts.append(p)
        if i + 1 < n:
            rem = rem - p.astype(F32)
    return parts


def _mm(a, b, dn, na=2, nb=2):
    ap, bp = _splits(a, na), _splits(b, nb)
    depth = max(len(ap), len(bp))
    out = None
    for i, x in enumerate(ap):
        for j, y in enumerate(bp):
            if i + j >= depth:
                continue
            t = lax.dot_general(x, y, dn, preferred_element_type=F32)
            out = t if out is None else out + t
    return out


def _rms(x, g):
    return x * lax.rsqrt(jnp.mean(x * x, axis=-1, keepdims=True) + RMS_EPS) * g


def _sigmoid(x):
    return jax.nn.sigmoid(x)


def _head_sum(x, ebd):
    return _mm(x, ebd, NN, na=2, nb=1)


def _ada_body(c_ref, w_ref, b_ref, o_ref):
    c = c_ref[...]
    s = (c * _sigmoid(c)).astype(BF16)
    o_ref[...] = jnp.dot(s, w_ref[...].astype(BF16), preferred_element_type=F32) + b_ref[...]


def _ada_call(c_all, w_ada, b_ada):
    n = c_all.shape[0]
    width = w_ada.shape[1]
    return pl.pallas_call(
        _ada_body,
        grid=(width // D_MODEL,),
        in_specs=[pl.BlockSpec((n, D_MODEL), lambda i: (0, 0)),
                  pl.BlockSpec((D_MODEL, D_MODEL), lambda i: (0, i)),
                  pl.BlockSpec((1, D_MODEL), lambda i: (0, i))],
        out_specs=pl.BlockSpec((None, n, D_MODEL), lambda i: (i, 0, 0)),
        out_shape=jax.ShapeDtypeStruct((width // D_MODEL, n, D_MODEL), F32),
        compiler_params=pltpu.CompilerParams(dimension_semantics=("arbitrary",),
                                             vmem_limit_bytes=VMEM_LIMIT),
        name="ada",
    )(c_all, w_ada, b_ada.reshape(1, width))


def _const_spec(shape):
    nd = len(shape)
    return pl.BlockSpec(shape, lambda b, j: (0,) * nd, pipeline_mode=pl.Buffered(1))


def _tok_spec(tm, width, tiles):
    return pl.BlockSpec((tm, width), lambda b, j: (b * tiles + j, 0))


def _mod_spec(slab, rows, row0):
    return pl.BlockSpec((None, rows, D_MODEL), lambda b, j: (slab, row0 // rows, 0))


def _mod_rows(ref, tm, stride):
    if stride == 1:
        return ref[pl.ds(pl.program_id(0), 1), :]
    return jnp.concatenate([ref[...]] * (tm // stride), axis=0)


def _params():
    return pltpu.CompilerParams(dimension_semantics=("arbitrary", "arbitrary"),
                                vmem_limit_bytes=VMEM_LIMIT)


def _ffn_body(x_ref, sh_ref, sc_ref, gt_ref, gpre_ref, gpost_ref, wup_ref, wdn_ref, o_ref, act_scr, *, res_w, stride):
    x = x_ref[...]
    tm = x.shape[0]
    h = _rms(x, gpre_ref[...]) * (1.0 + _mod_rows(sc_ref, tm, stride)) + _mod_rows(sh_ref, tm, stride)
    hb = h.astype(BF16)
    for j in range(D_FF // FF_CHUNK):
        lo = j * FF_CHUNK
        g = jnp.dot(hb, wup_ref[:, lo:lo + FF_CHUNK].astype(BF16), preferred_element_type=F32)
        u = jnp.dot(hb, wup_ref[:, D_FF + lo:D_FF + lo + FF_CHUNK].astype(BF16), preferred_element_type=F32)
        act_scr[:, lo:lo + FF_CHUNK] = (g * _sigmoid(g) * u).astype(BF16)
    out = None
    for j in range(D_FF // FF_CHUNK):
        lo = j * FF_CHUNK
        part = jnp.dot(act_scr[:, lo:lo + FF_CHUNK], wdn_ref[lo:lo + FF_CHUNK, :].astype(BF16), preferred_element_type=F32)
        out = part if out is None else out + part
    o_ref[...] = x + res_w * _mod_rows(gt_ref, tm, stride) * _rms(out, gpost_ref[...])


def _ffn_call(x, ada, sub, gpre, gpost, wup, wdn, *, nb, tiles, tm, stride, mod_rows, res_w, name):
    return pl.pallas_call(
        functools.partial(_ffn_body, res_w=res_w, stride=stride),
        grid=(nb, tiles),
        in_specs=[_tok_spec(tm, D_MODEL, tiles),
                  _mod_spec(3 * sub, *mod_rows), _mod_spec(3 * sub + 1, *mod_rows), _mod_spec(3 * sub + 2, *mod_rows),
                  _const_spec((1, D_MODEL)), _const_spec((1, D_MODEL)),
                  _const_spec((D_MODEL, 2 * D_FF)), _const_spec((D_FF, D_MODEL))],
        out_specs=_tok_spec(tm, D_MODEL, tiles),
        out_shape=jax.ShapeDtypeStruct(x.shape, F32),
        scratch_shapes=[pltpu.VMEM((tm, D_FF), BF16)],
        compiler_params=_params(),
        name=name,
    )(x, ada, ada, ada, gpre, gpost, wup, wdn)


def _pre_body(x_ref, sh_ref, sc_ref, gpre_ref, win_ref, mu_ref, w0_ref, wd_ref, a0_ref, wi_ref, wg_ref,
              kk_ref, ka_ref, rk_ref, ebd_ref, shift0_ref,
              r_ref, k_ref, v_ref, a_ref, b_ref, ld_ref, gate_ref, u_ref, bonus_ref, nshift_ref,
              carry_scr, *, tm, stride, head_major):
    x = x_ref[...]
    h = _rms(x, gpre_ref[...]) * (1.0 + _mod_rows(sc_ref, tm, stride)) + _mod_rows(sh_ref, tm, stride)
    proj = jnp.dot(h.astype(BF16), win_ref[...].astype(BF16), preferred_element_type=F32)
    p_sh = proj[:, :N_SHIFT]
    p_cv = proj[:, N_SHIFT:]

    if stride == 1:
        @pl.when(pl.program_id(1) == 0)
        def _():
            carry_scr[...] = shift0_ref[...]
        row = lax.broadcasted_iota(jnp.int32, (tm, 1), 0)
        prev = jnp.where(row == 0, carry_scr[...], pltpu.roll(p_sh, 1, axis=0))
        carry_scr[...] = p_sh[tm - 1:tm, :]
    else:
        prev = jnp.concatenate([shift0_ref[...], p_sh[:tm - stride, :]], axis=0)
    nshift_ref[...] = p_sh[tm - stride:tm, :]
    xs = p_sh + (prev - p_sh) * mu_ref[...]

    o1, o2, o3 = D_RWKV, 2 * D_RWKV, 3 * D_RWKV
    r, k, v = xs[:, :o1], xs[:, o1:o2], xs[:, o2:o3]
    dwda = xs[:, o3:o3 + DECAY_RANK + ICLR_RANK]
    dg = xs[:, o3 + DECAY_RANK + ICLR_RANK:]

    zdec = -(w0_ref[...] + jnp.dot(jnp.tanh(dwda).astype(BF16), wd_ref[...], preferred_element_type=F32))
    softplus = jnp.maximum(zdec, 0.0) + jnp.log1p(jnp.exp(-jnp.abs(zdec)))
    ld = -jnp.exp(-softplus - 0.5)
    iclr = _sigmoid(a0_ref[...] + jnp.dot(dwda.astype(BF16), wi_ref[...], preferred_element_type=F32))
    gate = jnp.dot(_sigmoid(dg).astype(BF16), wg_ref[...], preferred_element_type=F32)

    ebd = ebd_ref[...]
    kk = k * kk_ref[...]
    kk = kk / jnp.maximum(jnp.sqrt(_head_sum(kk * kk, ebd)), 1e-12)
    kmod = k * (1.0 + (iclr - 1.0) * ka_ref[...])
    bonus = _head_sum(r * kmod * rk_ref[...], ebd) * v

    gate_ref[...] = gate
    bonus_ref[...] = bonus
    u_ref[...] = p_cv[:, :D_CONV] * _sigmoid(p_cv[:, D_CONV:])
    outs = ((r_ref, r), (k_ref, kmod), (v_ref, v), (a_ref, -kk), (b_ref, kk * iclr), (ld_ref, ld))
    for ref, val in outs:
        if head_major:
            for pr in range(N_PAIRS):
                ref[pr] = val[:, pr * PAIR:(pr + 1) * PAIR]
        else:
            ref[...] = val


def _pre_call(x1, ada, gpre, w, shift0, *, nb, tiles, tm, stride, mod_rows, head_major, name):
    n = x1.shape[0]
    if head_major:
        seq = tiles * tm
        hm_spec = pl.BlockSpec((None, N_PAIRS, tm, PAIR), lambda b, j: (b, 0, j, 0))
        hm_shape = jax.ShapeDtypeStruct((nb, N_PAIRS, seq, PAIR), F32)
    else:
        hm_spec = _tok_spec(tm, D_RWKV, tiles)
        hm_shape = jax.ShapeDtypeStruct((n, D_RWKV), F32)
    ld_spec = _tok_spec(tm, D_RWKV, tiles)
    ld_shape = jax.ShapeDtypeStruct((n, D_RWKV), F32)
    if stride == 1:
        shift_spec = pl.BlockSpec((None, 1, N_SHIFT), lambda b, j: (b, 0, 0))
        nshift_shape = jax.ShapeDtypeStruct((nb, 1, N_SHIFT), F32)
    else:
        shift_spec = pl.BlockSpec((stride, N_SHIFT), lambda b, j: (0, 0))
        nshift_shape = jax.ShapeDtypeStruct((stride, N_SHIFT), F32)
    return pl.pallas_call(
        functools.partial(_pre_body, tm=tm, stride=stride, head_major=head_major),
        grid=(nb, tiles),
        in_specs=[_tok_spec(tm, D_MODEL, tiles), _mod_spec(3, *mod_rows), _mod_spec(4, *mod_rows),
                  _const_spec((1, D_MODEL)), _const_spec((D_MODEL, D_IN)), _const_spec((1, N_SHIFT)),
                  _const_spec((1, D_RWKV)), _const_spec((DECAY_RANK + ICLR_RANK, D_RWKV)),
                  _const_spec((1, D_RWKV)), _const_spec((DECAY_RANK + ICLR_RANK, D_RWKV)),
                  _const_spec((GATE_RANK, D_RWKV)),
                  _const_spec((1, D_RWKV)), _const_spec((1, D_RWKV)), _const_spec((1, D_RWKV)),
                  _const_spec((D_RWKV, D_RWKV)), shift_spec],
        out_specs=[hm_spec] * 6 + [ld_spec] * 3 + [shift_spec],
        out_shape=[hm_shape] * 6 + [ld_shape] * 3 + [nshift_shape],
        scratch_shapes=[pltpu.VMEM((1, N_SHIFT), F32)],
        compiler_params=_params(),
        name=name,
    )(x1, ada, ada, gpre, w["w_in"], w["mu"], w["w0"], w["wd"], w["a0"], w["wi"], w["wg"],
      w["k_k"], w["k_a"], w["r_k"], w["ebd"], shift0)


def _wkv_body(r_ref, k_ref, v_ref, a_ref, b_ref, ld_ref, s0_ref, y_ref, st_ref, s_scr, *, chunk, n_sub, ways):
    c = pl.program_id(1)
    rows = WKV_ROWS
    pp = rows // (2 * chunk)
    shift = chunk.bit_length() - 1

    @pl.when(c == 0)
    def _():
        zero = jnp.zeros(s0_ref.shape[:1] + (HEAD, HEAD), F32)
        s_scr[:, :HEAD, :] = jnp.concatenate([s0_ref[:, 0], zero], axis=-1)
        s_scr[:, HEAD:, :] = jnp.concatenate([zero, s0_ref[:, 1]], axis=-1)

    ri = lax.broadcasted_iota(jnp.int32, (rows, rows), 0)
    ci = lax.broadcasted_iota(jnp.int32, (rows, rows), 1)
    same = jnp.right_shift(ri, shift) == jnp.right_shift(ci, shift)
    strict = same & (ci < ri)
    incl = same & (ci <= ri)
    eye = (ri == ci).astype(F32)
    first = lax.broadcasted_iota(jnp.int32, (1, 1, PAIR), 2) < HEAD
    tpos = lax.broadcasted_iota(jnp.int32, (pp * chunk, 1), 0) & (chunk - 1)

    def cumsum(x3):
        x = x3.reshape(pp * chunk, PAIR)
        for s in range(shift):
            d = 1 << s
            x = x + jnp.where(tpos >= d, pltpu.roll(x, d, axis=0), 0.0)
        return x.reshape(pp, chunk, PAIR)

    stack = lambda x3: jnp.concatenate([jnp.where(first, x3, 0.0), jnp.where(first, 0.0, x3)], axis=1)
    flat = lambda x3: x3.reshape(rows, PAIR)
    three = lambda x: x.reshape(pp, 2 * chunk, PAIR)
    each = lambda f, *xs: [f(*t) for t in zip(*xs)]

    def sub(it, carry):
        sls = [pl.ds(pl.multiple_of((it * ways + q) * pp, pp), pp) for q in range(ways)]
        r, k, v, a, b, ld = ([ref[sl] for sl in sls] for ref in (r_ref, k_ref, v_ref, a_ref, b_ref, ld_ref))
        cs = each(cumsum, ld)
        e_inv = each(lambda x: jnp.exp(-x), cs)
        at = each(lambda x, c_, l_: stack(x * jnp.exp(c_ - l_)), a, cs, ld)
        rt = each(lambda x, c_: stack(x * jnp.exp(c_)), r, cs)
        bt = each(lambda x, e: stack(x * e), b, e_inv)
        kt = each(lambda x, e: stack(x * e), k, e_inv)
        vs = each(stack, v)
        gram = each(lambda p, q, m, n: _mm(jnp.concatenate([flat(p), flat(q)], axis=0),
                                           jnp.concatenate([flat(m), flat(n)], axis=0), NT, 1, 1), at, rt, bt, kt)
        t_ab = each(lambda g: jnp.where(strict, g[:rows, :rows], 0.0).astype(BF16), gram)
        t_ak = each(lambda g: jnp.where(strict, g[:rows, rows:], 0.0), gram)
        m_rb = each(lambda g: jnp.where(incl, g[rows:, :rows], 0.0), gram)
        m_rk = each(lambda g: jnp.where(incl, g[rows:, rows:], 0.0), gram)
        inv = each(lambda t: eye + t.astype(F32), t_ab)
        pw = t_ab
        for _ in range(shift - 1):
            pw = each(lambda p: _mm(p, p, NN, 1, 1), pw)
            inv = each(lambda x, p: x + _mm(p, x, NN, 1, 1), inv, pw)
        s0 = [s_scr[sl] for sl in sls]
        uy0 = each(lambda p, q, s: _mm(jnp.concatenate([p, q], axis=1), s, B_NT, 1, 1), at, rt, s0)
        z = each(lambda o, t, x: flat(o[:, :2 * chunk, :]) + _mm(t, flat(x), NN, 1, 1), uy0, t_ak, vs)
        u = each(lambda x, z_: _mm(x, z_, NN, 1, 1), inv, z)
        res = each(lambda z_, u_, t: z_ - u_ + _mm(t, u_, NN, 1, 2), z, u, t_ab)
        u = each(lambda u_, x, r_: u_ + _mm(x, r_, NN, 1, 1), u, inv, res)
        y = each(lambda o, m, u_, n, x: three(flat(o[:, 2 * chunk:, :]) + _mm(m, u_, NN, 1, 1)
                                              + _mm(n, flat(x), NN, 1, 1)), uy0, m_rb, u, m_rk, vs)
        for sl, y_ in zip(sls, y):
            y_ref[sl] = y_[:, :chunk, :] + y_[:, chunk:, :]
        c_last = each(lambda x: x[:, chunk - 1:chunk, :], cs)
        tail = each(lambda cl, x: jnp.exp(cl - x), c_last, cs)
        upd = each(lambda u_, x, b_, k_, t: _mm(jnp.concatenate([three(u_), x], axis=1),
                                                jnp.concatenate([stack(b_ * t), stack(k_ * t)], axis=1), B_TN, 1, 1),
                   u, vs, b, k, tail)
        for sl, s, cl, d in zip(sls, s0, c_last, upd):
            s_scr[sl] = s * jnp.exp(cl) + d
        return carry

    lax.fori_loop(0, n_sub // ways, sub, 0)

    @pl.when(c == pl.num_programs(1) - 1)
    def _():
        st_ref[:, 0] = s_scr[:, :HEAD, :HEAD]
        st_ref[:, 1] = s_scr[:, HEAD:, HEAD:]


def _wkv_call(ops, s0, *, pblk, chunk, name):
    n_pairs, t_total, _ = ops[0].shape
    pp = WKV_ROWS // (2 * chunk)
    op_spec = pl.BlockSpec((pblk, chunk, PAIR), lambda g, c: (g, c, 0))
    st_spec = pl.BlockSpec((pblk, 2, HEAD, HEAD), lambda g, c: (g, 0, 0, 0))
    return pl.pallas_call(
        functools.partial(_wkv_body, chunk=chunk, n_sub=pblk // pp, ways=WKV_WAYS),
        grid=(n_pairs // pblk, t_total // chunk),
        in_specs=[op_spec] * 6 + [st_spec],
        out_specs=[op_spec, st_spec],
        out_shape=[jax.ShapeDtypeStruct((n_pairs, t_total, PAIR), F32),
                   jax.ShapeDtypeStruct((n_pairs, 2, HEAD, HEAD), F32)],
        scratch_shapes=[pltpu.VMEM((pblk, PAIR, PAIR), F32)],
        compiler_params=_params(),
        name=name,
    )(*ops, s0)


def _wkvseq_body(r_ref, k_ref, v_ref, a_ref, b_ref, ld_ref, s0_ref, y_ref, st_ref, t_scr, yt_scr, *, steps, nb):
    for qi, ref in enumerate((r_ref, k_ref, v_ref, a_ref, b_ref, ld_ref)):
        for t in range(steps):
            xt = ref[t * nb:(t + 1) * nb, :].T
            t_scr[qi, t] = jnp.exp(xt) if qi == 5 else xt
    for s in range(2):
        lo = s * HEAD

        def body(vi, carry):
            st = s0_ref[s, vi]
            for t in range(steps):
                col = lambda qi: t_scr[qi, t, lo:lo + HEAD, :]
                sa = jnp.sum(st * col(3), axis=0, keepdims=True)
                vv = t_scr[2, t, pl.ds(lo + vi, 1), :]
                st = st * col(5) + sa * col(4) + vv * col(1)
                yt_scr[t, pl.ds(lo + vi, 1), :] = jnp.sum(st * col(0), axis=0, keepdims=True)
            st_ref[s, vi] = st
            return carry

        lax.fori_loop(0, HEAD, body, 0, unroll=4)
    for t in range(steps):
        y_ref[t * nb:(t + 1) * nb, :] = yt_scr[t].T


def _wkvseq_call(ops, s0, *, steps, nb, name):
    n = steps * nb
    op_spec = pl.BlockSpec((n, PAIR), lambda p: (0, p))
    st_spec = pl.BlockSpec((2, HEAD, HEAD, nb), lambda p: (p, 0, 0, 0))
    return pl.pallas_call(
        functools.partial(_wkvseq_body, steps=steps, nb=nb),
        grid=(N_PAIRS,),
        in_specs=[op_spec] * 6 + [st_spec],
        out_specs=[op_spec, st_spec],
        out_shape=[jax.ShapeDtypeStruct((n, D_RWKV), F32),
                   jax.ShapeDtypeStruct((N_HEADS, HEAD, HEAD, nb), F32)],
        scratch_shapes=[pltpu.VMEM((6, steps, PAIR, nb), F32), pltpu.VMEM((steps, PAIR, nb), F32)],
        compiler_params=pltpu.CompilerParams(dimension_semantics=("arbitrary",),
                                             vmem_limit_bytes=VMEM_LIMIT),
        name=name,
    )(*ops, s0)


def _post_body(y_ref, bonus_ref, gate_ref, u_ref, hist_ref, x_ref, gt_ref,
               lng_ref, lnb_ref, cw_ref, cb_ref, clg_ref, clb_ref, wout_ref, gpost_ref, ebd_ref,
               o_ref, full_scr, *, tm, stride, head_major):
    if head_major:
        y = jnp.concatenate([y_ref[pr] for pr in range(N_PAIRS)], axis=-1)
    else:
        y = y_ref[...]
    ebd = ebd_ref[...]
    mu = _head_sum(y, ebd) * (1.0 / HEAD)
    d = y - mu
    var = _head_sum(d * d, ebd) * (1.0 / HEAD)
    yn = d * lax.rsqrt(var + GN_EPS) * lng_ref[...] + lnb_ref[...]
    y_rwkv = (yn + bonus_ref[...]) * gate_ref[...]

    hist = full_scr.shape[0] - tm
    if stride == 1:
        @pl.when(pl.program_id(1) == 0)
        def _():
            full_scr[:hist, :] = hist_ref[...]
    else:
        full_scr[:hist, :] = hist_ref[...]
    u = u_ref[...]
    full_scr[hist:, :] = u
    base = hist - (CONV_W - 1) * stride
    offs = [base + j * stride for j in range(CONV_W)]
    if stride % 8 == 0:
        z = jnp.zeros((tm, D_CONV), F32)
        for j in range(CONV_W):
            z = z + full_scr[pl.ds(offs[j], tm), :] * cw_ref[j:j + 1, :]
    else:
        cols = []
        for lo in range(0, D_CONV, 128):
            pieces = []
            for t0 in range(0, tm, CONV_ROWS):
                zc = None
                for rsd in range(8):
                    n = CONV_ROWS + (8 if rsd else 0)
                    part = None
                    for j in [j for j in range(CONV_W) if offs[j] % 8 == rsd]:
                        term = (full_scr[pl.ds(t0 + offs[j] - rsd, n), lo:lo + 128]
                                * cw_ref[j:j + 1, lo:lo + 128])
                        part = term if part is None else part + term
                    if rsd:
                        part = pltpu.roll(part, n - rsd, axis=0)[:CONV_ROWS, :]
                    zc = part if zc is None else zc + part
                pieces.append(zc)
            cols.append(jnp.concatenate(pieces, axis=0))
        z = jnp.concatenate(cols, axis=1)
    if stride == 1:
        full_scr[:hist, :] = u[tm - hist:, :]
    z = z + cb_ref[...]
    zm = jnp.mean(z, axis=-1, keepdims=True)
    zc = z - zm
    zv = jnp.mean(zc * zc, axis=-1, keepdims=True)
    zn = zc * lax.rsqrt(zv + LN_EPS) * clg_ref[...] + clb_ref[...]
    zn = zn * _sigmoid(zn)

    mixed = (jnp.dot(y_rwkv.astype(BF16), wout_ref[:D_RWKV, :].astype(BF16), preferred_element_type=F32)
             + jnp.dot(zn.astype(BF16), wout_ref[D_RWKV:, :].astype(BF16), preferred_element_type=F32))
    o_ref[...] = x_ref[...] + _mod_rows(gt_ref, tm, stride) * _rms(mixed, gpost_ref[...])


def _post_call(y, bonus, gate, u, hist, x1, ada, w, *, nb, tiles, tm, stride, mod_rows, head_major, name):
    if head_major:
        y_spec = pl.BlockSpec((None, N_PAIRS, tm, PAIR), lambda b, j: (b, 0, j, 0))
        hist_spec = pl.BlockSpec((None, HALO, D_CONV), lambda b, j: (b, 0, 0))
        hist_rows = HALO
    else:
        y_spec = _tok_spec(tm, D_RWKV, tiles)
        hist_rows = hist.shape[0]
        hist_spec = pl.BlockSpec((hist_rows, D_CONV), lambda b, j: (0, 0))
    tok512 = _tok_spec(tm, D_RWKV, tiles)
    row512 = _const_spec((1, D_RWKV))
    return pl.pallas_call(
        functools.partial(_post_body, tm=tm, stride=stride, head_major=head_major),
        grid=(nb, tiles),
        in_specs=[y_spec, tok512, tok512, tok512, hist_spec, _tok_spec(tm, D_MODEL, tiles),
                  _mod_spec(5, *mod_rows),
                  row512, row512, _const_spec((CONV_W, D_CONV)), row512, row512, row512,
                  _const_spec((D_MODEL, D_MODEL)), _const_spec((1, D_MODEL)),
                  _const_spec((D_RWKV, D_RWKV))],
        out_specs=_tok_spec(tm, D_MODEL, tiles),
        out_shape=jax.ShapeDtypeStruct(x1.shape, F32),
        scratch_shapes=[pltpu.VMEM((hist_rows + tm, D_CONV), F32)],
        compiler_params=_params(),
        name=name,
    )(y, bonus, gate, u, hist, x1, ada, w["ln_x_g"], w["ln_x_b"], w["conv_w"], w["conv_b"],
      w["conv_ln_g"], w["conv_ln_b"], w["w_out"], w["g_post1"], w["ebd"])


def _layer(x, ada, w, shift0, hist, s0, *, nb, seq, tm, tm_pre, stride, mod_rows, head_major, tag):
    tiles = (seq // tm) if head_major else 1
    tiles_pre = (seq // tm_pre) if head_major else 1
    x1 = _ffn_call(x, ada, 0, w["g_pre0"], w["g_post0"], w["wup1"], w["wdn1"], nb=nb, tiles=tiles, tm=tm,
                   stride=stride, mod_rows=mod_rows, res_w=0.5, name=tag + "_ffn1")
    r, k, v, a, b, ld, gate, u, bonus, nshift = _pre_call(
        x1, ada, w["g_pre1"], w, shift0, nb=nb, tiles=tiles_pre, tm=tm_pre, stride=stride, mod_rows=mod_rows,
        head_major=head_major, name=tag + "_mixpre")
    if head_major:
        ops = [z.reshape(nb * N_PAIRS, seq, PAIR) for z in (r, k, v, a, b, ld)]
        y, s_new = _wkv_call(ops, s0, pblk=nb * N_PAIRS, chunk=32, name=tag + "_wkv")
        y = y.reshape(nb, N_PAIRS, seq, PAIR)
    else:
        y, s_new = _wkvseq_call((r, k, v, a, b, ld), s0, steps=x.shape[0] // stride, nb=stride, name=tag + "_wkv")
    x2 = _post_call(y, bonus, gate, u, hist, x1, ada, w, nb=nb, tiles=tiles, tm=tm, stride=stride,
                    mod_rows=mod_rows, head_major=head_major, name=tag + "_mixpost")
    out = _ffn_call(x2, ada, 2, w["g_pre2"], w["g_post2"], w["wup2"], w["wdn2"], nb=nb, tiles=tiles, tm=tm,
                    stride=stride, mod_rows=mod_rows, res_w=0.5, name=tag + "_ffn2")
    return out, s_new, nshift, u


def kernel(x_prompt, x_sample, c_prompt, c_sample, state_wkv, state_shift, state_conv, w_ada, b_ada, g_pre, g_post, w_ffn1_up, w_ffn1_down, w_in, mu_shift, w0, w_decay_up, a0, w_iclr_up, w_gate_up, k_k, k_a, r_k, ln_x_g, ln_x_b, conv_w, conv_b, conv_ln_g, conv_ln_b, w_out, w_ffn2_up, w_ffn2_down):
    bp, seq, _ = x_prompt.shape
    bs, dec, _ = x_sample.shape
    row = lambda z: z.reshape(1, -1).astype(F32)
    zeros64 = jnp.zeros((ICLR_RANK, D_RWKV), BF16)
    head_of_lane = jnp.arange(D_RWKV) // HEAD
    w = dict(
        wup1=w_ffn1_up, wdn1=w_ffn1_down, wup2=w_ffn2_up, wdn2=w_ffn2_down,
        w_in=w_in, w_out=w_out,
        wd=jnp.concatenate([w_decay_up.astype(BF16), zeros64], axis=0),
        wi=jnp.concatenate([zeros64, w_iclr_up.astype(BF16)], axis=0),
        wg=w_gate_up.astype(BF16),
        mu=row(mu_shift), w0=row(w0), a0=row(a0), k_k=row(k_k), k_a=row(k_a), r_k=row(r_k),
        ln_x_g=row(ln_x_g), ln_x_b=row(ln_x_b), conv_w=conv_w, conv_b=row(conv_b),
        conv_ln_g=row(conv_ln_g), conv_ln_b=row(conv_ln_b),
        g_pre0=g_pre[0:1], g_pre1=g_pre[1:2], g_pre2=g_pre[2:3],
        g_post0=g_post[0:1], g_post1=g_post[1:2], g_post2=g_post[2:3],
        ebd=(head_of_lane[:, None] == head_of_lane[None, :]).astype(BF16),
    )

    assert bs % bp == 0
    ada = _ada_call(jnp.concatenate([c_sample, c_prompt], axis=0), w_ada, b_ada)

    yp, wkv_p, shift_p, u_p = _layer(
        x_prompt.reshape(bp * seq, D_MODEL), ada, w,
        jnp.zeros((bp, 1, N_SHIFT), F32), jnp.zeros((bp, HALO, D_CONV), F32),
        jnp.zeros((bp * N_PAIRS, 2, HEAD, HEAD), F32),
        nb=bp, seq=seq, tm=512, tm_pre=512, stride=1, mod_rows=(bp, bs), head_major=True, tag="p")
    y_p = yp.reshape(bp, seq, D_MODEL)
    wkv_p = wkv_p.reshape(bp, N_HEADS, HEAD, HEAD)
    conv_p = u_p.reshape(bp, seq, D_CONV)[:, seq - (CONV_W - 1):, :]

    xs_tok = x_sample.transpose(1, 0, 2).reshape(dec * bs, D_MODEL)
    hist_s = state_conv.transpose(1, 0, 2).reshape((CONV_W - 1) * bs, D_CONV)
    ys, wkv_s, shift_s, u_s = _layer(
        xs_tok, ada, w, state_shift.reshape(bs, N_SHIFT), hist_s,
        state_wkv.transpose(1, 2, 3, 0),
        nb=1, seq=dec * bs, tm=dec * bs, tm_pre=dec * bs, stride=bs, mod_rows=(bs, 0), head_major=False, tag="s")
    y_s = ys.reshape(dec, bs, D_MODEL).transpose(1, 0, 2)
    wkv_s = wkv_s.transpose(3, 0, 1, 2)
    shift_s = shift_s.reshape(bs, 1, N_SHIFT)
    conv_s = jnp.concatenate([state_conv, u_s.reshape(dec, bs, D_CONV).transpose(1, 0, 2)],
                             axis=1)[:, dec:, :]
    return (y_p, y_s, wkv_p, shift_p, conv_p, wkv_s, shift_s, conv_s)
```

```python
import functools

import jax
import jax.numpy as jnp
from jax import lax
from jax.experimental import pallas as pl
from jax.experimental.pallas import tpu as pltpu

F32 = jnp.float32
BF16 = jnp.bfloat16

D_MODEL = 1024
D_RWKV = 512
HEAD = 64
PAIR = 2 * HEAD
N_PAIRS = D_RWKV // PAIR
N_HEADS = D_RWKV // HEAD
D_CONV = D_MODEL - D_RWKV
CONV_W = 31
DECAY_RANK = 64
ICLR_RANK = 64
GATE_RANK = 128
D_FF = 2816
N_SHIFT = 3 * D_RWKV + DECAY_RANK + ICLR_RANK + GATE_RANK
D_IN = N_SHIFT + 2 * D_CONV
N_SUB = 3
RMS_EPS = 1e-6
LN_EPS = 1e-5
GN_EPS = 64e-5

FF_CHUNK = 256
WKV_ROWS = 256
CONV_ROWS = 128
WKV_WAYS = 8
WKV_SPAN = 2
HALO = 32
VMEM_LIMIT = 56 * 1024 * 1024

NN = (((1,), (0,)), ((), ()))
NT = (((1,), (1,)), ((), ()))
B_NT = (((2,), (2,)), ((0,), (0,)))
B_TN = (((1,), (1,)), ((0,), (0,)))


def _splits(x, n):
    if x.dtype == BF16:
        return [x]
    parts, rem = [], x
    for i in range(n):
        p = rem.astype(BF16)
        parts.append(p)
        if i + 1 < n:
            rem = rem - p.astype(F32)
    return parts


def _mm(a, b, dn, na=2, nb=2):
    ap, bp = _splits(a, na), _splits(b, nb)
    depth = max(len(ap), len(bp))
    out = None
    for i, x in enumerate(ap):
        for j, y in enumerate(bp):
            if i + j >= depth:
                continue
            t = lax.dot_general(x, y, dn, preferred_element_type=F32)
            out = t if out is None else out + t
    return out


def _rms(x, g):
    return x * lax.rsqrt(jnp.mean(x * x, axis=-1, keepdims=True) + RMS_EPS) * g


def _sigmoid(x):
    return jax.nn.sigmoid(x)


def _head_sum(x, ebd):
    return _mm(x, ebd, NN, na=2, nb=1)


def _ada_body(c_ref, w_ref, b_ref, o_ref):
    c = c_ref[...]
    s = (c * _sigmoid(c)).astype(BF16)
    o_ref[...] = jnp.dot(s, w_ref[...].astype(BF16), preferred_element_type=F32) + b_ref[...]


def _ada_call(c_all, w_ada, b_ada):
    n = c_all.shape[0]
    width = w_ada.shape[1]
    return pl.pallas_call(
        _ada_body,
        grid=(width // D_MODEL,),
        in_specs=[pl.BlockSpec((n, D_MODEL), lambda i: (0, 0)),
                  pl.BlockSpec((D_MODEL, D_MODEL), lambda i: (0, i)),
                  pl.BlockSpec((1, D_MODEL), lambda i: (0, i))],
        out_specs=pl.BlockSpec((None, n, D_MODEL), lambda i: (i, 0, 0)),
        out_shape=jax.ShapeDtypeStruct((width // D_MODEL, n, D_MODEL), F32),
        compiler_params=pltpu.CompilerParams(dimension_semantics=("arbitrary",),
                                             vmem_limit_bytes=VMEM_LIMIT),
        name="ada",
    )(c_all, w_ada, b_ada.reshape(1, width))


def _const_spec(shape):
    nd = len(shape)
    return pl.BlockSpec(shape, lambda b, j: (0,) * nd, pipeline_mode=pl.Buffered(1))


def _tok_spec(tm, width, tiles):
    return pl.BlockSpec((tm, width), lambda b, j: (b * tiles + j, 0))


def _mod_spec(slab, rows, row0):
    return pl.BlockSpec((None, rows, D_MODEL), lambda b, j: (slab, row0 // rows, 0))


def _mod_rows(ref, tm, stride):
    if stride == 1:
        return ref[pl.ds(pl.program_id(0), 1), :]
    return jnp.concatenate([ref[...]] * (tm // stride), axis=0)


def _params():
    return pltpu.CompilerParams(dimension_semantics=("arbitrary", "arbitrary"),
                                vmem_limit_bytes=VMEM_LIMIT)


def _ffn_body(x_ref, sh_ref, sc_ref, gt_ref, gpre_ref, gpost_ref, wup_ref, wdn_ref, o_ref, act_scr, *, res_w, stride):
    x = x_ref[...]
    tm = x.shape[0]
    h = _rms(x, gpre_ref[...]) * (1.0 + _mod_rows(sc_ref, tm, stride)) + _mod_rows(sh_ref, tm, stride)
    hb = h.astype(BF16)
    for j in range(D_FF // FF_CHUNK):
        lo = j * FF_CHUNK
        g = jnp.dot(hb, wup_ref[:, lo:lo + FF_CHUNK].astype(BF16), preferred_element_type=F32)
        u = jnp.dot(hb, wup_ref[:, D_FF + lo:D_FF + lo + FF_CHUNK].astype(BF16), preferred_element_type=F32)
        act_scr[:, lo:lo + FF_CHUNK] = (g * _sigmoid(g) * u).astype(BF16)
    out = None
    for j in range(D_FF // FF_CHUNK):
        lo = j * FF_CHUNK
        part = jnp.dot(act_scr[:, lo:lo + FF_CHUNK], wdn_ref[lo:lo + FF_CHUNK, :].astype(BF16), preferred_element_type=F32)
        out = part if out is None else out + part
    o_ref[...] = x + res_w * _mod_rows(gt_ref, tm, stride) * _rms(out, gpost_ref[...])


def _ffn_call(x, ada, sub, gpre, gpost, wup, wdn, *, nb, tiles, tm, stride, mod_rows, res_w, name):
    return pl.pallas_call(
        functools.partial(_ffn_body, res_w=res_w, stride=stride),
        grid=(nb, tiles),
        in_specs=[_tok_spec(tm, D_MODEL, tiles),
                  _mod_spec(3 * sub, *mod_rows), _mod_spec(3 * sub + 1, *mod_rows), _mod_spec(3 * sub + 2, *mod_rows),
                  _const_spec((1, D_MODEL)), _const_spec((1, D_MODEL)),
                  _const_spec((D_MODEL, 2 * D_FF)), _const_spec((D_FF, D_MODEL))],
        out_specs=_tok_spec(tm, D_MODEL, tiles),
        out_shape=jax.ShapeDtypeStruct(x.shape, F32),
        scratch_shapes=[pltpu.VMEM((tm, D_FF), BF16)],
        compiler_params=_params(),
        name=name,
    )(x, ada, ada, ada, gpre, gpost, wup, wdn)


def _pre_body(x_ref, sh_ref, sc_ref, gpre_ref, win_ref, mu_ref, w0_ref, wd_ref, a0_ref, wi_ref, wg_ref,
              kk_ref, ka_ref, rk_ref, ebd_ref, shift0_ref,
              r_ref, k_ref, v_ref, a_ref, b_ref, ld_ref, gate_ref, u_ref, bonus_ref, nshift_ref,
              carry_scr, *, tm, stride, head_major):
    x = x_ref[...]
    h = _rms(x, gpre_ref[...]) * (1.0 + _mod_rows(sc_ref, tm, stride)) + _mod_rows(sh_ref, tm, stride)
    proj = jnp.dot(h.astype(BF16), win_ref[...].astype(BF16), preferred_element_type=F32)
    p_sh = proj[:, :N_SHIFT]
    p_cv = proj[:, N_SHIFT:]

    if stride == 1:
        @pl.when(pl.program_id(1) == 0)
        def _():
            carry_scr[...] = shift0_ref[...]
        row = lax.broadcasted_iota(jnp.int32, (tm, 1), 0)
        prev = jnp.where(row == 0, carry_scr[...], pltpu.roll(p_sh, 1, axis=0))
        carry_scr[...] = p_sh[tm - 1:tm, :]
    else:
        prev = jnp.concatenate([shift0_ref[...], p_sh[:tm - stride, :]], axis=0)
    nshift_ref[...] = p_sh[tm - stride:tm, :]
    xs = p_sh + (prev - p_sh) * mu_ref[...]

    o1, o2, o3 = D_RWKV, 2 * D_RWKV, 3 * D_RWKV
    r, k, v = xs[:, :o1], xs[:, o1:o2], xs[:, o2:o3]
    dwda = xs[:, o3:o3 + DECAY_RANK + ICLR_RANK]
    dg = xs[:, o3 + DECAY_RANK + ICLR_RANK:]

    zdec = -(w0_ref[...] + jnp.dot(jnp.tanh(dwda).astype(BF16), wd_ref[...], preferred_element_type=F32))
    softplus = jnp.maximum(zdec, 0.0) + jnp.log1p(jnp.exp(-jnp.abs(zdec)))
    ld = -jnp.exp(-softplus - 0.5)
    iclr = _sigmoid(a0_ref[...] + jnp.dot(dwda.astype(BF16), wi_ref[...], preferred_element_type=F32))
    gate = jnp.dot(_sigmoid(dg).astype(BF16), wg_ref[...], preferred_element_type=F32)

    ebd = ebd_ref[...]
    kk = k * kk_ref[...]
    kk = kk / jnp.maximum(jnp.sqrt(_head_sum(kk * kk, ebd)), 1e-12)
    kmod = k * (1.0 + (iclr - 1.0) * ka_ref[...])
    bonus = _head_sum(r * kmod * rk_ref[...], ebd) * v

    gate_ref[...] = gate
    bonus_ref[...] = bonus
    u_ref[...] = p_cv[:, :D_CONV] * _sigmoid(p_cv[:, D_CONV:])
    outs = ((r_ref, r), (k_ref, kmod), (v_ref, v), (a_ref, -kk), (b_ref, kk * iclr), (ld_ref, ld))
    for ref, val in outs:
        if head_major:
            for pr in range(N_PAIRS):
                ref[pr] = val[:, pr * PAIR:(pr + 1) * PAIR]
        else:
            ref[...] = val


def _pre_call(x1, ada, gpre, w, shift0, *, nb, tiles, tm, stride, mod_rows, head_major, name):
    n = x1.shape[0]
    if head_major:
        seq = tiles * tm
        hm_spec = pl.BlockSpec((None, N_PAIRS, tm, PAIR), lambda b, j: (b, 0, j, 0))
        hm_shape = jax.ShapeDtypeStruct((nb, N_PAIRS, seq, PAIR), F32)
    else:
        hm_spec = _tok_spec(tm, D_RWKV, tiles)
        hm_shape = jax.ShapeDtypeStruct((n, D_RWKV), F32)
    ld_spec = _tok_spec(tm, D_RWKV, tiles)
    ld_shape = jax.ShapeDtypeStruct((n, D_RWKV), F32)
    if stride == 1:
        shift_spec = pl.BlockSpec((None, 1, N_SHIFT), lambda b, j: (b, 0, 0))
        nshift_shape = jax.ShapeDtypeStruct((nb, 1, N_SHIFT), F32)
    else:
        shift_spec = pl.BlockSpec((stride, N_SHIFT), lambda b, j: (0, 0))
        nshift_shape = jax.ShapeDtypeStruct((stride, N_SHIFT), F32)
    return pl.pallas_call(
        functools.partial(_pre_body, tm=tm, stride=stride, head_major=head_major),
        grid=(nb, tiles),
        in_specs=[_tok_spec(tm, D_MODEL, tiles), _mod_spec(3, *mod_rows), _mod_spec(4, *mod_rows),
                  _const_spec((1, D_MODEL)), _const_spec((D_MODEL, D_IN)), _const_spec((1, N_SHIFT)),
                  _const_spec((1, D_RWKV)), _const_spec((DECAY_RANK + ICLR_RANK, D_RWKV)),
                  _const_spec((1, D_RWKV)), _const_spec((DECAY_RANK + ICLR_RANK, D_RWKV)),
                  _const_spec((GATE_RANK, D_RWKV)),
                  _const_spec((1, D_RWKV)), _const_spec((1, D_RWKV)), _const_spec((1, D_RWKV)),
                  _const_spec((D_RWKV, D_RWKV)), shift_spec],
        out_specs=[hm_spec] * 6 + [ld_spec] * 3 + [shift_spec],
        out_shape=[hm_shape] * 6 + [ld_shape] * 3 + [nshift_shape],
        scratch_shapes=[pltpu.VMEM((1, N_SHIFT), F32)],
        compiler_params=_params(),
        name=name,
    )(x1, ada, ada, gpre, w["w_in"], w["mu"], w["w0"], w["wd"], w["a0"], w["wi"], w["wg"],
      w["k_k"], w["k_a"], w["r_k"], w["ebd"], shift0)


def _wkv_body(r_ref, k_ref, v_ref, a_ref, b_ref, ld_ref, s0_ref, y_ref, st_ref, s_scr, *, chunk, n_sub, ways, span):
    c = pl.program_id(1)
    rows = WKV_ROWS
    pp = rows // (2 * chunk)
    shift = chunk.bit_length() - 1

    @pl.when(c == 0)
    def _():
        zero = jnp.zeros(s0_ref.shape[:1] + (HEAD, HEAD), F32)
        s_scr[:, :HEAD, :] = jnp.concatenate([s0_ref[:, 0], zero], axis=-1)
        s_scr[:, HEAD:, :] = jnp.concatenate([zero, s0_ref[:, 1]], axis=-1)

    ri = lax.broadcasted_iota(jnp.int32, (rows, rows), 0)
    ci = lax.broadcasted_iota(jnp.int32, (rows, rows), 1)
    same = jnp.right_shift(ri, shift) == jnp.right_shift(ci, shift)
    strict = same & (ci < ri)
    incl = same & (ci <= ri)
    eye = (ri == ci).astype(F32)
    first = lax.broadcasted_iota(jnp.int32, (1, 1, PAIR), 2) < HEAD
    tpos = lax.broadcasted_iota(jnp.int32, (pp * chunk, 1), 0) & (chunk - 1)

    def cumsum(x3):
        x = x3.reshape(pp * chunk, PAIR)
        for s in range(shift):
            d = 1 << s
            x = x + jnp.where(tpos >= d, pltpu.roll(x, d, axis=0), 0.0)
        return x.reshape(pp, chunk, PAIR)

    stack = lambda x3: jnp.concatenate([jnp.where(first, x3, 0.0), jnp.where(first, 0.0, x3)], axis=1)
    flat = lambda x3: x3.reshape(rows, PAIR)
    three = lambda x: x.reshape(pp, 2 * chunk, PAIR)
    each = lambda f, *xs: [f(*t) for t in zip(*xs)]

    def sub(it, carry, t0):
        sls = [pl.ds(pl.multiple_of((it * ways + q) * pp, pp), pp) for q in range(ways)]
        tw = slice(t0, t0 + chunk)
        r, k, v, a, b, ld = ([ref[sl, tw, :] for sl in sls] for ref in (r_ref, k_ref, v_ref, a_ref, b_ref, ld_ref))
        cs = each(cumsum, ld)
        e_inv = each(lambda x: jnp.exp(-x), cs)
        at = each(lambda x, c_, l_: stack(x * jnp.exp(c_ - l_)), a, cs, ld)
        rt = each(lambda x, c_: stack(x * jnp.exp(c_)), r, cs)
        bt = each(lambda x, e: stack(x * e), b, e_inv)
        kt = each(lambda x, e: stack(x * e), k, e_inv)
        vs = each(stack, v)
        gram = each(lambda p, q, m, n: _mm(jnp.concatenate([flat(p), flat(q)], axis=0),
                                           jnp.concatenate([flat(m), flat(n)], axis=0), NT, 1, 1), at, rt, bt, kt)
        t_ab = each(lambda g: jnp.where(strict, g[:rows, :rows], 0.0).astype(BF16), gram)
        t_ak = each(lambda g: jnp.where(strict, g[:rows, rows:], 0.0), gram)
        m_rb = each(lambda g: jnp.where(incl, g[rows:, :rows], 0.0), gram)
        m_rk = each(lambda g: jnp.where(incl, g[rows:, rows:], 0.0), gram)
        inv = each(lambda t: eye + t.astype(F32), t_ab)
        pw = t_ab
        for _ in range(shift - 1):
            pw = each(lambda p: _mm(p, p, NN, 1, 1), pw)
            inv = each(lambda x, p: x + _mm(p, x, NN, 1, 1), inv, pw)
        s0 = [s_scr[sl] for sl in sls]
        uy0 = each(lambda p, q, s: _mm(jnp.concatenate([p, q], axis=1), s, B_NT, 1, 1), at, rt, s0)
        z = each(lambda o, t, x: flat(o[:, :2 * chunk, :]) + _mm(t, flat(x), NN, 1, 1), uy0, t_ak, vs)
        u = each(lambda x, z_: _mm(x, z_, NN, 1, 1), inv, z)
        res = each(lambda z_, u_, t: z_ - u_ + _mm(t, u_, NN, 1, 2), z, u, t_ab)
        u = each(lambda u_, x, r_: u_ + _mm(x, r_, NN, 1, 1), u, inv, res)
        y = each(lambda o, m, u_, n, x: three(flat(o[:, 2 * chunk:, :]) + _mm(m, u_, NN, 1, 1)
                                              + _mm(n, flat(x), NN, 1, 1)), uy0, m_rb, u, m_rk, vs)
        for sl, y_ in zip(sls, y):
            y_ref[sl, tw, :] = y_[:, :chunk, :] + y_[:, chunk:, :]
        c_last = each(lambda x: x[:, chunk - 1:chunk, :], cs)
        tail = each(lambda cl, x: jnp.exp(cl - x), c_last, cs)
        upd = each(lambda u_, x, b_, k_, t: _mm(jnp.concatenate([three(u_), x], axis=1),
                                                jnp.concatenate([stack(b_ * t), stack(k_ * t)], axis=1), B_TN, 1, 1),
                   u, vs, b, k, tail)
        for sl, s, cl, d in zip(sls, s0, c_last, upd):
            s_scr[sl] = s * jnp.exp(cl) + d
        return carry

    for ci in range(span):
        lax.fori_loop(0, n_sub // ways, functools.partial(sub, t0=ci * chunk), 0)

    @pl.when(c == pl.num_programs(1) - 1)
    def _():
        st_ref[:, 0] = s_scr[:, :HEAD, :HEAD]
        st_ref[:, 1] = s_scr[:, HEAD:, HEAD:]


def _wkv_call(ops, s0, *, pblk, chunk, span, name):
    n_pairs, t_total, _ = ops[0].shape
    pp = WKV_ROWS // (2 * chunk)
    op_spec = pl.BlockSpec((pblk, span * chunk, PAIR), lambda g, c: (g, c, 0))
    st_spec = pl.BlockSpec((pblk, 2, HEAD, HEAD), lambda g, c: (g, 0, 0, 0))
    return pl.pallas_call(
        functools.partial(_wkv_body, chunk=chunk, n_sub=pblk // pp, ways=WKV_WAYS, span=span),
        grid=(n_pairs // pblk, t_total // (span * chunk)),
        in_specs=[op_spec] * 6 + [st_spec],
        out_specs=[op_spec, st_spec],
        out_shape=[jax.ShapeDtypeStruct((n_pairs, t_total, PAIR), F32),
                   jax.ShapeDtypeStruct((n_pairs, 2, HEAD, HEAD), F32)],
        scratch_shapes=[pltpu.VMEM((pblk, PAIR, PAIR), F32)],
        compiler_params=_params(),
        name=name,
    )(*ops, s0)


def _wkvseq_body(r_ref, k_ref, v_ref, a_ref, b_ref, ld_ref, s0_ref, y_ref, st_ref, t_scr, yt_scr, *, steps, nb):
    for qi, ref in enumerate((r_ref, k_ref, v_ref, a_ref, b_ref, ld_ref)):
        for t in range(steps):
            xt = ref[t * nb:(t + 1) * nb, :].T
            t_scr[qi, t] = jnp.exp(xt) if qi == 5 else xt
    for s in range(2):
        lo = s * HEAD

        def body(vi, carry):
            st = s0_ref[s, vi]
            for t in range(steps):
                col = lambda qi: t_scr[qi, t, lo:lo + HEAD, :]
                sa = jnp.sum(st * col(3), axis=0, keepdims=True)
                vv = t_scr[2, t, pl.ds(lo + vi, 1), :]
                st = st * col(5) + sa * col(4) + vv * col(1)
                yt_scr[t, pl.ds(lo + vi, 1), :] = jnp.sum(st * col(0), axis=0, keepdims=True)
            st_ref[s, vi] = st
            return carry

        lax.fori_loop(0, HEAD, body, 0, unroll=4)
    for t in range(steps):
        y_ref[t * nb:(t + 1) * nb, :] = yt_scr[t].T


def _wkvseq_call(ops, s0, *, steps, nb, name):
    n = steps * nb
    op_spec = pl.BlockSpec((n, PAIR), lambda p: (0, p))
    st_spec = pl.BlockSpec((2, HEAD, HEAD, nb), lambda p: (p, 0, 0, 0))
    return pl.pallas_call(
        functools.partial(_wkvseq_body, steps=steps, nb=nb),
        grid=(N_PAIRS,),
        in_specs=[op_spec] * 6 + [st_spec],
        out_specs=[op_spec, st_spec],
        out_shape=[jax.ShapeDtypeStruct((n, D_RWKV), F32),
                   jax.ShapeDtypeStruct((N_HEADS, HEAD, HEAD, nb), F32)],
        scratch_shapes=[pltpu.VMEM((6, steps, PAIR, nb), F32), pltpu.VMEM((steps, PAIR, nb), F32)],
        compiler_params=pltpu.CompilerParams(dimension_semantics=("arbitrary",),
                                             vmem_limit_bytes=VMEM_LIMIT),
        name=name,
    )(*ops, s0)


def _post_body(y_ref, bonus_ref, gate_ref, u_ref, hist_ref, x_ref, gt_ref,
               lng_ref, lnb_ref, cw_ref, cb_ref, clg_ref, clb_ref, wout_ref, gpost_ref, ebd_ref,
               o_ref, full_scr, *, tm, stride, head_major):
    if head_major:
        y = jnp.concatenate([y_ref[pr] for pr in range(N_PAIRS)], axis=-1)
    else:
        y = y_ref[...]
    ebd = ebd_ref[...]
    mu = _head_sum(y, ebd) * (1.0 / HEAD)
    d = y - mu
    var = _head_sum(d * d, ebd) * (1.0 / HEAD)
    yn = d * lax.rsqrt(var + GN_EPS) * lng_ref[...] + lnb_ref[...]
    y_rwkv = (yn + bonus_ref[...]) * gate_ref[...]

    hist = full_scr.shape[0] - tm
    if stride == 1:
        @pl.when(pl.program_id(1) == 0)
        def _():
            full_scr[:hist, :] = hist_ref[...]
    else:
        full_scr[:hist, :] = hist_ref[...]
    u = u_ref[...]
    full_scr[hist:, :] = u
    base = hist - (CONV_W - 1) * stride
    offs = [base + j * stride for j in range(CONV_W)]
    if stride % 8 == 0:
        z = jnp.zeros((tm, D_CONV), F32)
        for j in range(CONV_W):
            z = z + full_scr[pl.ds(offs[j], tm), :] * cw_ref[j:j + 1, :]
    else:
        cols = []
        for lo in range(0, D_CONV, 128):
            pieces = []
            for t0 in range(0, tm, CONV_ROWS):
                zc = None
                for rsd in range(8):
                    n = CONV_ROWS + (8 if rsd else 0)
                    part = None
                    for j in [j for j in range(CONV_W) if offs[j] % 8 == rsd]:
                        term = (full_scr[pl.ds(t0 + offs[j] - rsd, n), lo:lo + 128]
                                * cw_ref[j:j + 1, lo:lo + 128])
                        part = term if part is None else part + term
                    if rsd:
                        part = pltpu.roll(part, n - rsd, axis=0)[:CONV_ROWS, :]
                    zc = part if zc is None else zc + part
                pieces.append(zc)
            cols.append(jnp.concatenate(pieces, axis=0))
        z = jnp.concatenate(cols, axis=1)
    if stride == 1:
        full_scr[:hist, :] = u[tm - hist:, :]
    z = z + cb_ref[...]
    zm = jnp.mean(z, axis=-1, keepdims=True)
    zc = z - zm
    zv = jnp.mean(zc * zc, axis=-1, keepdims=True)
    zn = zc * lax.rsqrt(zv + LN_EPS) * clg_ref[...] + clb_ref[...]
    zn = zn * _sigmoid(zn)

    mixed = (jnp.dot(y_rwkv.astype(BF16), wout_ref[:D_RWKV, :].astype(BF16), preferred_element_type=F32)
             + jnp.dot(zn.astype(BF16), wout_ref[D_RWKV:, :].astype(BF16), preferred_element_type=F32))
    o_ref[...] = x_ref[...] + _mod_rows(gt_ref, tm, stride) * _rms(mixed, gpost_ref[...])


def _post_call(y, bonus, gate, u, hist, x1, ada, w, *, nb, tiles, tm, stride, mod_rows, head_major, name):
    if head_major:
        y_spec = pl.BlockSpec((None, N_PAIRS, tm, PAIR), lambda b, j: (b, 0, j, 0))
        hist_spec = pl.BlockSpec((None, HALO, D_CONV), lambda b, j: (b, 0, 0))
        hist_rows = HALO
    else:
        y_spec = _tok_spec(tm, D_RWKV, tiles)
        hist_rows = hist.shape[0]
        hist_spec = pl.BlockSpec((hist_rows, D_CONV), lambda b, j: (0, 0))
    tok512 = _tok_spec(tm, D_RWKV, tiles)
    row512 = _const_spec((1, D_RWKV))
    return pl.pallas_call(
        functools.partial(_post_body, tm=tm, stride=stride, head_major=head_major),
        grid=(nb, tiles),
        in_specs=[y_spec, tok512, tok512, tok512, hist_spec, _tok_spec(tm, D_MODEL, tiles),
                  _mod_spec(5, *mod_rows),
                  row512, row512, _const_spec((CONV_W, D_CONV)), row512, row512, row512,
                  _const_spec((D_MODEL, D_MODEL)), _const_spec((1, D_MODEL)),
                  _const_spec((D_RWKV, D_RWKV))],
        out_specs=_tok_spec(tm, D_MODEL, tiles),
        out_shape=jax.ShapeDtypeStruct(x1.shape, F32),
        scratch_shapes=[pltpu.VMEM((hist_rows + tm, D_CONV), F32)],
        compiler_params=_params(),
        name=name,
    )(y, bonus, gate, u, hist, x1, ada, w["ln_x_g"], w["ln_x_b"], w["conv_w"], w["conv_b"],
      w["conv_ln_g"], w["conv_ln_b"], w["w_out"], w["g_post1"], w["ebd"])


def _layer(x, ada, w, shift0, hist, s0, *, nb, seq, tm, tm_pre, stride, mod_rows, head_major, tag):
    tiles = (seq // tm) if head_major else 1
    tiles_pre = (seq // tm_pre) if head_major else 1
    x1 = _ffn_call(x, ada, 0, w["g_pre0"], w["g_post0"], w["wup1"], w["wdn1"], nb=nb, tiles=tiles, tm=tm,
                   stride=stride, mod_rows=mod_rows, res_w=0.5, name=tag + "_ffn1")
    r, k, v, a, b, ld, gate, u, bonus, nshift = _pre_call(
        x1, ada, w["g_pre1"], w, shift0, nb=nb, tiles=tiles_pre, tm=tm_pre, stride=stride, mod_rows=mod_rows,
        head_major=head_major, name=tag + "_mixpre")
    if head_major:
        ops = [z.reshape(nb * N_PAIRS, seq, PAIR) for z in (r, k, v, a, b, ld)]
        y, s_new = _wkv_call(ops, s0, pblk=nb * N_PAIRS, chunk=32, span=WKV_SPAN, name=tag + "_wkv")
        y = y.reshape(nb, N_PAIRS, seq, PAIR)
    else:
        y, s_new = _wkvseq_call((r, k, v, a, b, ld), s0, steps=x.shape[0] // stride, nb=stride, name=tag + "_wkv")
    x2 = _post_call(y, bonus, gate, u, hist, x1, ada, w, nb=nb, tiles=tiles, tm=tm, stride=stride,
                    mod_rows=mod_rows, head_major=head_major, name=tag + "_mixpost")
    out = _ffn_call(x2, ada, 2, w["g_pre2"], w["g_post2"], w["wup2"], w["wdn2"], nb=nb, tiles=tiles, tm=tm,
                    stride=stride, mod_rows=mod_rows, res_w=0.5, name=tag + "_ffn2")
    return out, s_new, nshift, u


def kernel(x_prompt, x_sample, c_prompt, c_sample, state_wkv, state_shift, state_conv, w_ada, b_ada, g_pre, g_post, w_ffn1_up, w_ffn1_down, w_in, mu_shift, w0, w_decay_up, a0, w_iclr_up, w_gate_up, k_k, k_a, r_k, ln_x_g, ln_x_b, conv_w, conv_b, conv_ln_g, conv_ln_b, w_out, w_ffn2_up, w_ffn2_down):
    bp, seq, _ = x_prompt.shape
    bs, dec, _ = x_sample.shape
    row = lambda z: z.reshape(1, -1).astype(F32)
    zeros64 = jnp.zeros((ICLR_RANK, D_RWKV), BF16)
    head_of_lane = jnp.arange(D_RWKV) // HEAD
    w = dict(
        wup1=w_ffn1_up, wdn1=w_ffn1_down, wup2=w_ffn2_up, wdn2=w_ffn2_down,
        w_in=w_in, w_out=w_out,
        wd=jnp.concatenate([w_decay_up.astype(BF16), zeros64], axis=0),
        wi=jnp.concatenate([zeros64, w_iclr_up.astype(BF16)], axis=0),
        wg=w_gate_up.astype(BF16),
        mu=row(mu_shift), w0=row(w0), a0=row(a0), k_k=row(k_k), k_a=row(k_a), r_k=row(r_k),
        ln_x_g=row(ln_x_g), ln_x_b=row(ln_x_b), conv_w=conv_w, conv_b=row(conv_b),
        conv_ln_g=row(conv_ln_g), conv_ln_b=row(conv_ln_b),
        g_pre0=g_pre[0:1], g_pre1=g_pre[1:2], g_pre2=g_pre[2:3],
        g_post0=g_post[0:1], g_post1=g_post[1:2], g_post2=g_post[2:3],
        ebd=(head_of_lane[:, None] == head_of_lane[None, :]).astype(BF16),
    )

    assert bs % bp == 0
    ada = _ada_call(jnp.concatenate([c_sample, c_prompt], axis=0), w_ada, b_ada)

    yp, wkv_p, shift_p, u_p = _layer(
        x_prompt.reshape(bp * seq, D_MODEL), ada, w,
        jnp.zeros((bp, 1, N_SHIFT), F32), jnp.zeros((bp, HALO, D_CONV), F32),
        jnp.zeros((bp * N_PAIRS, 2, HEAD, HEAD), F32),
        nb=bp, seq=seq, tm=512, tm_pre=512, stride=1, mod_rows=(bp, bs), head_major=True, tag="p")
    y_p = yp.reshape(bp, seq, D_MODEL)
    wkv_p = wkv_p.reshape(bp, N_HEADS, HEAD, HEAD)
    conv_p = u_p.reshape(bp, seq, D_CONV)[:, seq - (CONV_W - 1):, :]

    xs_tok = x_sample.transpose(1, 0, 2).reshape(dec * bs, D_MODEL)
    hist_s = state_conv.transpose(1, 0, 2).reshape((CONV_W - 1) * bs, D_CONV)
    ys, wkv_s, shift_s, u_s = _layer(
        xs_tok, ada, w, state_shift.reshape(bs, N_SHIFT), hist_s,
        state_wkv.transpose(1, 2, 3, 0),
        nb=1, seq=dec * bs, tm=dec * bs, tm_pre=dec * bs, stride=bs, mod_rows=(bs, 0), head_major=False, tag="s")
    y_s = ys.reshape(dec, bs, D_MODEL).transpose(1, 0, 2)
    wkv_s = wkv_s.transpose(3, 0, 1, 2)
    shift_s = shift_s.reshape(bs, 1, N_SHIFT)
    conv_s = jnp.concatenate([state_conv, u_s.reshape(dec, bs, D_CONV).transpose(1, 0, 2)],
                             axis=1)[:, dec:, :]
    return (y_p, y_s, wkv_p, shift_p, conv_p, wkv_s, shift_s, conv_s)
```

```python
import functools

import jax
import jax.numpy as jnp
from jax import lax
from jax.experimental import pallas as pl
from jax.experimental.pallas import tpu as pltpu

F32 = jnp.float32
BF16 = jnp.bfloat16

D_MODEL = 1024
D_RWKV = 512
HEAD = 64
PAIR = 2 * HEAD
N_PAIRS = D_RWKV // PAIR
N_HEADS = D_RWKV // HEAD
D_CONV = D_MODEL - D_RWKV
CONV_W = 31
DECAY_RANK = 64
ICLR_RANK = 64
GATE_RANK = 128
D_FF = 2816
N_SHIFT = 3 * D_RWKV + DECAY_RANK + ICLR_RANK + GATE_RANK
D_IN = N_SHIFT + 2 * D_CONV
N_SUB = 3
RMS_EPS = 1e-6
LN_EPS = 1e-5
GN_EPS = 64e-5

FF_CHUNK = 256
WKV_ROWS = 256
PRE_ROWS = 256
CONV_ROWS = 128
WKV_WAYS = 8
WKV_SPAN = 2
HALO = 32
VMEM_LIMIT = 56 * 1024 * 1024

NN = (((1,), (0,)), ((), ()))
NT = (((1,), (1,)), ((), ()))
B_NT = (((2,), (2,)), ((0,), (0,)))
B_TN = (((1,), (1,)), ((0,), (0,)))


def _splits(x, n):
    if x.dtype == BF16:
        return [x]
    parts, rem = [], x
    for i in range(n):
        p = rem.astype(BF16)
        parts.append(p)
        if i + 1 < n:
            rem = rem - p.astype(F32)
    return parts


def _mm(a, b, dn, na=2, nb=2):
    ap, bp = _splits(a, na), _splits(b, nb)
    depth = max(len(ap), len(bp))
    out = None
    for i, x in enumerate(ap):
        for j, y in enumerate(bp):
            if i + j >= depth:
                continue
            t = lax.dot_general(x, y, dn, preferred_element_type=F32)
            out = t if out is None else out + t
    return out


def _rms(x, g):
    return x * lax.rsqrt(jnp.mean(x * x, axis=-1, keepdims=True) + RMS_EPS) * g


def _sigmoid(x):
    return jax.nn.sigmoid(x)


def _head_sum(x, ebd):
    return _mm(x, ebd, NN, na=2, nb=1)


def _ada_body(c_ref, w_ref, b_ref, o_ref):
    c = c_ref[...]
    s = (c * _sigmoid(c)).astype(BF16)
    o_ref[...] = jnp.dot(s, w_ref[...].astype(BF16), preferred_element_type=F32) + b_ref[...]


def _ada_call(c_all, w_ada, b_ada):
    n = c_all.shape[0]
    width = w_ada.shape[1]
    return pl.pallas_call(
        _ada_body,
        grid=(width // D_MODEL,),
        in_specs=[pl.BlockSpec((n, D_MODEL), lambda i: (0, 0)),
                  pl.BlockSpec((D_MODEL, D_MODEL), lambda i: (0, i)),
                  pl.BlockSpec((1, D_MODEL), lambda i: (0, i))],
        out_specs=pl.BlockSpec((None, n, D_MODEL), lambda i: (i, 0, 0)),
        out_shape=jax.ShapeDtypeStruct((width // D_MODEL, n, D_MODEL), F32),
        compiler_params=pltpu.CompilerParams(dimension_semantics=("arbitrary",),
                                             vmem_limit_bytes=VMEM_LIMIT),
        name="ada",
    )(c_all, w_ada, b_ada.reshape(1, width))


def _const_spec(shape):
    nd = len(shape)
    return pl.BlockSpec(shape, lambda b, j: (0,) * nd, pipeline_mode=pl.Buffered(1))


def _tok_spec(tm, width, tiles):
    return pl.BlockSpec((tm, width), lambda b, j: (b * tiles + j, 0))


def _mod_spec(slab, rows, row0):
    return pl.BlockSpec((None, rows, D_MODEL), lambda b, j: (slab, row0 // rows, 0))


def _mod_rows(ref, tm, stride):
    if stride == 1:
        return ref[pl.ds(pl.program_id(0), 1), :]
    return jnp.concatenate([ref[...]] * (tm // stride), axis=0)


def _params():
    return pltpu.CompilerParams(dimension_semantics=("arbitrary", "arbitrary"),
                                vmem_limit_bytes=VMEM_LIMIT)


def _ffn_body(x_ref, sh_ref, sc_ref, gt_ref, gpre_ref, gpost_ref, wup_ref, wdn_ref, o_ref, act_scr, *, res_w, stride):
    x = x_ref[...]
    tm = x.shape[0]
    h = _rms(x, gpre_ref[...]) * (1.0 + _mod_rows(sc_ref, tm, stride)) + _mod_rows(sh_ref, tm, stride)
    hb = h.astype(BF16)
    for j in range(D_FF // FF_CHUNK):
        lo = j * FF_CHUNK
        g = jnp.dot(hb, wup_ref[:, lo:lo + FF_CHUNK].astype(BF16), preferred_element_type=F32)
        u = jnp.dot(hb, wup_ref[:, D_FF + lo:D_FF + lo + FF_CHUNK].astype(BF16), preferred_element_type=F32)
        act_scr[:, lo:lo + FF_CHUNK] = (g * _sigmoid(g) * u).astype(BF16)
    out = None
    for j in range(D_FF // FF_CHUNK):
        lo = j * FF_CHUNK
        part = jnp.dot(act_scr[:, lo:lo + FF_CHUNK], wdn_ref[lo:lo + FF_CHUNK, :].astype(BF16), preferred_element_type=F32)
        out = part if out is None else out + part
    o_ref[...] = x + res_w * _mod_rows(gt_ref, tm, stride) * _rms(out, gpost_ref[...])


def _ffn_call(x, ada, sub, gpre, gpost, wup, wdn, *, nb, tiles, tm, stride, mod_rows, res_w, name):
    return pl.pallas_call(
        functools.partial(_ffn_body, res_w=res_w, stride=stride),
        grid=(nb, tiles),
        in_specs=[_tok_spec(tm, D_MODEL, tiles),
                  _mod_spec(3 * sub, *mod_rows), _mod_spec(3 * sub + 1, *mod_rows), _mod_spec(3 * sub + 2, *mod_rows),
                  _const_spec((1, D_MODEL)), _const_spec((1, D_MODEL)),
                  _const_spec((D_MODEL, 2 * D_FF)), _const_spec((D_FF, D_MODEL))],
        out_specs=_tok_spec(tm, D_MODEL, tiles),
        out_shape=jax.ShapeDtypeStruct(x.shape, F32),
        scratch_shapes=[pltpu.VMEM((tm, D_FF), BF16)],
        compiler_params=_params(),
        name=name,
    )(x, ada, ada, ada, gpre, gpost, wup, wdn)


def _ffn_stream_body(x_ref, sh_ref, sc_ref, gt_ref, gpre_ref, gpost_ref, wg_ref, wu_ref, wdn_ref, o_ref,
                     hb_scr, acc_scr, *, res_w, stride):
    j = pl.program_id(0)
    tm = x_ref.shape[0]

    @pl.when(j == 0)
    def _():
        h = _rms(x_ref[...], gpre_ref[...]) * (1.0 + _mod_rows(sc_ref, tm, stride)) + _mod_rows(sh_ref, tm, stride)
        hb_scr[...] = h.astype(BF16)
        acc_scr[...] = jnp.zeros_like(acc_scr)

    hb = hb_scr[...]
    g = jnp.dot(hb, wg_ref[...].astype(BF16), preferred_element_type=F32)
    u = jnp.dot(hb, wu_ref[...].astype(BF16), preferred_element_type=F32)
    act = (g * _sigmoid(g) * u).astype(BF16)
    acc_scr[...] += jnp.dot(act, wdn_ref[...].astype(BF16), preferred_element_type=F32)

    @pl.when(j == pl.num_programs(0) - 1)
    def _():
        o_ref[...] = x_ref[...] + res_w * _mod_rows(gt_ref, tm, stride) * _rms(acc_scr[...], gpost_ref[...])


def _ffn_stream_call(x, ada, sub, gpre, gpost, wup, wdn, *, stride, mod_rows, res_w, name):
    assert stride != 1
    tm = x.shape[0]
    n_chunks = D_FF // FF_CHUNK
    whole = lambda shape: pl.BlockSpec(shape, lambda j: (0,) * len(shape))
    mod = lambda slab: pl.BlockSpec((None, mod_rows[0], D_MODEL), lambda j: (slab, mod_rows[1] // mod_rows[0], 0))
    return pl.pallas_call(
        functools.partial(_ffn_stream_body, res_w=res_w, stride=stride),
        grid=(n_chunks,),
        in_specs=[whole((tm, D_MODEL)), mod(3 * sub), mod(3 * sub + 1), mod(3 * sub + 2),
                  whole((1, D_MODEL)), whole((1, D_MODEL)),
                  pl.BlockSpec((D_MODEL, FF_CHUNK), lambda j: (0, j)),
                  pl.BlockSpec((D_MODEL, FF_CHUNK), lambda j: (0, n_chunks + j)),
                  pl.BlockSpec((FF_CHUNK, D_MODEL), lambda j: (j, 0))],
        out_specs=whole((tm, D_MODEL)),
        out_shape=jax.ShapeDtypeStruct(x.shape, F32),
        scratch_shapes=[pltpu.VMEM((tm, D_MODEL), BF16), pltpu.VMEM((tm, D_MODEL), F32)],
        compiler_params=pltpu.CompilerParams(dimension_semantics=("arbitrary",), vmem_limit_bytes=VMEM_LIMIT),
        name=name,
    )(x, ada, ada, ada, gpre, gpost, wup, wup, wdn)


def _pre_body(x_ref, sh_ref, sc_ref, gpre_ref, win_ref, mu_ref, w0_ref, wd_ref, a0_ref, wi_ref, wg_ref,
              kk_ref, ka_ref, rk_ref, ebd_ref, shift0_ref,
              r_ref, k_ref, v_ref, a_ref, b_ref, ld_ref, gate_ref, u_ref, bonus_ref, nshift_ref,
              carry_scr, *, tm, stride, head_major):
    x = x_ref[...]
    h = _rms(x, gpre_ref[...]) * (1.0 + _mod_rows(sc_ref, tm, stride)) + _mod_rows(sh_ref, tm, stride)
    hb = h.astype(BF16)
    wb = win_ref[...].astype(BF16)
    rc = PRE_ROWS
    projs = [jnp.dot(hb[lo:lo + rc, :], wb, preferred_element_type=F32) for lo in range(0, tm, rc)]
    if stride == 1:
        @pl.when(pl.program_id(1) == 0)
        def _():
            carry_scr[...] = shift0_ref[...]
        first = carry_scr[...]
        row = lax.broadcasted_iota(jnp.int32, (rc, 1), 0)
    else:
        first = shift0_ref[...]
    ebd = ebd_ref[...]
    o1, o2, o3 = D_RWKV, 2 * D_RWKV, 3 * D_RWKV
    for ci, proj in enumerate(projs):
        rows = slice(ci * rc, (ci + 1) * rc)
        p_sh = proj[:, :N_SHIFT]
        p_cv = proj[:, N_SHIFT:]
        if stride == 1:
            prev = jnp.where(row == 0, first, pltpu.roll(p_sh, 1, axis=0))
        else:
            prev = jnp.concatenate([first, p_sh[:rc - stride, :]], axis=0)
        first = p_sh[rc - stride:rc, :]
        xs = p_sh + (prev - p_sh) * mu_ref[...]

        r, k, v = xs[:, :o1], xs[:, o1:o2], xs[:, o2:o3]
        dwda = xs[:, o3:o3 + DECAY_RANK + ICLR_RANK]
        dg = xs[:, o3 + DECAY_RANK + ICLR_RANK:]

        zdec = -(w0_ref[...] + jnp.dot(jnp.tanh(dwda).astype(BF16), wd_ref[...], preferred_element_type=F32))
        softplus = jnp.maximum(zdec, 0.0) + jnp.log1p(jnp.exp(-jnp.abs(zdec)))
        ld = -jnp.exp(-softplus - 0.5)
        iclr = _sigmoid(a0_ref[...] + jnp.dot(dwda.astype(BF16), wi_ref[...], preferred_element_type=F32))
        gate = jnp.dot(_sigmoid(dg).astype(BF16), wg_ref[...], preferred_element_type=F32)

        kk = k * kk_ref[...]
        kk = kk / jnp.maximum(jnp.sqrt(_head_sum(kk * kk, ebd)), 1e-12)
        kmod = k * (1.0 + (iclr - 1.0) * ka_ref[...])
        bonus = _head_sum(r * kmod * rk_ref[...], ebd) * v

        gate_ref[rows, :] = gate
        bonus_ref[rows, :] = bonus
        u_ref[rows, :] = p_cv[:, :D_CONV] * _sigmoid(p_cv[:, D_CONV:])
        outs = ((r_ref, r), (k_ref, kmod), (v_ref, v), (a_ref, -kk), (b_ref, kk * iclr), (ld_ref, ld))
        for ref, val in outs:
            if head_major:
                for pr in range(N_PAIRS):
                    ref[pr, rows, :] = val[:, pr * PAIR:(pr + 1) * PAIR]
            else:
                ref[rows, :] = val
    if stride == 1:
        carry_scr[...] = first
    nshift_ref[...] = first


def _pre_call(x1, ada, gpre, w, shift0, *, nb, tiles, tm, stride, mod_rows, head_major, name):
    n = x1.shape[0]
    if head_major:
        seq = tiles * tm
        hm_spec = pl.BlockSpec((None, N_PAIRS, tm, PAIR), lambda b, j: (b, 0, j, 0))
        hm_shape = jax.ShapeDtypeStruct((nb, N_PAIRS, seq, PAIR), F32)
    else:
        hm_spec = _tok_spec(tm, D_RWKV, tiles)
        hm_shape = jax.ShapeDtypeStruct((n, D_RWKV), F32)
    ld_spec = _tok_spec(tm, D_RWKV, tiles)
    ld_shape = jax.ShapeDtypeStruct((n, D_RWKV), F32)
    if stride == 1:
        shift_spec = pl.BlockSpec((None, 1, N_SHIFT), lambda b, j: (b, 0, 0))
        nshift_shape = jax.ShapeDtypeStruct((nb, 1, N_SHIFT), F32)
    else:
        shift_spec = pl.BlockSpec((stride, N_SHIFT), lambda b, j: (0, 0))
        nshift_shape = jax.ShapeDtypeStruct((stride, N_SHIFT), F32)
    return pl.pallas_call(
        functools.partial(_pre_body, tm=tm, stride=stride, head_major=head_major),
        grid=(nb, tiles),
        in_specs=[_tok_spec(tm, D_MODEL, tiles), _mod_spec(3, *mod_rows), _mod_spec(4, *mod_rows),
                  _const_spec((1, D_MODEL)), _const_spec((D_MODEL, D_IN)), _const_spec((1, N_SHIFT)),
                  _const_spec((1, D_RWKV)), _const_spec((DECAY_RANK + ICLR_RANK, D_RWKV)),
                  _const_spec((1, D_RWKV)), _const_spec((DECAY_RANK + ICLR_RANK, D_RWKV)),
                  _const_spec((GATE_RANK, D_RWKV)),
                  _const_spec((1, D_RWKV)), _const_spec((1, D_RWKV)), _const_spec((1, D_RWKV)),
                  _const_spec((D_RWKV, D_RWKV)), shift_spec],
        out_specs=[hm_spec] * 6 + [ld_spec] * 3 + [shift_spec],
        out_shape=[hm_shape] * 6 + [ld_shape] * 3 + [nshift_shape],
        scratch_shapes=[pltpu.VMEM((1, N_SHIFT), F32)],
        compiler_params=_params(),
        name=name,
    )(x1, ada, ada, gpre, w["w_in"], w["mu"], w["w0"], w["wd"], w["a0"], w["wi"], w["wg"],
      w["k_k"], w["k_a"], w["r_k"], w["ebd"], shift0)


def _wkv_body(r_ref, k_ref, v_ref, a_ref, b_ref, ld_ref, s0_ref, y_ref, st_ref, s_scr, *, chunk, n_sub, ways, span):
    c = pl.program_id(1)
    rows = WKV_ROWS
    pp = rows // (2 * chunk)
    shift = chunk.bit_length() - 1

    @pl.when(c == 0)
    def _():
        zero = jnp.zeros(s0_ref.shape[:1] + (HEAD, HEAD), F32)
        s_scr[:, :HEAD, :] = jnp.concatenate([s0_ref[:, 0], zero], axis=-1)
        s_scr[:, HEAD:, :] = jnp.concatenate([zero, s0_ref[:, 1]], axis=-1)

    ri = lax.broadcasted_iota(jnp.int32, (rows, rows), 0)
    ci = lax.broadcasted_iota(jnp.int32, (rows, rows), 1)
    same = jnp.right_shift(ri, shift) == jnp.right_shift(ci, shift)
    strict = same & (ci < ri)
    incl = same & (ci <= ri)
    eye = (ri == ci).astype(F32)
    first = lax.broadcasted_iota(jnp.int32, (1, 1, PAIR), 2) < HEAD
    tpos = lax.broadcasted_iota(jnp.int32, (pp * chunk, 1), 0) & (chunk - 1)

    def cumsum(x3):
        x = x3.reshape(pp * chunk, PAIR)
        for s in range(shift):
            d = 1 << s
            x = x + jnp.where(tpos >= d, pltpu.roll(x, d, axis=0), 0.0)
        return x.reshape(pp, chunk, PAIR)

    stack = lambda x3: jnp.concatenate([jnp.where(first, x3, 0.0), jnp.where(first, 0.0, x3)], axis=1)
    flat = lambda x3: x3.reshape(rows, PAIR)
    three = lambda x: x.reshape(pp, 2 * chunk, PAIR)
    each = lambda f, *xs: [f(*t) for t in zip(*xs)]

    def sub(it, carry, t0):
        sls = [pl.ds(pl.multiple_of((it * ways + q) * pp, pp), pp) for q in range(ways)]
        tw = slice(t0, t0 + chunk)
        r, k, v, a, b, ld = ([ref[sl, tw, :] for sl in sls] for ref in (r_ref, k_ref, v_ref, a_ref, b_ref, ld_ref))
        cs = each(cumsum, ld)
        e_inv = each(lambda x: jnp.exp(-x), cs)
        at = each(lambda x, c_, l_: stack(x * jnp.exp(c_ - l_)), a, cs, ld)
        rt = each(lambda x, c_: stack(x * jnp.exp(c_)), r, cs)
        bt = each(lambda x, e: stack(x * e), b, e_inv)
        kt = each(lambda x, e: stack(x * e), k, e_inv)
        vs = each(stack, v)
        gram = each(lambda p, q, m, n: _mm(jnp.concatenate([flat(p), flat(q)], axis=0),
                                           jnp.concatenate([flat(m), flat(n)], axis=0), NT, 1, 1), at, rt, bt, kt)
        t_ab = each(lambda g: jnp.where(strict, g[:rows, :rows], 0.0).astype(BF16), gram)
        t_ak = each(lambda g: jnp.where(strict, g[:rows, rows:], 0.0), gram)
        m_rb = each(lambda g: jnp.where(incl, g[rows:, :rows], 0.0), gram)
        m_rk = each(lambda g: jnp.where(incl, g[rows:, rows:], 0.0), gram)
        inv = each(lambda t: eye + t.astype(F32), t_ab)
        pw = t_ab
        for _ in range(shift - 1):
            pw = each(lambda p: _mm(p, p, NN, 1, 1), pw)
            inv = each(lambda x, p: x + _mm(p, x, NN, 1, 1), inv, pw)
        s0 = [s_scr[sl] for sl in sls]
        uy0 = each(lambda p, q, s: _mm(jnp.concatenate([p, q], axis=1), s, B_NT, 1, 1), at, rt, s0)
        z = each(lambda o, t, x: flat(o[:, :2 * chunk, :]) + _mm(t, flat(x), NN, 1, 1), uy0, t_ak, vs)
        u = each(lambda x, z_: _mm(x, z_, NN, 1, 1), inv, z)
        res = each(lambda z_, u_, t: z_ - u_ + _mm(t, u_, NN, 1, 2), z, u, t_ab)
        u = each(lambda u_, x, r_: u_ + _mm(x, r_, NN, 1, 1), u, inv, res)
        y = each(lambda o, m, u_, n, x: three(flat(o[:, 2 * chunk:, :]) + _mm(m, u_, NN, 1, 1)
                                              + _mm(n, flat(x), NN, 1, 1)), uy0, m_rb, u, m_rk, vs)
        for sl, y_ in zip(sls, y):
            y_ref[sl, tw, :] = y_[:, :chunk, :] + y_[:, chunk:, :]
        c_last = each(lambda x: x[:, chunk - 1:chunk, :], cs)
        tail = each(lambda cl, x: jnp.exp(cl - x), c_last, cs)
        upd = each(lambda u_, x, b_, k_, t: _mm(jnp.concatenate([three(u_), x], axis=1),
                                                jnp.concatenate([stack(b_ * t), stack(k_ * t)], axis=1), B_TN, 1, 1),
                   u, vs, b, k, tail)
        for sl, s, cl, d in zip(sls, s0, c_last, upd):
            s_scr[sl] = s * jnp.exp(cl) + d
        return carry

    for ci in range(span):
        lax.fori_loop(0, n_sub // ways, functools.partial(sub, t0=ci * chunk), 0)

    @pl.when(c == pl.num_programs(1) - 1)
    def _():
        st_ref[:, 0] = s_scr[:, :HEAD, :HEAD]
        st_ref[:, 1] = s_scr[:, HEAD:, HEAD:]


def _wkv_call(ops, s0, *, pblk, chunk, span, name):
    n_pairs, t_total, _ = ops[0].shape
    pp = WKV_ROWS // (2 * chunk)
    op_spec = pl.BlockSpec((pblk, span * chunk, PAIR), lambda g, c: (g, c, 0))
    st_spec = pl.BlockSpec((pblk, 2, HEAD, HEAD), lambda g, c: (g, 0, 0, 0))
    return pl.pallas_call(
        functools.partial(_wkv_body, chunk=chunk, n_sub=pblk // pp, ways=WKV_WAYS, span=span),
        grid=(n_pairs // pblk, t_total // (span * chunk)),
        in_specs=[op_spec] * 6 + [st_spec],
        out_specs=[op_spec, st_spec],
        out_shape=[jax.ShapeDtypeStruct((n_pairs, t_total, PAIR), F32),
                   jax.ShapeDtypeStruct((n_pairs, 2, HEAD, HEAD), F32)],
        scratch_shapes=[pltpu.VMEM((pblk, PAIR, PAIR), F32)],
        compiler_params=_params(),
        name=name,
    )(*ops, s0)


def _wkvseq_body(r_ref, k_ref, v_ref, a_ref, b_ref, ld_ref, s0_ref, y_ref, st_ref, t_scr, yt_scr, *, steps, nb):
    for qi, ref in enumerate((r_ref, k_ref, v_ref, a_ref, b_ref, ld_ref)):
        for t in range(steps):
            xt = ref[t * nb:(t + 1) * nb, :].T
            t_scr[qi, t] = jnp.exp(xt) if qi == 5 else xt
    for s in range(2):
        lo = s * HEAD

        def body(vi, carry):
            st = s0_ref[s, vi]
            for t in range(steps):
                col = lambda qi: t_scr[qi, t, lo:lo + HEAD, :]
                sa = jnp.sum(st * col(3), axis=0, keepdims=True)
                vv = t_scr[2, t, pl.ds(lo + vi, 1), :]
                st = st * col(5) + sa * col(4) + vv * col(1)
                yt_scr[t, pl.ds(lo + vi, 1), :] = jnp.sum(st * col(0), axis=0, keepdims=True)
            st_ref[s, vi] = st
            return carry

        lax.fori_loop(0, HEAD, body, 0, unroll=4)
    for t in range(steps):
        y_ref[t * nb:(t + 1) * nb, :] = yt_scr[t].T


def _wkvseq_call(ops, s0, *, steps, nb, name):
    n = steps * nb
    op_spec = pl.BlockSpec((n, PAIR), lambda p: (0, p))
    st_spec = pl.BlockSpec((2, HEAD, HEAD, nb), lambda p: (p, 0, 0, 0))
    return pl.pallas_call(
        functools.partial(_wkvseq_body, steps=steps, nb=nb),
        grid=(N_PAIRS,),
        in_specs=[op_spec] * 6 + [st_spec],
        out_specs=[op_spec, st_spec],
        out_shape=[jax.ShapeDtypeStruct((n, D_RWKV), F32),
                   jax.ShapeDtypeStruct((N_HEADS, HEAD, HEAD, nb), F32)],
        scratch_shapes=[pltpu.VMEM((6, steps, PAIR, nb), F32), pltpu.VMEM((steps, PAIR, nb), F32)],
        compiler_params=pltpu.CompilerParams(dimension_semantics=("arbitrary",),
                                             vmem_limit_bytes=VMEM_LIMIT),
        name=name,
    )(*ops, s0)


def _post_body(y_ref, bonus_ref, gate_ref, u_ref, hist_ref, x_ref, gt_ref,
               lng_ref, lnb_ref, cw_ref, cb_ref, clg_ref, clb_ref, wout_ref, gpost_ref, ebd_ref,
               o_ref, full_scr, *, tm, stride, head_major):
    if head_major:
        y = jnp.concatenate([y_ref[pr] for pr in range(N_PAIRS)], axis=-1)
    else:
        y = y_ref[...]
    ebd = ebd_ref[...]
    mu = _head_sum(y, ebd) * (1.0 / HEAD)
    d = y - mu
    var = _head_sum(d * d, ebd) * (1.0 / HEAD)
    yn = d * lax.rsqrt(var + GN_EPS) * lng_ref[...] + lnb_ref[...]
    y_rwkv = (yn + bonus_ref[...]) * gate_ref[...]

    hist = full_scr.shape[0] - tm
    if stride == 1:
        @pl.when(pl.program_id(1) == 0)
        def _():
            full_scr[:hist, :] = hist_ref[...]
    else:
        full_scr[:hist, :] = hist_ref[...]
    u = u_ref[...]
    full_scr[hist:, :] = u
    base = hist - (CONV_W - 1) * stride
    offs = [base + j * stride for j in range(CONV_W)]
    if stride % 8 == 0:
        z = jnp.zeros((tm, D_CONV), F32)
        for j in range(CONV_W):
            z = z + full_scr[pl.ds(offs[j], tm), :] * cw_ref[j:j + 1, :]
    else:
        cols = []
        for lo in range(0, D_CONV, 128):
            pieces = []
            for t0 in range(0, tm, CONV_ROWS):
                zc = None
                for rsd in range(8):
                    n = CONV_ROWS + (8 if rsd else 0)
                    part = None
                    for j in [j for j in range(CONV_W) if offs[j] % 8 == rsd]:
                        term = (full_scr[pl.ds(t0 + offs[j] - rsd, n), lo:lo + 128]
                                * cw_ref[j:j + 1, lo:lo + 128])
                        part = term if part is None else part + term
                    if rsd:
                        part = pltpu.roll(part, n - rsd, axis=0)[:CONV_ROWS, :]
                    zc = part if zc is None else zc + part
                pieces.append(zc)
            cols.append(jnp.concatenate(pieces, axis=0))
        z = jnp.concatenate(cols, axis=1)
    if stride == 1:
        full_scr[:hist, :] = u[tm - hist:, :]
    z = z + cb_ref[...]
    zm = jnp.mean(z, axis=-1, keepdims=True)
    zc = z - zm
    zv = jnp.mean(zc * zc, axis=-1, keepdims=True)
    zn = zc * lax.rsqrt(zv + LN_EPS) * clg_ref[...] + clb_ref[...]
    zn = zn * _sigmoid(zn)

    mixed = (jnp.dot(y_rwkv.astype(BF16), wout_ref[:D_RWKV, :].astype(BF16), preferred_element_type=F32)
             + jnp.dot(zn.astype(BF16), wout_ref[D_RWKV:, :].astype(BF16), preferred_element_type=F32))
    o_ref[...] = x_ref[...] + _mod_rows(gt_ref, tm, stride) * _rms(mixed, gpost_ref[...])


def _post_call(y, bonus, gate, u, hist, x1, ada, w, *, nb, tiles, tm, stride, mod_rows, head_major, name):
    if head_major:
        y_spec = pl.BlockSpec((None, N_PAIRS, tm, PAIR), lambda b, j: (b, 0, j, 0))
        hist_spec = pl.BlockSpec((None, HALO, D_CONV), lambda b, j: (b, 0, 0))
        hist_rows = HALO
    else:
        y_spec = _tok_spec(tm, D_RWKV, tiles)
        hist_rows = hist.shape[0]
        hist_spec = pl.BlockSpec((hist_rows, D_CONV), lambda b, j: (0, 0))
    tok512 = _tok_spec(tm, D_RWKV, tiles)
    row512 = _const_spec((1, D_RWKV))
    return pl.pallas_call(
        functools.partial(_post_body, tm=tm, stride=stride, head_major=head_major),
        grid=(nb, tiles),
        in_specs=[y_spec, tok512, tok512, tok512, hist_spec, _tok_spec(tm, D_MODEL, tiles),
                  _mod_spec(5, *mod_rows),
                  row512, row512, _const_spec((CONV_W, D_CONV)), row512, row512, row512,
                  _const_spec((D_MODEL, D_MODEL)), _const_spec((1, D_MODEL)),
                  _const_spec((D_RWKV, D_RWKV))],
        out_specs=_tok_spec(tm, D_MODEL, tiles),
        out_shape=jax.ShapeDtypeStruct(x1.shape, F32),
        scratch_shapes=[pltpu.VMEM((hist_rows + tm, D_CONV), F32)],
        compiler_params=_params(),
        name=name,
    )(y, bonus, gate, u, hist, x1, ada, w["ln_x_g"], w["ln_x_b"], w["conv_w"], w["conv_b"],
      w["conv_ln_g"], w["conv_ln_b"], w["w_out"], w["g_post1"], w["ebd"])


def _layer(x, ada, w, shift0, hist, s0, *, nb, seq, tm, tm_pre, stride, mod_rows, head_major, tag):
    tiles = (seq // tm) if head_major else 1
    tiles_pre = (seq // tm_pre) if head_major else 1
    def ffn(z, sub, wup, wdn, name):
        gpre, gpost = w["g_pre%d" % sub], w["g_post%d" % sub]
        if head_major:
            return _ffn_call(z, ada, sub, gpre, gpost, wup, wdn, nb=nb, tiles=tiles, tm=tm, stride=stride,
                             mod_rows=mod_rows, res_w=0.5, name=name)
        return _ffn_stream_call(z, ada, sub, gpre, gpost, wup, wdn, stride=stride, mod_rows=mod_rows, res_w=0.5, name=name)

    x1 = ffn(x, 0, w["wup1"], w["wdn1"], tag + "_ffn1")
    r, k, v, a, b, ld, gate, u, bonus, nshift = _pre_call(
        x1, ada, w["g_pre1"], w, shift0, nb=nb, tiles=tiles_pre, tm=tm_pre, stride=stride, mod_rows=mod_rows,
        head_major=head_major, name=tag + "_mixpre")
    if head_major:
        ops = [z.reshape(nb * N_PAIRS, seq, PAIR) for z in (r, k, v, a, b, ld)]
        y, s_new = _wkv_call(ops, s0, pblk=nb * N_PAIRS, chunk=32, span=WKV_SPAN, name=tag + "_wkv")
        y = y.reshape(nb, N_PAIRS, seq, PAIR)
    else:
        y, s_new = _wkvseq_call((r, k, v, a, b, ld), s0, steps=x.shape[0] // stride, nb=stride, name=tag + "_wkv")
    x2 = _post_call(y, bonus, gate, u, hist, x1, ada, w, nb=nb, tiles=tiles, tm=tm, stride=stride,
                    mod_rows=mod_rows, head_major=head_major, name=tag + "_mixpost")
    out = ffn(x2, 2, w["wup2"], w["wdn2"], tag + "_ffn2")
    return out, s_new, nshift, u


def kernel(x_prompt, x_sample, c_prompt, c_sample, state_wkv, state_shift, state_conv, w_ada, b_ada, g_pre, g_post, w_ffn1_up, w_ffn1_down, w_in, mu_shift, w0, w_decay_up, a0, w_iclr_up, w_gate_up, k_k, k_a, r_k, ln_x_g, ln_x_b, conv_w, conv_b, conv_ln_g, conv_ln_b, w_out, w_ffn2_up, w_ffn2_down):
    bp, seq, _ = x_prompt.shape
    bs, dec, _ = x_sample.shape
    row = lambda z: z.reshape(1, -1).astype(F32)
    zeros64 = jnp.zeros((ICLR_RANK, D_RWKV), BF16)
    head_of_lane = jnp.arange(D_RWKV) // HEAD
    w = dict(
        wup1=w_ffn1_up, wdn1=w_ffn1_down, wup2=w_ffn2_up, wdn2=w_ffn2_down,
        w_in=w_in, w_out=w_out,
        wd=jnp.concatenate([w_decay_up.astype(BF16), zeros64], axis=0),
        wi=jnp.concatenate([zeros64, w_iclr_up.astype(BF16)], axis=0),
        wg=w_gate_up.astype(BF16),
        mu=row(mu_shift), w0=row(w0), a0=row(a0), k_k=row(k_k), k_a=row(k_a), r_k=row(r_k),
        ln_x_g=row(ln_x_g), ln_x_b=row(ln_x_b), conv_w=conv_w, conv_b=row(conv_b),
        conv_ln_g=row(conv_ln_g), conv_ln_b=row(conv_ln_b),
        g_pre0=g_pre[0:1], g_pre1=g_pre[1:2], g_pre2=g_pre[2:3],
        g_post0=g_post[0:1], g_post1=g_post[1:2], g_post2=g_post[2:3],
        ebd=(head_of_lane[:, None] == head_of_lane[None, :]).astype(BF16),
    )

    assert bs % bp == 0
    ada = _ada_call(jnp.concatenate([c_sample, c_prompt], axis=0), w_ada, b_ada)

    yp, wkv_p, shift_p, u_p = _layer(
        x_prompt.reshape(bp * seq, D_MODEL), ada, w,
        jnp.zeros((bp, 1, N_SHIFT), F32), jnp.zeros((bp, HALO, D_CONV), F32),
        jnp.zeros((bp * N_PAIRS, 2, HEAD, HEAD), F32),
        nb=bp, seq=seq, tm=512, tm_pre=512, stride=1, mod_rows=(bp, bs), head_major=True, tag="p")
    y_p = yp.reshape(bp, seq, D_MODEL)
    wkv_p = wkv_p.reshape(bp, N_HEADS, HEAD, HEAD)
    conv_p = u_p.reshape(bp, seq, D_CONV)[:, seq - (CONV_W - 1):, :]

    xs_tok = x_sample.transpose(1, 0, 2).reshape(dec * bs, D_MODEL)
    hist_s = state_conv.transpose(1, 0, 2).reshape((CONV_W - 1) * bs, D_CONV)
    ys, wkv_s, shift_s, u_s = _layer(
        xs_tok, ada, w, state_shift.reshape(bs, N_SHIFT), hist_s,
        state_wkv.transpose(1, 2, 3, 0),
        nb=1, seq=dec * bs, tm=dec * bs, tm_pre=dec * bs, stride=bs, mod_rows=(bs, 0), head_major=False, tag="s")
    y_s = ys.reshape(dec, bs, D_MODEL).transpose(1, 0, 2)
    wkv_s = wkv_s.transpose(3, 0, 1, 2)
    shift_s = shift_s.reshape(bs, 1, N_SHIFT)
    conv_s = jnp.concatenate([state_conv, u_s.reshape(dec, bs, D_CONV).transpose(1, 0, 2)],
                             axis=1)[:, dec:, :]
    return (y_p, y_s, wkv_p, shift_p, conv_p, wkv_s, shift_s, conv_s)
```

```python
import functools

import jax
import jax.numpy as jnp
from jax import lax
from jax.experimental import pallas as pl
from jax.experimental.pallas import tpu as pltpu

F32 = jnp.float32
BF16 = jnp.bfloat16

D_MODEL = 1024
D_RWKV = 512
HEAD = 64
PAIR = 2 * HEAD
N_PAIRS = D_RWKV // PAIR
N_HEADS = D_RWKV // HEAD
D_CONV = D_MODEL - D_RWKV
CONV_W = 31
DECAY_RANK = 64
ICLR_RANK = 64
GATE_RANK = 128
D_FF = 2816
N_SHIFT = 3 * D_RWKV + DECAY_RANK + ICLR_RANK + GATE_RANK
D_IN = N_SHIFT + 2 * D_CONV
N_SUB = 3
RMS_EPS = 1e-6
LN_EPS = 1e-5
GN_EPS = 64e-5

FF_CHUNK = 256
WKV_ROWS = 256
PRE_ROWS = 256
CONV_ROWS = 128
WKV_WAYS = 8
WKV_SPAN = 2
HALO = 32
VMEM_LIMIT = 56 * 1024 * 1024

NN = (((1,), (0,)), ((), ()))
NT = (((1,), (1,)), ((), ()))
B_NT = (((2,), (2,)), ((0,), (0,)))
B_TN = (((1,), (1,)), ((0,), (0,)))


def _splits(x, n):
    if x.dtype == BF16:
        return [x]
    parts, rem = [], x
    for i in range(n):
        p = rem.astype(BF16)
        parts.append(p)
        if i + 1 < n:
            rem = rem - p.astype(F32)
    return parts


def _mm(a, b, dn, na=2, nb=2):
    ap, bp = _splits(a, na), _splits(b, nb)
    depth = max(len(ap), len(bp))
    out = None
    for i, x in enumerate(ap):
        for j, y in enumerate(bp):
            if i + j >= depth:
                continue
            t = lax.dot_general(x, y, dn, preferred_element_type=F32)
            out = t if out is None else out + t
    return out


def _rms(x, g):
    return x * lax.rsqrt(jnp.mean(x * x, axis=-1, keepdims=True) + RMS_EPS) * g


def _sigmoid(x):
    return jax.nn.sigmoid(x)


def _head_sum(x, ebd):
    return _mm(x, ebd, NN, na=2, nb=1)


def _ada_body(c_ref, w_ref, b_ref, o_ref):
    c = c_ref[...]
    s = (c * _sigmoid(c)).astype(BF16)
    o_ref[...] = jnp.dot(s, w_ref[...].astype(BF16), preferred_element_type=F32) + b_ref[...]


def _ada_call(c_all, w_ada, b_ada):
    n = c_all.shape[0]
    width = w_ada.shape[1]
    return pl.pallas_call(
        _ada_body,
        grid=(width // D_MODEL,),
        in_specs=[pl.BlockSpec((n, D_MODEL), lambda i: (0, 0)),
                  pl.BlockSpec((D_MODEL, D_MODEL), lambda i: (0, i)),
                  pl.BlockSpec((1, D_MODEL), lambda i: (0, i))],
        out_specs=pl.BlockSpec((None, n, D_MODEL), lambda i: (i, 0, 0)),
        out_shape=jax.ShapeDtypeStruct((width // D_MODEL, n, D_MODEL), F32),
        compiler_params=pltpu.CompilerParams(dimension_semantics=("arbitrary",),
                                             vmem_limit_bytes=VMEM_LIMIT),
        name="ada",
    )(c_all, w_ada, b_ada.reshape(1, width))


def _const_spec(shape):
    nd = len(shape)
    return pl.BlockSpec(shape, lambda b, j: (0,) * nd, pipeline_mode=pl.Buffered(1))


def _tok_spec(tm, width, tiles):
    return pl.BlockSpec((tm, width), lambda b, j: (b * tiles + j, 0))


def _mod_spec(slab, rows, row0):
    return pl.BlockSpec((None, rows, D_MODEL), lambda b, j: (slab, row0 // rows, 0))


def _mod_rows(ref, tm, stride):
    if stride == 1:
        return ref[pl.ds(pl.program_id(0), 1), :]
    return jnp.concatenate([ref[...]] * (tm // stride), axis=0)


def _params():
    return pltpu.CompilerParams(dimension_semantics=("arbitrary", "arbitrary"),
                                vmem_limit_bytes=VMEM_LIMIT)


def _ffn_body(x_ref, sh_ref, sc_ref, gt_ref, gpre_ref, gpost_ref, wup_ref, wdn_ref, o_ref, act_scr, *, res_w, stride):
    x = x_ref[...]
    tm = x.shape[0]
    h = _rms(x, gpre_ref[...]) * (1.0 + _mod_rows(sc_ref, tm, stride)) + _mod_rows(sh_ref, tm, stride)
    hb = h.astype(BF16)
    for j in range(D_FF // FF_CHUNK):
        lo = j * FF_CHUNK
        g = jnp.dot(hb, wup_ref[:, lo:lo + FF_CHUNK].astype(BF16), preferred_element_type=F32)
        u = jnp.dot(hb, wup_ref[:, D_FF + lo:D_FF + lo + FF_CHUNK].astype(BF16), preferred_element_type=F32)
        act_scr[:, lo:lo + FF_CHUNK] = (g * _sigmoid(g) * u).astype(BF16)
    out = None
    for j in range(D_FF // FF_CHUNK):
        lo = j * FF_CHUNK
        part = jnp.dot(act_scr[:, lo:lo + FF_CHUNK], wdn_ref[lo:lo + FF_CHUNK, :].astype(BF16), preferred_element_type=F32)
        out = part if out is None else out + part
    o_ref[...] = x + res_w * _mod_rows(gt_ref, tm, stride) * _rms(out, gpost_ref[...])


def _ffn_call(x, ada, sub, gpre, gpost, wup, wdn, *, nb, tiles, tm, stride, mod_rows, res_w, name):
    return pl.pallas_call(
        functools.partial(_ffn_body, res_w=res_w, stride=stride),
        grid=(nb, tiles),
        in_specs=[_tok_spec(tm, D_MODEL, tiles),
                  _mod_spec(3 * sub, *mod_rows), _mod_spec(3 * sub + 1, *mod_rows), _mod_spec(3 * sub + 2, *mod_rows),
                  _const_spec((1, D_MODEL)), _const_spec((1, D_MODEL)),
                  _const_spec((D_MODEL, 2 * D_FF)), _const_spec((D_FF, D_MODEL))],
        out_specs=_tok_spec(tm, D_MODEL, tiles),
        out_shape=jax.ShapeDtypeStruct(x.shape, F32),
        scratch_shapes=[pltpu.VMEM((tm, D_FF), BF16)],
        compiler_params=_params(),
        name=name,
    )(x, ada, ada, ada, gpre, gpost, wup, wdn)


def _pre_body(x_ref, sh_ref, sc_ref, gpre_ref, win_ref, mu_ref, w0_ref, wd_ref, a0_ref, wi_ref, wg_ref,
              kk_ref, ka_ref, rk_ref, ebd_ref, shift0_ref,
              r_ref, k_ref, v_ref, a_ref, b_ref, ld_ref, gate_ref, u_ref, bonus_ref, nshift_ref,
              carry_scr, *, tm, stride, head_major):
    x = x_ref[...]
    h = _rms(x, gpre_ref[...]) * (1.0 + _mod_rows(sc_ref, tm, stride)) + _mod_rows(sh_ref, tm, stride)
    hb = h.astype(BF16)
    wb = win_ref[...].astype(BF16)
    rc = PRE_ROWS
    projs = [jnp.dot(hb[lo:lo + rc, :], wb, preferred_element_type=F32) for lo in range(0, tm, rc)]
    if stride == 1:
        @pl.when(pl.program_id(1) == 0)
        def _():
            carry_scr[...] = shift0_ref[...]
        first = carry_scr[...]
        row = lax.broadcasted_iota(jnp.int32, (rc, 1), 0)
    else:
        first = shift0_ref[...]
    ebd = ebd_ref[...]
    o1, o2, o3 = D_RWKV, 2 * D_RWKV, 3 * D_RWKV
    for ci, proj in enumerate(projs):
        rows = slice(ci * rc, (ci + 1) * rc)
        p_sh = proj[:, :N_SHIFT]
        p_cv = proj[:, N_SHIFT:]
        if stride == 1:
            prev = jnp.where(row == 0, first, pltpu.roll(p_sh, 1, axis=0))
        else:
            prev = jnp.concatenate([first, p_sh[:rc - stride, :]], axis=0)
        first = p_sh[rc - stride:rc, :]
        xs = p_sh + (prev - p_sh) * mu_ref[...]

        r, k, v = xs[:, :o1], xs[:, o1:o2], xs[:, o2:o3]
        dwda = xs[:, o3:o3 + DECAY_RANK + ICLR_RANK]
        dg = xs[:, o3 + DECAY_RANK + ICLR_RANK:]

        zdec = -(w0_ref[...] + jnp.dot(jnp.tanh(dwda).astype(BF16), wd_ref[...], preferred_element_type=F32))
        softplus = jnp.maximum(zdec, 0.0) + jnp.log1p(jnp.exp(-jnp.abs(zdec)))
        ld = -jnp.exp(-softplus - 0.5)
        iclr = _sigmoid(a0_ref[...] + jnp.dot(dwda.astype(BF16), wi_ref[...], preferred_element_type=F32))
        gate = jnp.dot(_sigmoid(dg).astype(BF16), wg_ref[...], preferred_element_type=F32)

        kk = k * kk_ref[...]
        kk = kk / jnp.maximum(jnp.sqrt(_head_sum(kk * kk, ebd)), 1e-12)
        kmod = k * (1.0 + (iclr - 1.0) * ka_ref[...])
        bonus = _head_sum(r * kmod * rk_ref[...], ebd) * v

        gate_ref[rows, :] = gate
        bonus_ref[rows, :] = bonus
        u_ref[rows, :] = p_cv[:, :D_CONV] * _sigmoid(p_cv[:, D_CONV:])
        outs = ((r_ref, r), (k_ref, kmod), (v_ref, v), (a_ref, -kk), (b_ref, kk * iclr), (ld_ref, ld))
        for ref, val in outs:
            if head_major:
                for pr in range(N_PAIRS):
                    ref[pr, rows, :] = val[:, pr * PAIR:(pr + 1) * PAIR]
            else:
                ref[rows, :] = val
    if stride == 1:
        carry_scr[...] = first
    nshift_ref[...] = first


def _pre_call(x1, ada, gpre, w, shift0, *, nb, tiles, tm, stride, mod_rows, head_major, name):
    n = x1.shape[0]
    if head_major:
        seq = tiles * tm
        hm_spec = pl.BlockSpec((None, N_PAIRS, tm, PAIR), lambda b, j: (b, 0, j, 0))
        hm_shape = jax.ShapeDtypeStruct((nb, N_PAIRS, seq, PAIR), F32)
    else:
        hm_spec = _tok_spec(tm, D_RWKV, tiles)
        hm_shape = jax.ShapeDtypeStruct((n, D_RWKV), F32)
    ld_spec = _tok_spec(tm, D_RWKV, tiles)
    ld_shape = jax.ShapeDtypeStruct((n, D_RWKV), F32)
    if stride == 1:
        shift_spec = pl.BlockSpec((None, 1, N_SHIFT), lambda b, j: (b, 0, 0))
        nshift_shape = jax.ShapeDtypeStruct((nb, 1, N_SHIFT), F32)
    else:
        shift_spec = pl.BlockSpec((stride, N_SHIFT), lambda b, j: (0, 0))
        nshift_shape = jax.ShapeDtypeStruct((stride, N_SHIFT), F32)
    return pl.pallas_call(
        functools.partial(_pre_body, tm=tm, stride=stride, head_major=head_major),
        grid=(nb, tiles),
        in_specs=[_tok_spec(tm, D_MODEL, tiles), _mod_spec(3, *mod_rows), _mod_spec(4, *mod_rows),
                  _const_spec((1, D_MODEL)), _const_spec((D_MODEL, D_IN)), _const_spec((1, N_SHIFT)),
                  _const_spec((1, D_RWKV)), _const_spec((DECAY_RANK + ICLR_RANK, D_RWKV)),
                  _const_spec((1, D_RWKV)), _const_spec((DECAY_RANK + ICLR_RANK, D_RWKV)),
                  _const_spec((GATE_RANK, D_RWKV)),
                  _const_spec((1, D_RWKV)), _const_spec((1, D_RWKV)), _const_spec((1, D_RWKV)),
                  _const_spec((D_RWKV, D_RWKV)), shift_spec],
        out_specs=[hm_spec] * 6 + [ld_spec] * 3 + [shift_spec],
        out_shape=[hm_shape] * 6 + [ld_shape] * 3 + [nshift_shape],
        scratch_shapes=[pltpu.VMEM((1, N_SHIFT), F32)],
        compiler_params=_params(),
        name=name,
    )(x1, ada, ada, gpre, w["w_in"], w["mu"], w["w0"], w["wd"], w["a0"], w["wi"], w["wg"],
      w["k_k"], w["k_a"], w["r_k"], w["ebd"], shift0)


def _wkv_body(r_ref, k_ref, v_ref, a_ref, b_ref, ld_ref, s0_ref, y_ref, st_ref, s_scr, *, chunk, n_sub, ways, span):
    c = pl.program_id(1)
    rows = WKV_ROWS
    pp = rows // (2 * chunk)
    shift = chunk.bit_length() - 1

    @pl.when(c == 0)
    def _():
        zero = jnp.zeros(s0_ref.shape[:1] + (HEAD, HEAD), F32)
        s_scr[:, :HEAD, :] = jnp.concatenate([s0_ref[:, 0], zero], axis=-1)
        s_scr[:, HEAD:, :] = jnp.concatenate([zero, s0_ref[:, 1]], axis=-1)

    ri = lax.broadcasted_iota(jnp.int32, (rows, rows), 0)
    ci = lax.broadcasted_iota(jnp.int32, (rows, rows), 1)
    same = jnp.right_shift(ri, shift) == jnp.right_shift(ci, shift)
    strict = same & (ci < ri)
    incl = same & (ci <= ri)
    eye = (ri == ci).astype(F32)
    first = lax.broadcasted_iota(jnp.int32, (1, 1, PAIR), 2) < HEAD
    tpos = lax.broadcasted_iota(jnp.int32, (pp * chunk, 1), 0) & (chunk - 1)

    def cumsum(x3):
        x = x3.reshape(pp * chunk, PAIR)
        for s in range(shift):
            d = 1 << s
            x = x + jnp.where(tpos >= d, pltpu.roll(x, d, axis=0), 0.0)
        return x.reshape(pp, chunk, PAIR)

    stack = lambda x3: jnp.concatenate([jnp.where(first, x3, 0.0), jnp.where(first, 0.0, x3)], axis=1)
    flat = lambda x3: x3.reshape(rows, PAIR)
    three = lambda x: x.reshape(pp, 2 * chunk, PAIR)
    each = lambda f, *xs: [f(*t) for t in zip(*xs)]

    def sub(it, carry, t0):
        sls = [pl.ds(pl.multiple_of((it * ways + q) * pp, pp), pp) for q in range(ways)]
        tw = slice(t0, t0 + chunk)
        r, k, v, a, b, ld = ([ref[sl, tw, :] for sl in sls] for ref in (r_ref, k_ref, v_ref, a_ref, b_ref, ld_ref))
        cs = each(cumsum, ld)
        e_inv = each(lambda x: jnp.exp(-x), cs)
        at = each(lambda x, c_, l_: stack(x * jnp.exp(c_ - l_)), a, cs, ld)
        rt = each(lambda x, c_: stack(x * jnp.exp(c_)), r, cs)
        bt = each(lambda x, e: stack(x * e), b, e_inv)
        kt = each(lambda x, e: stack(x * e), k, e_inv)
        vs = each(stack, v)
        gram = each(lambda p, q, m, n: _mm(jnp.concatenate([flat(p), flat(q)], axis=0),
                                           jnp.concatenate([flat(m), flat(n)], axis=0), NT, 1, 1), at, rt, bt, kt)
        t_ab = each(lambda g: jnp.where(strict, g[:rows, :rows], 0.0).astype(BF16), gram)
        t_ak = each(lambda g: jnp.where(strict, g[:rows, rows:], 0.0), gram)
        m_rb = each(lambda g: jnp.where(incl, g[rows:, :rows], 0.0), gram)
        m_rk = each(lambda g: jnp.where(incl, g[rows:, rows:], 0.0), gram)
        inv = each(lambda t: eye + t.astype(F32), t_ab)
        pw = t_ab
        for _ in range(shift - 1):
            pw = each(lambda p: _mm(p, p, NN, 1, 1), pw)
            inv = each(lambda x, p: x + _mm(p, x, NN, 1, 1), inv, pw)
        s0 = [s_scr[sl] for sl in sls]
        uy0 = each(lambda p, q, s: _mm(jnp.concatenate([p, q], axis=1), s, B_NT, 1, 1), at, rt, s0)
        z = each(lambda o, t, x: flat(o[:, :2 * chunk, :]) + _mm(t, flat(x), NN, 1, 1), uy0, t_ak, vs)
        u = each(lambda x, z_: _mm(x, z_, NN, 1, 1), inv, z)
        res = each(lambda z_, u_, t: z_ - u_ + _mm(t, u_, NN, 1, 2), z, u, t_ab)
        u = each(lambda u_, x, r_: u_ + _mm(x, r_, NN, 1, 1), u, inv, res)
        y = each(lambda o, m, u_, n, x: three(flat(o[:, 2 * chunk:, :]) + _mm(m, u_, NN, 1, 1)
                                              + _mm(n, flat(x), NN, 1, 1)), uy0, m_rb, u, m_rk, vs)
        for sl, y_ in zip(sls, y):
            y_ref[sl, tw, :] = y_[:, :chunk, :] + y_[:, chunk:, :]
        c_last = each(lambda x: x[:, chunk - 1:chunk, :], cs)
        tail = each(lambda cl, x: jnp.exp(cl - x), c_last, cs)
        upd = each(lambda u_, x, b_, k_, t: _mm(jnp.concatenate([three(u_), x], axis=1),
                                                jnp.concatenate([stack(b_ * t), stack(k_ * t)], axis=1), B_TN, 1, 1),
                   u, vs, b, k, tail)
        for sl, s, cl, d in zip(sls, s0, c_last, upd):
            s_scr[sl] = s * jnp.exp(cl) + d
        return carry

    for ci in range(span):
        lax.fori_loop(0, n_sub // ways, functools.partial(sub, t0=ci * chunk), 0)

    @pl.when(c == pl.num_programs(1) - 1)
    def _():
        st_ref[:, 0] = s_scr[:, :HEAD, :HEAD]
        st_ref[:, 1] = s_scr[:, HEAD:, HEAD:]


def _wkv_call(ops, s0, *, pblk, chunk, span, name):
    n_pairs, t_total, _ = ops[0].shape
    pp = WKV_ROWS // (2 * chunk)
    op_spec = pl.BlockSpec((pblk, span * chunk, PAIR), lambda g, c: (g, c, 0))
    st_spec = pl.BlockSpec((pblk, 2, HEAD, HEAD), lambda g, c: (g, 0, 0, 0))
    return pl.pallas_call(
        functools.partial(_wkv_body, chunk=chunk, n_sub=pblk // pp, ways=WKV_WAYS, span=span),
        grid=(n_pairs // pblk, t_total // (span * chunk)),
        in_specs=[op_spec] * 6 + [st_spec],
        out_specs=[op_spec, st_spec],
        out_shape=[jax.ShapeDtypeStruct((n_pairs, t_total, PAIR), F32),
                   jax.ShapeDtypeStruct((n_pairs, 2, HEAD, HEAD), F32)],
        scratch_shapes=[pltpu.VMEM((pblk, PAIR, PAIR), F32)],
        compiler_params=_params(),
        name=name,
    )(*ops, s0)


def _wkvseq_body(r_ref, k_ref, v_ref, a_ref, b_ref, ld_ref, s0_ref, y_ref, st_ref, t_scr, yt_scr, *, steps, nb):
    for qi, ref in enumerate((r_ref, k_ref, v_ref, a_ref, b_ref, ld_ref)):
        for t in range(steps):
            xt = ref[t * nb:(t + 1) * nb, :].T
            t_scr[qi, t] = jnp.exp(xt) if qi == 5 else xt
    for s in range(2):
        lo = s * HEAD

        def body(vi, carry):
            st = s0_ref[s, vi]
            for t in range(steps):
                col = lambda qi: t_scr[qi, t, lo:lo + HEAD, :]
                sa = jnp.sum(st * col(3), axis=0, keepdims=True)
                vv = t_scr[2, t, pl.ds(lo + vi, 1), :]
                st = st * col(5) + sa * col(4) + vv * col(1)
                yt_scr[t, pl.ds(lo + vi, 1), :] = jnp.sum(st * col(0), axis=0, keepdims=True)
            st_ref[s, vi] = st
            return carry

        lax.fori_loop(0, HEAD, body, 0, unroll=4)
    for t in range(steps):
        y_ref[t * nb:(t + 1) * nb, :] = yt_scr[t].T


def _wkvseq_call(ops, s0, *, steps, nb, name):
    n = steps * nb
    op_spec = pl.BlockSpec((n, PAIR), lambda p: (0, p))
    st_spec = pl.BlockSpec((2, HEAD, HEAD, nb), lambda p: (p, 0, 0, 0))
    return pl.pallas_call(
        functools.partial(_wkvseq_body, steps=steps, nb=nb),
        grid=(N_PAIRS,),
        in_specs=[op_spec] * 6 + [st_spec],
        out_specs=[op_spec, st_spec],
        out_shape=[jax.ShapeDtypeStruct((n, D_RWKV), F32),
                   jax.ShapeDtypeStruct((N_HEADS, HEAD, HEAD, nb), F32)],
        scratch_shapes=[pltpu.VMEM((6, steps, PAIR, nb), F32), pltpu.VMEM((steps, PAIR, nb), F32)],
        compiler_params=pltpu.CompilerParams(dimension_semantics=("arbitrary",),
                                             vmem_limit_bytes=VMEM_LIMIT),
        name=name,
    )(*ops, s0)


def _post_body(y_ref, bonus_ref, gate_ref, u_ref, hist_ref, x_ref, gt_ref,
               lng_ref, lnb_ref, cw_ref, cb_ref, clg_ref, clb_ref, wout_ref, gpost_ref, ebd_ref,
               o_ref, full_scr, *, tm, stride, head_major):
    if head_major:
        y = jnp.concatenate([y_ref[pr] for pr in range(N_PAIRS)], axis=-1)
    else:
        y = y_ref[...]
    ebd = ebd_ref[...]
    mu = _head_sum(y, ebd) * (1.0 / HEAD)
    d = y - mu
    var = _head_sum(d * d, ebd) * (1.0 / HEAD)
    yn = d * lax.rsqrt(var + GN_EPS) * lng_ref[...] + lnb_ref[...]
    y_rwkv = (yn + bonus_ref[...]) * gate_ref[...]

    hist = full_scr.shape[0] - tm
    if stride == 1:
        @pl.when(pl.program_id(1) == 0)
        def _():
            full_scr[:hist, :] = hist_ref[...]
    else:
        full_scr[:hist, :] = hist_ref[...]
    u = u_ref[...]
    full_scr[hist:, :] = u
    base = hist - (CONV_W - 1) * stride
    offs = [base + j * stride for j in range(CONV_W)]
    if stride % 8 == 0:
        z = jnp.zeros((tm, D_CONV), F32)
        for j in range(CONV_W):
            z = z + full_scr[pl.ds(offs[j], tm), :] * cw_ref[j:j + 1, :]
    else:
        cols = []
        for lo in range(0, D_CONV, 128):
            pieces = []
            for t0 in range(0, tm, CONV_ROWS):
                zc = None
                for rsd in range(8):
                    n = CONV_ROWS + (8 if rsd else 0)
                    part = None
                    for j in [j for j in range(CONV_W) if offs[j] % 8 == rsd]:
                        term = (full_scr[pl.ds(t0 + offs[j] - rsd, n), lo:lo + 128]
                                * cw_ref[j:j + 1, lo:lo + 128])
                        part = term if part is None else part + term
                    if rsd:
                        part = pltpu.roll(part, n - rsd, axis=0)[:CONV_ROWS, :]
                    zc = part if zc is None else zc + part
                pieces.append(zc)
            cols.append(jnp.concatenate(pieces, axis=0))
        z = jnp.concatenate(cols, axis=1)
    if stride == 1:
        full_scr[:hist, :] = u[tm - hist:, :]
    z = z + cb_ref[...]
    zm = jnp.mean(z, axis=-1, keepdims=True)
    zc = z - zm
    zv = jnp.mean(zc * zc, axis=-1, keepdims=True)
    zn = zc * lax.rsqrt(zv + LN_EPS) * clg_ref[...] + clb_ref[...]
    zn = zn * _sigmoid(zn)

    mixed = (jnp.dot(y_rwkv.astype(BF16), wout_ref[:D_RWKV, :].astype(BF16), preferred_element_type=F32)
             + jnp.dot(zn.astype(BF16), wout_ref[D_RWKV:, :].astype(BF16), preferred_element_type=F32))
    o_ref[...] = x_ref[...] + _mod_rows(gt_ref, tm, stride) * _rms(mixed, gpost_ref[...])


def _post_call(y, bonus, gate, u, hist, x1, ada, w, *, nb, tiles, tm, stride, mod_rows, head_major, name):
    if head_major:
        y_spec = pl.BlockSpec((None, N_PAIRS, tm, PAIR), lambda b, j: (b, 0, j, 0))
        hist_spec = pl.BlockSpec((None, HALO, D_CONV), lambda b, j: (b, 0, 0))
        hist_rows = HALO
    else:
        y_spec = _tok_spec(tm, D_RWKV, tiles)
        hist_rows = hist.shape[0]
        hist_spec = pl.BlockSpec((hist_rows, D_CONV), lambda b, j: (0, 0))
    tok512 = _tok_spec(tm, D_RWKV, tiles)
    row512 = _const_spec((1, D_RWKV))
    return pl.pallas_call(
        functools.partial(_post_body, tm=tm, stride=stride, head_major=head_major),
        grid=(nb, tiles),
        in_specs=[y_spec, tok512, tok512, tok512, hist_spec, _tok_spec(tm, D_MODEL, tiles),
                  _mod_spec(5, *mod_rows),
                  row512, row512, _const_spec((CONV_W, D_CONV)), row512, row512, row512,
                  _const_spec((D_MODEL, D_MODEL)), _const_spec((1, D_MODEL)),
                  _const_spec((D_RWKV, D_RWKV))],
        out_specs=_tok_spec(tm, D_MODEL, tiles),
        out_shape=jax.ShapeDtypeStruct(x1.shape, F32),
        scratch_shapes=[pltpu.VMEM((hist_rows + tm, D_CONV), F32)],
        compiler_params=_params(),
        name=name,
    )(y, bonus, gate, u, hist, x1, ada, w["ln_x_g"], w["ln_x_b"], w["conv_w"], w["conv_b"],
      w["conv_ln_g"], w["conv_ln_b"], w["w_out"], w["g_post1"], w["ebd"])


def _layer(x, ada, w, shift0, hist, s0, *, nb, seq, tm, tm_pre, stride, mod_rows, head_major, tag):
    tiles = (seq // tm) if head_major else 1
    tiles_pre = (seq // tm_pre) if head_major else 1
    x1 = _ffn_call(x, ada, 0, w["g_pre0"], w["g_post0"], w["wup1"], w["wdn1"], nb=nb, tiles=tiles, tm=tm,
                   stride=stride, mod_rows=mod_rows, res_w=0.5, name=tag + "_ffn1")
    r, k, v, a, b, ld, gate, u, bonus, nshift = _pre_call(
        x1, ada, w["g_pre1"], w, shift0, nb=nb, tiles=tiles_pre, tm=tm_pre, stride=stride, mod_rows=mod_rows,
        head_major=head_major, name=tag + "_mixpre")
    if head_major:
        ops = [z.reshape(nb * N_PAIRS, seq, PAIR) for z in (r, k, v, a, b, ld)]
        y, s_new = _wkv_call(ops, s0, pblk=nb * N_PAIRS, chunk=32, span=WKV_SPAN, name=tag + "_wkv")
        y = y.reshape(nb, N_PAIRS, seq, PAIR)
    else:
        y, s_new = _wkvseq_call((r, k, v, a, b, ld), s0, steps=x.shape[0] // stride, nb=stride, name=tag + "_wkv")
    x2 = _post_call(y, bonus, gate, u, hist, x1, ada, w, nb=nb, tiles=tiles, tm=tm, stride=stride,
                    mod_rows=mod_rows, head_major=head_major, name=tag + "_mixpost")
    out = _ffn_call(x2, ada, 2, w["g_pre2"], w["g_post2"], w["wup2"], w["wdn2"], nb=nb, tiles=tiles, tm=tm,
                    stride=stride, mod_rows=mod_rows, res_w=0.5, name=tag + "_ffn2")
    return out, s_new, nshift, u


def kernel(x_prompt, x_sample, c_prompt, c_sample, state_wkv, state_shift, state_conv, w_ada, b_ada, g_pre, g_post, w_ffn1_up, w_ffn1_down, w_in, mu_shift, w0, w_decay_up, a0, w_iclr_up, w_gate_up, k_k, k_a, r_k, ln_x_g, ln_x_b, conv_w, conv_b, conv_ln_g, conv_ln_b, w_out, w_ffn2_up, w_ffn2_down):
    bp, seq, _ = x_prompt.shape
    bs, dec, _ = x_sample.shape
    row = lambda z: z.reshape(1, -1).astype(F32)
    zeros64 = jnp.zeros((ICLR_RANK, D_RWKV), BF16)
    head_of_lane = jnp.arange(D_RWKV) // HEAD
    w = dict(
        wup1=w_ffn1_up, wdn1=w_ffn1_down, wup2=w_ffn2_up, wdn2=w_ffn2_down,
        w_in=w_in, w_out=w_out,
        wd=jnp.concatenate([w_decay_up.astype(BF16), zeros64], axis=0),
        wi=jnp.concatenate([zeros64, w_iclr_up.astype(BF16)], axis=0),
        wg=w_gate_up.astype(BF16),
        mu=row(mu_shift), w0=row(w0), a0=row(a0), k_k=row(k_k), k_a=row(k_a), r_k=row(r_k),
        ln_x_g=row(ln_x_g), ln_x_b=row(ln_x_b), conv_w=conv_w, conv_b=row(conv_b),
        conv_ln_g=row(conv_ln_g), conv_ln_b=row(conv_ln_b),
        g_pre0=g_pre[0:1], g_pre1=g_pre[1:2], g_pre2=g_pre[2:3],
        g_post0=g_post[0:1], g_post1=g_post[1:2], g_post2=g_post[2:3],
        ebd=(head_of_lane[:, None] == head_of_lane[None, :]).astype(BF16),
    )

    assert bs % bp == 0
    ada = _ada_call(jnp.concatenate([c_sample, c_prompt], axis=0), w_ada, b_ada)

    yp, wkv_p, shift_p, u_p = _layer(
        x_prompt.reshape(bp * seq, D_MODEL), ada, w,
        jnp.zeros((bp, 1, N_SHIFT), F32), jnp.zeros((bp, HALO, D_CONV), F32),
        jnp.zeros((bp * N_PAIRS, 2, HEAD, HEAD), F32),
        nb=bp, seq=seq, tm=512, tm_pre=512, stride=1, mod_rows=(bp, bs), head_major=True, tag="p")
    y_p = yp.reshape(bp, seq, D_MODEL)
    wkv_p = wkv_p.reshape(bp, N_HEADS, HEAD, HEAD)
    conv_p = u_p.reshape(bp, seq, D_CONV)[:, seq - (CONV_W - 1):, :]

    xs_tok = x_sample.transpose(1, 0, 2).reshape(dec * bs, D_MODEL)
    hist_s = state_conv.transpose(1, 0, 2).reshape((CONV_W - 1) * bs, D_CONV)
    ys, wkv_s, shift_s, u_s = _layer(
        xs_tok, ada, w, state_shift.reshape(bs, N_SHIFT), hist_s,
        state_wkv.transpose(1, 2, 3, 0),
        nb=1, seq=dec * bs, tm=dec * bs, tm_pre=dec * bs, stride=bs, mod_rows=(bs, 0), head_major=False, tag="s")
    y_s = ys.reshape(dec, bs, D_MODEL).transpose(1, 0, 2)
    wkv_s = wkv_s.transpose(3, 0, 1, 2)
    shift_s = shift_s.reshape(bs, 1, N_SHIFT)
    conv_s = jnp.concatenate([state_conv, u_s.reshape(dec, bs, D_CONV).transpose(1, 0, 2)],
                             axis=1)[:, dec:, :]
    return (y_p, y_s, wkv_p, shift_p, conv_p, wkv_s, shift_s, conv_s)
```

```python
import functools

import jax
import jax.numpy as jnp
from jax import lax
from jax.experimental import pallas as pl
from jax.experimental.pallas import tpu as pltpu

F32 = jnp.float32
BF16 = jnp.bfloat16

D_MODEL = 1024
D_RWKV = 512
HEAD = 64
LANES = 128
PAIR = 2 * HEAD
N_PAIRS = D_RWKV // PAIR
N_HEADS = D_RWKV // HEAD
D_CONV = D_MODEL - D_RWKV
CONV_W = 31
DECAY_RANK = 64
ICLR_RANK = 64
GATE_RANK = 128
D_FF = 2816
N_SHIFT = 3 * D_RWKV + DECAY_RANK + ICLR_RANK + GATE_RANK
D_IN = N_SHIFT + 2 * D_CONV
N_SUB = 3
RMS_EPS = 1e-6
LN_EPS = 1e-5
GN_EPS = 64e-5

FF_CHUNK = 256
WKV_ROWS = 256
PRE_ROWS = 256
CONV_ROWS = 128
WKV_WAYS = 8
WKV_SPAN = 2
HALO = 32
VMEM_LIMIT = 56 * 1024 * 1024

NN = (((1,), (0,)), ((), ()))
NT = (((1,), (1,)), ((), ()))
B_NT = (((2,), (2,)), ((0,), (0,)))
B_TN = (((1,), (1,)), ((0,), (0,)))


def _splits(x, n):
    if x.dtype == BF16:
        return [x]
    parts, rem = [], x
    for i in range(n):
        p = rem.astype(BF16)
        parts.append(p)
        if i + 1 < n:
            rem = rem - p.astype(F32)
    return parts


def _mm(a, b, dn, na=2, nb=2):
    ap, bp = _splits(a, na), _splits(b, nb)
    depth = max(len(ap), len(bp))
    out = None
    for i, x in enumerate(ap):
        for j, y in enumerate(bp):
            if i + j >= depth:
                continue
            t = lax.dot_general(x, y, dn, preferred_element_type=F32)
            out = t if out is None else out + t
    return out


def _rms(x, g):
    return x * lax.rsqrt(jnp.mean(x * x, axis=-1, keepdims=True) + RMS_EPS) * g


def _sigmoid(x):
    return jax.nn.sigmoid(x)


def _head_sum(x, ebd):
    return _mm(x, ebd, NN, na=2, nb=1)


def _ada_body(c_ref, w_ref, b_ref, o_ref):
    c = c_ref[...]
    s = (c * _sigmoid(c)).astype(BF16)
    o_ref[...] = jnp.dot(s, w_ref[...].astype(BF16), preferred_element_type=F32) + b_ref[...]


def _ada_call(c_all, w_ada, b_ada):
    n = c_all.shape[0]
    width = w_ada.shape[1]
    return pl.pallas_call(
        _ada_body,
        grid=(width // D_MODEL,),
        in_specs=[pl.BlockSpec((n, D_MODEL), lambda i: (0, 0)),
                  pl.BlockSpec((D_MODEL, D_MODEL), lambda i: (0, i)),
                  pl.BlockSpec((1, D_MODEL), lambda i: (0, i))],
        out_specs=pl.BlockSpec((None, n, D_MODEL), lambda i: (i, 0, 0)),
        out_shape=jax.ShapeDtypeStruct((width // D_MODEL, n, D_MODEL), F32),
        compiler_params=pltpu.CompilerParams(dimension_semantics=("arbitrary",),
                                             vmem_limit_bytes=VMEM_LIMIT),
        name="ada",
    )(c_all, w_ada, b_ada.reshape(1, width))


def _const_spec(shape):
    nd = len(shape)
    return pl.BlockSpec(shape, lambda b, j: (0,) * nd, pipeline_mode=pl.Buffered(1))


def _tok_spec(tm, width, tiles):
    return pl.BlockSpec((tm, width), lambda b, j: (b * tiles + j, 0))


def _mod_spec(slab, rows, row0):
    return pl.BlockSpec((None, rows, D_MODEL), lambda b, j: (slab, row0 // rows, 0))


def _mod_rows(ref, tm, stride):
    if stride == 1:
        return ref[pl.ds(pl.program_id(0), 1), :]
    return jnp.concatenate([ref[...]] * (tm // stride), axis=0)


def _params():
    return pltpu.CompilerParams(dimension_semantics=("arbitrary", "arbitrary"),
                                vmem_limit_bytes=VMEM_LIMIT)


def _ffn_body(x_ref, sh_ref, sc_ref, gt_ref, gpre_ref, gpost_ref, wup_ref, wdn_ref, o_ref, act_scr, *, res_w, stride):
    x = x_ref[...]
    tm = x.shape[0]
    h = _rms(x, gpre_ref[...]) * (1.0 + _mod_rows(sc_ref, tm, stride)) + _mod_rows(sh_ref, tm, stride)
    hb = h.astype(BF16)
    for j in range(D_FF // FF_CHUNK):
        lo = j * FF_CHUNK
        g = jnp.dot(hb, wup_ref[:, lo:lo + FF_CHUNK].astype(BF16), preferred_element_type=F32)
        u = jnp.dot(hb, wup_ref[:, D_FF + lo:D_FF + lo + FF_CHUNK].astype(BF16), preferred_element_type=F32)
        act_scr[:, lo:lo + FF_CHUNK] = (g * _sigmoid(g) * u).astype(BF16)
    out = None
    for j in range(D_FF // FF_CHUNK):
        lo = j * FF_CHUNK
        part = jnp.dot(act_scr[:, lo:lo + FF_CHUNK], wdn_ref[lo:lo + FF_CHUNK, :].astype(BF16), preferred_element_type=F32)
        out = part if out is None else out + part
    o_ref[...] = x + res_w * _mod_rows(gt_ref, tm, stride) * _rms(out, gpost_ref[...])


def _ffn_call(x, ada, sub, gpre, gpost, wup, wdn, *, nb, tiles, tm, stride, mod_rows, res_w, name):
    return pl.pallas_call(
        functools.partial(_ffn_body, res_w=res_w, stride=stride),
        grid=(nb, tiles),
        in_specs=[_tok_spec(tm, D_MODEL, tiles),
                  _mod_spec(3 * sub, *mod_rows), _mod_spec(3 * sub + 1, *mod_rows), _mod_spec(3 * sub + 2, *mod_rows),
                  _const_spec((1, D_MODEL)), _const_spec((1, D_MODEL)),
                  _const_spec((D_MODEL, 2 * D_FF)), _const_spec((D_FF, D_MODEL))],
        out_specs=_tok_spec(tm, D_MODEL, tiles),
        out_shape=jax.ShapeDtypeStruct(x.shape, F32),
        scratch_shapes=[pltpu.VMEM((tm, D_FF), BF16)],
        compiler_params=_params(),
        name=name,
    )(x, ada, ada, ada, gpre, gpost, wup, wdn)


def _pre_body(x_ref, sh_ref, sc_ref, gpre_ref, win_ref, mu_ref, w0_ref, wd_ref, a0_ref, wi_ref, wg_ref,
              kk_ref, ka_ref, rk_ref, ebd_ref, shift0_ref,
              r_ref, k_ref, v_ref, a_ref, b_ref, ld_ref, gate_ref, u_ref, bonus_ref, nshift_ref,
              carry_scr, *, tm, stride, head_major):
    x = x_ref[...]
    h = _rms(x, gpre_ref[...]) * (1.0 + _mod_rows(sc_ref, tm, stride)) + _mod_rows(sh_ref, tm, stride)
    hb = h.astype(BF16)
    wb = win_ref[...].astype(BF16)
    rc = PRE_ROWS
    projs = [jnp.dot(hb[lo:lo + rc, :], wb, preferred_element_type=F32) for lo in range(0, tm, rc)]
    if stride == 1:
        @pl.when(pl.program_id(1) == 0)
        def _():
            carry_scr[...] = shift0_ref[...]
        first = carry_scr[...]
        row = lax.broadcasted_iota(jnp.int32, (rc, 1), 0)
    else:
        first = shift0_ref[...]
    ebd = ebd_ref[...]
    o1, o2, o3 = D_RWKV, 2 * D_RWKV, 3 * D_RWKV
    for ci, proj in enumerate(projs):
        rows = slice(ci * rc, (ci + 1) * rc)
        p_sh = proj[:, :N_SHIFT]
        p_cv = proj[:, N_SHIFT:]
        if stride == 1:
            prev = jnp.where(row == 0, first, pltpu.roll(p_sh, 1, axis=0))
        else:
            prev = jnp.concatenate([first, p_sh[:rc - stride, :]], axis=0)
        first = p_sh[rc - stride:rc, :]
        xs = p_sh + (prev - p_sh) * mu_ref[...]

        r, k, v = xs[:, :o1], xs[:, o1:o2], xs[:, o2:o3]
        dwda = xs[:, o3:o3 + DECAY_RANK + ICLR_RANK]
        dg = xs[:, o3 + DECAY_RANK + ICLR_RANK:]

        zdec = -(w0_ref[...] + jnp.dot(jnp.tanh(dwda).astype(BF16), wd_ref[...], preferred_element_type=F32))
        softplus = jnp.maximum(zdec, 0.0) + jnp.log1p(jnp.exp(-jnp.abs(zdec)))
        ld = -jnp.exp(-softplus - 0.5)
        iclr = _sigmoid(a0_ref[...] + jnp.dot(dwda.astype(BF16), wi_ref[...], preferred_element_type=F32))
        gate = jnp.dot(_sigmoid(dg).astype(BF16), wg_ref[...], preferred_element_type=F32)

        kk = k * kk_ref[...]
        kk = kk / jnp.maximum(jnp.sqrt(_head_sum(kk * kk, ebd)), 1e-12)
        kmod = k * (1.0 + (iclr - 1.0) * ka_ref[...])
        bonus = _head_sum(r * kmod * rk_ref[...], ebd) * v

        gate_ref[rows, :] = gate
        bonus_ref[rows, :] = bonus
        u_ref[rows, :] = p_cv[:, :D_CONV] * _sigmoid(p_cv[:, D_CONV:])
        outs = ((r_ref, r), (k_ref, kmod), (v_ref, v), (a_ref, -kk), (b_ref, kk * iclr), (ld_ref, ld))
        for ref, val in outs:
            if head_major:
                for pr in range(N_PAIRS):
                    ref[pr, rows, :] = val[:, pr * PAIR:(pr + 1) * PAIR]
            else:
                ref[rows, :] = val
    if stride == 1:
        carry_scr[...] = first
    nshift_ref[...] = first


def _pre_call(x1, ada, gpre, w, shift0, *, nb, tiles, tm, stride, mod_rows, head_major, name):
    n = x1.shape[0]
    if head_major:
        seq = tiles * tm
        hm_spec = pl.BlockSpec((None, N_PAIRS, tm, PAIR), lambda b, j: (b, 0, j, 0))
        hm_shape = jax.ShapeDtypeStruct((nb, N_PAIRS, seq, PAIR), F32)
    else:
        hm_spec = _tok_spec(tm, D_RWKV, tiles)
        hm_shape = jax.ShapeDtypeStruct((n, D_RWKV), F32)
    ld_spec = _tok_spec(tm, D_RWKV, tiles)
    ld_shape = jax.ShapeDtypeStruct((n, D_RWKV), F32)
    if stride == 1:
        shift_spec = pl.BlockSpec((None, 1, N_SHIFT), lambda b, j: (b, 0, 0))
        nshift_shape = jax.ShapeDtypeStruct((nb, 1, N_SHIFT), F32)
    else:
        shift_spec = pl.BlockSpec((stride, N_SHIFT), lambda b, j: (0, 0))
        nshift_shape = jax.ShapeDtypeStruct((stride, N_SHIFT), F32)
    return pl.pallas_call(
        functools.partial(_pre_body, tm=tm, stride=stride, head_major=head_major),
        grid=(nb, tiles),
        in_specs=[_tok_spec(tm, D_MODEL, tiles), _mod_spec(3, *mod_rows), _mod_spec(4, *mod_rows),
                  _const_spec((1, D_MODEL)), _const_spec((D_MODEL, D_IN)), _const_spec((1, N_SHIFT)),
                  _const_spec((1, D_RWKV)), _const_spec((DECAY_RANK + ICLR_RANK, D_RWKV)),
                  _const_spec((1, D_RWKV)), _const_spec((DECAY_RANK + ICLR_RANK, D_RWKV)),
                  _const_spec((GATE_RANK, D_RWKV)),
                  _const_spec((1, D_RWKV)), _const_spec((1, D_RWKV)), _const_spec((1, D_RWKV)),
                  _const_spec((D_RWKV, D_RWKV)), shift_spec],
        out_specs=[hm_spec] * 6 + [ld_spec] * 3 + [shift_spec],
        out_shape=[hm_shape] * 6 + [ld_shape] * 3 + [nshift_shape],
        scratch_shapes=[pltpu.VMEM((1, N_SHIFT), F32)],
        compiler_params=_params(),
        name=name,
    )(x1, ada, ada, gpre, w["w_in"], w["mu"], w["w0"], w["wd"], w["a0"], w["wi"], w["wg"],
      w["k_k"], w["k_a"], w["r_k"], w["ebd"], shift0)


def _wkv_body(r_ref, k_ref, v_ref, a_ref, b_ref, ld_ref, s0_ref, y_ref, st_ref, s_scr, *, chunk, n_sub, ways, span):
    c = pl.program_id(1)
    rows = WKV_ROWS
    pp = rows // (2 * chunk)
    shift = chunk.bit_length() - 1

    @pl.when(c == 0)
    def _():
        zero = jnp.zeros(s0_ref.shape[:1] + (HEAD, HEAD), F32)
        s_scr[:, :HEAD, :] = jnp.concatenate([s0_ref[:, 0], zero], axis=-1)
        s_scr[:, HEAD:, :] = jnp.concatenate([zero, s0_ref[:, 1]], axis=-1)

    ri = lax.broadcasted_iota(jnp.int32, (rows, rows), 0)
    ci = lax.broadcasted_iota(jnp.int32, (rows, rows), 1)
    same = jnp.right_shift(ri, shift) == jnp.right_shift(ci, shift)
    strict = same & (ci < ri)
    incl = same & (ci <= ri)
    eye = (ri == ci).astype(F32)
    first = lax.broadcasted_iota(jnp.int32, (1, 1, PAIR), 2) < HEAD
    tpos = lax.broadcasted_iota(jnp.int32, (pp * chunk, 1), 0) & (chunk - 1)

    def cumsum(x3):
        x = x3.reshape(pp * chunk, PAIR)
        for s in range(shift):
            d = 1 << s
            x = x + jnp.where(tpos >= d, pltpu.roll(x, d, axis=0), 0.0)
        return x.reshape(pp, chunk, PAIR)

    stack = lambda x3: jnp.concatenate([jnp.where(first, x3, 0.0), jnp.where(first, 0.0, x3)], axis=1)
    flat = lambda x3: x3.reshape(rows, PAIR)
    three = lambda x: x.reshape(pp, 2 * chunk, PAIR)
    each = lambda f, *xs: [f(*t) for t in zip(*xs)]

    def sub(it, carry, t0):
        sls = [pl.ds(pl.multiple_of((it * ways + q) * pp, pp), pp) for q in range(ways)]
        tw = slice(t0, t0 + chunk)
        r, k, v, a, b, ld = ([ref[sl, tw, :] for sl in sls] for ref in (r_ref, k_ref, v_ref, a_ref, b_ref, ld_ref))
        cs = each(cumsum, ld)
        e_inv = each(lambda x: jnp.exp(-x), cs)
        at = each(lambda x, c_, l_: stack(x * jnp.exp(c_ - l_)), a, cs, ld)
        rt = each(lambda x, c_: stack(x * jnp.exp(c_)), r, cs)
        bt = each(lambda x, e: stack(x * e), b, e_inv)
        kt = each(lambda x, e: stack(x * e), k, e_inv)
        vs = each(stack, v)
        gram = each(lambda p, q, m, n: _mm(jnp.concatenate([flat(p), flat(q)], axis=0),
                                           jnp.concatenate([flat(m), flat(n)], axis=0), NT, 1, 1), at, rt, bt, kt)
        t_ab = each(lambda g: jnp.where(strict, g[:rows, :rows], 0.0).astype(BF16), gram)
        t_ak = each(lambda g: jnp.where(strict, g[:rows, rows:], 0.0), gram)
        m_rb = each(lambda g: jnp.where(incl, g[rows:, :rows], 0.0), gram)
        m_rk = each(lambda g: jnp.where(incl, g[rows:, rows:], 0.0), gram)
        inv = each(lambda t: eye + t.astype(F32), t_ab)
        pw = t_ab
        for _ in range(shift - 1):
            pw = each(lambda p: _mm(p, p, NN, 1, 1), pw)
            inv = each(lambda x, p: x + _mm(p, x, NN, 1, 1), inv, pw)
        s0 = [s_scr[sl] for sl in sls]
        uy0 = each(lambda p, q, s: _mm(jnp.concatenate([p, q], axis=1), s, B_NT, 1, 1), at, rt, s0)
        z = each(lambda o, t, x: flat(o[:, :2 * chunk, :]) + _mm(t, flat(x), NN, 1, 1), uy0, t_ak, vs)
        u = each(lambda x, z_: _mm(x, z_, NN, 1, 1), inv, z)
        res = each(lambda z_, u_, t: z_ - u_ + _mm(t, u_, NN, 1, 2), z, u, t_ab)
        u = each(lambda u_, x, r_: u_ + _mm(x, r_, NN, 1, 1), u, inv, res)
        y = each(lambda o, m, u_, n, x: three(flat(o[:, 2 * chunk:, :]) + _mm(m, u_, NN, 1, 1)
                                              + _mm(n, flat(x), NN, 1, 1)), uy0, m_rb, u, m_rk, vs)
        for sl, y_ in zip(sls, y):
            y_ref[sl, tw, :] = y_[:, :chunk, :] + y_[:, chunk:, :]
        c_last = each(lambda x: x[:, chunk - 1:chunk, :], cs)
        tail = each(lambda cl, x: jnp.exp(cl - x), c_last, cs)
        upd = each(lambda u_, x, b_, k_, t: _mm(jnp.concatenate([three(u_), x], axis=1),
                                                jnp.concatenate([stack(b_ * t), stack(k_ * t)], axis=1), B_TN, 1, 1),
                   u, vs, b, k, tail)
        for sl, s, cl, d in zip(sls, s0, c_last, upd):
            s_scr[sl] = s * jnp.exp(cl) + d
        return carry

    for ci in range(span):
        lax.fori_loop(0, n_sub // ways, functools.partial(sub, t0=ci * chunk), 0)

    @pl.when(c == pl.num_programs(1) - 1)
    def _():
        st_ref[:, 0] = s_scr[:, :HEAD, :HEAD]
        st_ref[:, 1] = s_scr[:, HEAD:, HEAD:]


def _wkv_call(ops, s0, *, pblk, chunk, span, name):
    n_pairs, t_total, _ = ops[0].shape
    pp = WKV_ROWS // (2 * chunk)
    op_spec = pl.BlockSpec((pblk, span * chunk, PAIR), lambda g, c: (g, c, 0))
    st_spec = pl.BlockSpec((pblk, 2, HEAD, HEAD), lambda g, c: (g, 0, 0, 0))
    return pl.pallas_call(
        functools.partial(_wkv_body, chunk=chunk, n_sub=pblk // pp, ways=WKV_WAYS, span=span),
        grid=(n_pairs // pblk, t_total // (span * chunk)),
        in_specs=[op_spec] * 6 + [st_spec],
        out_specs=[op_spec, st_spec],
        out_shape=[jax.ShapeDtypeStruct((n_pairs, t_total, PAIR), F32),
                   jax.ShapeDtypeStruct((n_pairs, 2, HEAD, HEAD), F32)],
        scratch_shapes=[pltpu.VMEM((pblk, PAIR, PAIR), F32)],
        compiler_params=_params(),
        name=name,
    )(*ops, s0)


def _wkvseq_body(r_ref, k_ref, v_ref, a_ref, b_ref, ld_ref, s0_ref, y_ref, st_ref, t_scr, yt_scr, *, steps, nb):
    for qi, ref in enumerate((r_ref, k_ref, v_ref, a_ref, b_ref, ld_ref)):
        for t in range(steps):
            xt = ref[t * nb:(t + 1) * nb, :].T
            t_scr[qi, t] = jnp.exp(xt) if qi == 5 else xt
    for s in range(2):
        lo = s * HEAD

        def body(vi, carry):
            st = s0_ref[s, vi]
            for t in range(steps):
                col = lambda qi: t_scr[qi, t, lo:lo + HEAD, :]
                sa = jnp.sum(st * col(3), axis=0, keepdims=True)
                vv = t_scr[2, t, pl.ds(lo + vi, 1), :]
                st = st * col(5) + sa * col(4) + vv * col(1)
                yt_scr[t, pl.ds(lo + vi, 1), :] = jnp.sum(st * col(0), axis=0, keepdims=True)
            st_ref[s, vi] = st
            return carry

        lax.fori_loop(0, HEAD, body, 0, unroll=4)
    for t in range(steps):
        y_ref[t * nb:(t + 1) * nb, :] = yt_scr[t].T


def _wkvseq_call(ops, s0, *, steps, nb, name):
    n = steps * nb
    op_spec = pl.BlockSpec((n, PAIR), lambda p: (0, p))
    st_spec = pl.BlockSpec((2, HEAD, HEAD, nb), lambda p: (p, 0, 0, 0))
    return pl.pallas_call(
        functools.partial(_wkvseq_body, steps=steps, nb=nb),
        grid=(N_PAIRS,),
        in_specs=[op_spec] * 6 + [st_spec],
        out_specs=[op_spec, st_spec],
        out_shape=[jax.ShapeDtypeStruct((n, D_RWKV), F32),
                   jax.ShapeDtypeStruct((N_HEADS, HEAD, HEAD, nb), F32)],
        scratch_shapes=[pltpu.VMEM((6, steps, PAIR, nb), F32), pltpu.VMEM((steps, PAIR, nb), F32)],
        compiler_params=pltpu.CompilerParams(dimension_semantics=("arbitrary",),
                                             vmem_limit_bytes=VMEM_LIMIT),
        name=name,
    )(*ops, s0)


def _post_body(y_ref, bonus_ref, gate_ref, u_ref, hist_ref, x_ref, gt_ref,
               lng_ref, lnb_ref, cw_ref, cb_ref, clg_ref, clb_ref, wout_ref, gpost_ref, ebd_ref,
               o_ref, full_scr, *, tm, stride, head_major):
    if head_major:
        y = jnp.concatenate([y_ref[pr] for pr in range(N_PAIRS)], axis=-1)
    else:
        y = y_ref[...]
    ebd = ebd_ref[...]
    mu = _head_sum(y, ebd) * (1.0 / HEAD)
    d = y - mu
    var = _head_sum(d * d, ebd) * (1.0 / HEAD)
    yn = d * lax.rsqrt(var + GN_EPS) * lng_ref[...] + lnb_ref[...]
    y_rwkv = (yn + bonus_ref[...]) * gate_ref[...]

    hist = full_scr.shape[0] - tm
    if stride == 1:
        @pl.when(pl.program_id(1) == 0)
        def _():
            full_scr[:hist, :] = hist_ref[...]
    else:
        full_scr[:hist, :] = hist_ref[...]
    u = u_ref[...]
    full_scr[hist:, :] = u
    base = hist - (CONV_W - 1) * stride
    offs = [base + j * stride for j in range(CONV_W)]
    if stride % 8 == 0:
        z = jnp.zeros((tm, D_CONV), F32)
        for j in range(CONV_W):
            z = z + full_scr[pl.ds(offs[j], tm), :] * cw_ref[j:j + 1, :]
    else:
        cols = []
        for lo in range(0, D_CONV, LANES):
            pieces = []
            for t0 in range(0, tm, CONV_ROWS):
                zc = None
                for rsd in range(8):
                    n = CONV_ROWS + (8 if rsd else 0)
                    part = None
                    for j in [j for j in range(CONV_W) if offs[j] % 8 == rsd]:
                        term = (full_scr[pl.ds(t0 + offs[j] - rsd, n), lo:lo + LANES]
                                * cw_ref[j:j + 1, lo:lo + LANES])
                        part = term if part is None else part + term
                    if rsd:
                        part = pltpu.roll(part, n - rsd, axis=0)[:CONV_ROWS, :]
                    zc = part if zc is None else zc + part
                pieces.append(zc)
            cols.append(jnp.concatenate(pieces, axis=0))
        z = jnp.concatenate(cols, axis=1)
    if stride == 1:
        full_scr[:hist, :] = u[tm - hist:, :]
    z = z + cb_ref[...]
    zm = jnp.mean(z, axis=-1, keepdims=True)
    zc = z - zm
    zv = jnp.mean(zc * zc, axis=-1, keepdims=True)
    zn = zc * lax.rsqrt(zv + LN_EPS) * clg_ref[...] + clb_ref[...]
    zn = zn * _sigmoid(zn)

    mixed = (jnp.dot(y_rwkv.astype(BF16), wout_ref[:D_RWKV, :].astype(BF16), preferred_element_type=F32)
             + jnp.dot(zn.astype(BF16), wout_ref[D_RWKV:, :].astype(BF16), preferred_element_type=F32))
    o_ref[...] = x_ref[...] + _mod_rows(gt_ref, tm, stride) * _rms(mixed, gpost_ref[...])


def _post_call(y, bonus, gate, u, hist, x1, ada, w, *, nb, tiles, tm, stride, mod_rows, head_major, name):
    if head_major:
        y_spec = pl.BlockSpec((None, N_PAIRS, tm, PAIR), lambda b, j: (b, 0, j, 0))
        hist_spec = pl.BlockSpec((None, HALO, D_CONV), lambda b, j: (b, 0, 0))
        hist_rows = HALO
    else:
        y_spec = _tok_spec(tm, D_RWKV, tiles)
        hist_rows = hist.shape[0]
        hist_spec = pl.BlockSpec((hist_rows, D_CONV), lambda b, j: (0, 0))
    tok512 = _tok_spec(tm, D_RWKV, tiles)
    row512 = _const_spec((1, D_RWKV))
    return pl.pallas_call(
        functools.partial(_post_body, tm=tm, stride=stride, head_major=head_major),
        grid=(nb, tiles),
        in_specs=[y_spec, tok512, tok512, tok512, hist_spec, _tok_spec(tm, D_MODEL, tiles),
                  _mod_spec(5, *mod_rows),
                  row512, row512, _const_spec((CONV_W, D_CONV)), row512, row512, row512,
                  _const_spec((D_MODEL, D_MODEL)), _const_spec((1, D_MODEL)),
                  _const_spec((D_RWKV, D_RWKV))],
        out_specs=_tok_spec(tm, D_MODEL, tiles),
        out_shape=jax.ShapeDtypeStruct(x1.shape, F32),
        scratch_shapes=[pltpu.VMEM((hist_rows + tm, D_CONV), F32)],
        compiler_params=_params(),
        name=name,
    )(y, bonus, gate, u, hist, x1, ada, w["ln_x_g"], w["ln_x_b"], w["conv_w"], w["conv_b"],
      w["conv_ln_g"], w["conv_ln_b"], w["w_out"], w["g_post1"], w["ebd"])


def _layer(x, ada, w, shift0, hist, s0, *, nb, seq, tm, tm_post, stride, mod_rows, head_major, tag):
    tiles = (seq // tm) if head_major else 1
    tiles_post = (seq // tm_post) if head_major else 1
    x1 = _ffn_call(x, ada, 0, w["g_pre0"], w["g_post0"], w["wup1"], w["wdn1"], nb=nb, tiles=tiles, tm=tm,
                   stride=stride, mod_rows=mod_rows, res_w=0.5, name=tag + "_ffn1")
    r, k, v, a, b, ld, gate, u, bonus, nshift = _pre_call(
        x1, ada, w["g_pre1"], w, shift0, nb=nb, tiles=tiles, tm=tm, stride=stride, mod_rows=mod_rows,
        head_major=head_major, name=tag + "_mixpre")
    if head_major:
        ops = [z.reshape(nb * N_PAIRS, seq, PAIR) for z in (r, k, v, a, b, ld)]
        y, s_new = _wkv_call(ops, s0, pblk=nb * N_PAIRS, chunk=32, span=WKV_SPAN, name=tag + "_wkv")
        y = y.reshape(nb, N_PAIRS, seq, PAIR)
    else:
        y, s_new = _wkvseq_call((r, k, v, a, b, ld), s0, steps=x.shape[0] // stride, nb=stride, name=tag + "_wkv")
    x2 = _post_call(y, bonus, gate, u, hist, x1, ada, w, nb=nb, tiles=tiles_post, tm=tm_post, stride=stride,
                    mod_rows=mod_rows, head_major=head_major, name=tag + "_mixpost")
    out = _ffn_call(x2, ada, 2, w["g_pre2"], w["g_post2"], w["wup2"], w["wdn2"], nb=nb, tiles=tiles, tm=tm,
                    stride=stride, mod_rows=mod_rows, res_w=0.5, name=tag + "_ffn2")
    return out, s_new, nshift, u


def kernel(x_prompt, x_sample, c_prompt, c_sample, state_wkv, state_shift, state_conv, w_ada, b_ada, g_pre, g_post, w_ffn1_up, w_ffn1_down, w_in, mu_shift, w0, w_decay_up, a0, w_iclr_up, w_gate_up, k_k, k_a, r_k, ln_x_g, ln_x_b, conv_w, conv_b, conv_ln_g, conv_ln_b, w_out, w_ffn2_up, w_ffn2_down):
    bp, seq, _ = x_prompt.shape
    bs, dec, _ = x_sample.shape
    row = lambda z: z.reshape(1, -1).astype(F32)
    zeros64 = jnp.zeros((ICLR_RANK, D_RWKV), BF16)
    head_of_lane = jnp.arange(D_RWKV) // HEAD
    w = dict(
        wup1=w_ffn1_up, wdn1=w_ffn1_down, wup2=w_ffn2_up, wdn2=w_ffn2_down,
        w_in=w_in, w_out=w_out,
        wd=jnp.concatenate([w_decay_up.astype(BF16), zeros64], axis=0),
        wi=jnp.concatenate([zeros64, w_iclr_up.astype(BF16)], axis=0),
        wg=w_gate_up.astype(BF16),
        mu=row(mu_shift), w0=row(w0), a0=row(a0), k_k=row(k_k), k_a=row(k_a), r_k=row(r_k),
        ln_x_g=row(ln_x_g), ln_x_b=row(ln_x_b), conv_w=conv_w, conv_b=row(conv_b),
        conv_ln_g=row(conv_ln_g), conv_ln_b=row(conv_ln_b),
        g_pre0=g_pre[0:1], g_pre1=g_pre[1:2], g_pre2=g_pre[2:3],
        g_post0=g_post[0:1], g_post1=g_post[1:2], g_post2=g_post[2:3],
        ebd=(head_of_lane[:, None] == head_of_lane[None, :]).astype(BF16),
    )

    assert bs % bp == 0
    ada = _ada_call(jnp.concatenate([c_sample, c_prompt], axis=0), w_ada, b_ada)

    yp, wkv_p, shift_p, u_p = _layer(
        x_prompt.reshape(bp * seq, D_MODEL), ada, w,
        jnp.zeros((bp, 1, N_SHIFT), F32), jnp.zeros((bp, HALO, D_CONV), F32),
        jnp.zeros((bp * N_PAIRS, 2, HEAD, HEAD), F32),
        nb=bp, seq=seq, tm=512, tm_post=1024, stride=1, mod_rows=(bp, bs), head_major=True, tag="p")
    y_p = yp.reshape(bp, seq, D_MODEL)
    wkv_p = wkv_p.reshape(bp, N_HEADS, HEAD, HEAD)
    conv_p = u_p.reshape(bp, seq, D_CONV)[:, seq - (CONV_W - 1):, :]

    xs_tok = x_sample.transpose(1, 0, 2).reshape(dec * bs, D_MODEL)
    hist_s = state_conv.transpose(1, 0, 2).reshape((CONV_W - 1) * bs, D_CONV)
    ys, wkv_s, shift_s, u_s = _layer(
        xs_tok, ada, w, state_shift.reshape(bs, N_SHIFT), hist_s,
        state_wkv.transpose(1, 2, 3, 0),
        nb=1, seq=dec * bs, tm=dec * bs, tm_post=dec * bs, stride=bs, mod_rows=(bs, 0), head_major=False, tag="s")
    y_s = ys.reshape(dec, bs, D_MODEL).transpose(1, 0, 2)
    wkv_s = wkv_s.transpose(3, 0, 1, 2)
    shift_s = shift_s.reshape(bs, 1, N_SHIFT)
    conv_s = jnp.concatenate([state_conv, u_s.reshape(dec, bs, D_CONV).transpose(1, 0, 2)],
                             axis=1)[:, dec:, :]
    return (y_p, y_s, wkv_p, shift_p, conv_p, wkv_s, shift_s, conv_s)
```

```python
import functools

import jax
import jax.numpy as jnp
from jax import lax
from jax.experimental import pallas as pl
from jax.experimental.pallas import tpu as pltpu

F32 = jnp.float32
BF16 = jnp.bfloat16

D_MODEL = 1024
D_RWKV = 512
HEAD = 64
LANES = 128
PAIR = 2 * HEAD
N_PAIRS = D_RWKV // PAIR
N_HEADS = D_RWKV // HEAD
D_CONV = D_MODEL - D_RWKV
CONV_W = 31
DECAY_RANK = 64
ICLR_RANK = 64
GATE_RANK = 128
D_FF = 2816
N_SHIFT = 3 * D_RWKV + DECAY_RANK + ICLR_RANK + GATE_RANK
D_IN = N_SHIFT + 2 * D_CONV
N_SUB = 3
RMS_EPS = 1e-6
LN_EPS = 1e-5
GN_EPS = 64e-5

FF_CHUNK = 256
WKV_ROWS = 256
PRE_ROWS = 256
CONV_ROWS = 128
WKV_WAYS = 8
WKV_SPAN = 2
HALO = 32
VMEM_LIMIT = 56 * 1024 * 1024

NN = (((1,), (0,)), ((), ()))
NT = (((1,), (1,)), ((), ()))
B_NT = (((2,), (2,)), ((0,), (0,)))
B_TN = (((1,), (1,)), ((0,), (0,)))


def _splits(x, n):
    if x.dtype == BF16:
        return [x]
    parts, rem = [], x
    for i in range(n):
        p = rem.astype(BF16)
        parts.append(p)
        if i + 1 < n:
            rem = rem - p.astype(F32)
    return parts


def _mm(a, b, dn, na=2, nb=2):
    ap, bp = _splits(a, na), _splits(b, nb)
    depth = max(len(ap), len(bp))
    out = None
    for i, x in enumerate(ap):
        for j, y in enumerate(bp):
            if i + j >= depth:
                continue
            t = lax.dot_general(x, y, dn, preferred_element_type=F32)
            out = t if out is None else out + t
    return out


def _zero_of(x):
    bits = lax.bitcast_convert_type(x, jnp.uint32)
    return lax.bitcast_convert_type((bits >> 16) >> 16, jnp.int32)


def _tie(x, zero):
    reps = (x.shape[0] // zero.shape[0], x.shape[1] // zero.shape[1])
    return jnp.where(jnp.tile(zero, reps) == 0, x, 0.0)


def _rms(x, g):
    return x * lax.rsqrt(jnp.mean(x * x, axis=-1, keepdims=True) + RMS_EPS) * g


def _sigmoid(x):
    return jax.nn.sigmoid(x)


def _head_sum(x, ebd):
    return _mm(x, ebd, NN, na=2, nb=1)


def _ada_body(c_ref, w_ref, b_ref, o_ref):
    c = c_ref[...]
    s = (c * _sigmoid(c)).astype(BF16)
    o_ref[...] = jnp.dot(s, w_ref[...].astype(BF16), preferred_element_type=F32) + b_ref[...]


def _ada_call(c_all, w_ada, b_ada):
    n = c_all.shape[0]
    width = w_ada.shape[1]
    return pl.pallas_call(
        _ada_body,
        grid=(width // D_MODEL,),
        in_specs=[pl.BlockSpec((n, D_MODEL), lambda i: (0, 0)),
                  pl.BlockSpec((D_MODEL, D_MODEL), lambda i: (0, i)),
                  pl.BlockSpec((1, D_MODEL), lambda i: (0, i))],
        out_specs=pl.BlockSpec((None, n, D_MODEL), lambda i: (i, 0, 0)),
        out_shape=jax.ShapeDtypeStruct((width // D_MODEL, n, D_MODEL), F32),
        compiler_params=pltpu.CompilerParams(dimension_semantics=("arbitrary",),
                                             vmem_limit_bytes=VMEM_LIMIT),
        name="ada",
    )(c_all, w_ada, b_ada.reshape(1, width))


def _const_spec(shape):
    nd = len(shape)
    return pl.BlockSpec(shape, lambda b, j: (0,) * nd, pipeline_mode=pl.Buffered(1))


def _tok_spec(tm, width, tiles):
    return pl.BlockSpec((tm, width), lambda b, j: (b * tiles + j, 0))


def _mod_spec(slab, rows, row0):
    return pl.BlockSpec((None, rows, D_MODEL), lambda b, j: (slab, row0 // rows, 0))


def _mod_rows(ref, tm, stride, seq_idx=None):
    if stride == 1:
        return ref[pl.ds(pl.program_id(0) if seq_idx is None else seq_idx, 1), :]
    return jnp.concatenate([ref[...]] * (tm // stride), axis=0)


def _params():
    return pltpu.CompilerParams(dimension_semantics=("arbitrary", "arbitrary"),
                                vmem_limit_bytes=VMEM_LIMIT)


def _ffn_tile(x, sh, sc, gt, gpre_ref, gpost_ref, wup_ref, wdn_ref, act_scr, res_w, between=lambda: None):
    h = _rms(x, gpre_ref[...]) * (1.0 + sc) + sh
    hb = h.astype(BF16)
    for j in range(D_FF // FF_CHUNK):
        lo = j * FF_CHUNK
        g = jnp.dot(hb, wup_ref[:, lo:lo + FF_CHUNK].astype(BF16), preferred_element_type=F32)
        u = jnp.dot(hb, wup_ref[:, D_FF + lo:D_FF + lo + FF_CHUNK].astype(BF16), preferred_element_type=F32)
        zero = between()
        if zero is not None:
            g = _tie(g, zero)
        act_scr[:, lo:lo + FF_CHUNK] = (g * _sigmoid(g) * u).astype(BF16)
    out = None
    for j in range(D_FF // FF_CHUNK):
        lo = j * FF_CHUNK
        part = jnp.dot(act_scr[:, lo:lo + FF_CHUNK], wdn_ref[lo:lo + FF_CHUNK, :].astype(BF16), preferred_element_type=F32)
        zero = between()
        if zero is not None:
            part = _tie(part, zero)
        out = part if out is None else out + part
    return x + res_w * gt * _rms(out, gpost_ref[...])


def _ffn_body(x_ref, sh_ref, sc_ref, gt_ref, gpre_ref, gpost_ref, wup_ref, wdn_ref, o_ref, act_scr, *, res_w, stride):
    x = x_ref[...]
    tm = x.shape[0]
    o_ref[...] = _ffn_tile(x, _mod_rows(sh_ref, tm, stride), _mod_rows(sc_ref, tm, stride), _mod_rows(gt_ref, tm, stride),
                           gpre_ref, gpost_ref, wup_ref, wdn_ref, act_scr, res_w)


def _ffn_call(x, ada, sub, gpre, gpost, wup, wdn, *, nb, tiles, tm, stride, mod_rows, res_w, name):
    return pl.pallas_call(
        functools.partial(_ffn_body, res_w=res_w, stride=stride),
        grid=(nb, tiles),
        in_specs=[_tok_spec(tm, D_MODEL, tiles),
                  _mod_spec(3 * sub, *mod_rows), _mod_spec(3 * sub + 1, *mod_rows), _mod_spec(3 * sub + 2, *mod_rows),
                  _const_spec((1, D_MODEL)), _const_spec((1, D_MODEL)),
                  _const_spec((D_MODEL, 2 * D_FF)), _const_spec((D_FF, D_MODEL))],
        out_specs=_tok_spec(tm, D_MODEL, tiles),
        out_shape=jax.ShapeDtypeStruct(x.shape, F32),
        scratch_shapes=[pltpu.VMEM((tm, D_FF), BF16)],
        compiler_params=_params(),
        name=name,
    )(x, ada, ada, ada, gpre, gpost, wup, wdn)


def _pre_body(x_ref, sh_ref, sc_ref, gpre_ref, win_ref, mu_ref, w0_ref, wd_ref, a0_ref, wi_ref, wg_ref,
              kk_ref, ka_ref, rk_ref, ebd_ref, shift0_ref,
              r_ref, k_ref, v_ref, a_ref, b_ref, ld_ref, gate_ref, u_ref, bonus_ref, nshift_ref,
              carry_scr, *, tm, stride, head_major):
    x = x_ref[...]
    h = _rms(x, gpre_ref[...]) * (1.0 + _mod_rows(sc_ref, tm, stride)) + _mod_rows(sh_ref, tm, stride)
    hb = h.astype(BF16)
    wb = win_ref[...].astype(BF16)
    rc = PRE_ROWS
    projs = [jnp.dot(hb[lo:lo + rc, :], wb, preferred_element_type=F32) for lo in range(0, tm, rc)]
    if stride == 1:
        @pl.when(pl.program_id(1) == 0)
        def _():
            carry_scr[...] = shift0_ref[...]
        first = carry_scr[...]
        row = lax.broadcasted_iota(jnp.int32, (rc, 1), 0)
    else:
        first = shift0_ref[...]
    ebd = ebd_ref[...]
    o1, o2, o3 = D_RWKV, 2 * D_RWKV, 3 * D_RWKV
    for ci, proj in enumerate(projs):
        rows = slice(ci * rc, (ci + 1) * rc)
        p_sh = proj[:, :N_SHIFT]
        p_cv = proj[:, N_SHIFT:]
        if stride == 1:
            prev = jnp.where(row == 0, first, pltpu.roll(p_sh, 1, axis=0))
        else:
            prev = jnp.concatenate([first, p_sh[:rc - stride, :]], axis=0)
        first = p_sh[rc - stride:rc, :]
        xs = p_sh + (prev - p_sh) * mu_ref[...]

        r, k, v = xs[:, :o1], xs[:, o1:o2], xs[:, o2:o3]
        dwda = xs[:, o3:o3 + DECAY_RANK + ICLR_RANK]
        dg = xs[:, o3 + DECAY_RANK + ICLR_RANK:]

        zdec = -(w0_ref[...] + jnp.dot(jnp.tanh(dwda).astype(BF16), wd_ref[...], preferred_element_type=F32))
        softplus = jnp.maximum(zdec, 0.0) + jnp.log1p(jnp.exp(-jnp.abs(zdec)))
        ld = -jnp.exp(-softplus - 0.5)
        iclr = _sigmoid(a0_ref[...] + jnp.dot(dwda.astype(BF16), wi_ref[...], preferred_element_type=F32))
        gate = jnp.dot(_sigmoid(dg).astype(BF16), wg_ref[...], preferred_element_type=F32)

        kk = k * kk_ref[...]
        kk = kk / jnp.maximum(jnp.sqrt(_head_sum(kk * kk, ebd)), 1e-12)
        kmod = k * (1.0 + (iclr - 1.0) * ka_ref[...])
        bonus = _head_sum(r * kmod * rk_ref[...], ebd) * v

        gate_ref[rows, :] = gate
        bonus_ref[rows, :] = bonus
        u_ref[rows, :] = p_cv[:, :D_CONV] * _sigmoid(p_cv[:, D_CONV:])
        outs = ((r_ref, r), (k_ref, kmod), (v_ref, v), (a_ref, -kk), (b_ref, kk * iclr), (ld_ref, ld))
        for ref, val in outs:
            if head_major:
                for pr in range(N_PAIRS):
                    ref[pr, rows, :] = val[:, pr * PAIR:(pr + 1) * PAIR]
            else:
                ref[rows, :] = val
    if stride == 1:
        carry_scr[...] = first
    nshift_ref[...] = first


def _pre_call(x1, ada, gpre, w, shift0, *, nb, tiles, tm, stride, mod_rows, head_major, name):
    n = x1.shape[0]
    if head_major:
        seq = tiles * tm
        hm_spec = pl.BlockSpec((None, N_PAIRS, tm, PAIR), lambda b, j: (b, 0, j, 0))
        hm_shape = jax.ShapeDtypeStruct((nb, N_PAIRS, seq, PAIR), F32)
    else:
        hm_spec = _tok_spec(tm, D_RWKV, tiles)
        hm_shape = jax.ShapeDtypeStruct((n, D_RWKV), F32)
    ld_spec = _tok_spec(tm, D_RWKV, tiles)
    ld_shape = jax.ShapeDtypeStruct((n, D_RWKV), F32)
    if stride == 1:
        shift_spec = pl.BlockSpec((None, 1, N_SHIFT), lambda b, j: (b, 0, 0))
        nshift_shape = jax.ShapeDtypeStruct((nb, 1, N_SHIFT), F32)
    else:
        shift_spec = pl.BlockSpec((stride, N_SHIFT), lambda b, j: (0, 0))
        nshift_shape = jax.ShapeDtypeStruct((stride, N_SHIFT), F32)
    return pl.pallas_call(
        functools.partial(_pre_body, tm=tm, stride=stride, head_major=head_major),
        grid=(nb, tiles),
        in_specs=[_tok_spec(tm, D_MODEL, tiles), _mod_spec(3, *mod_rows), _mod_spec(4, *mod_rows),
                  _const_spec((1, D_MODEL)), _const_spec((D_MODEL, D_IN)), _const_spec((1, N_SHIFT)),
                  _const_spec((1, D_RWKV)), _const_spec((DECAY_RANK + ICLR_RANK, D_RWKV)),
                  _const_spec((1, D_RWKV)), _const_spec((DECAY_RANK + ICLR_RANK, D_RWKV)),
                  _const_spec((GATE_RANK, D_RWKV)),
                  _const_spec((1, D_RWKV)), _const_spec((1, D_RWKV)), _const_spec((1, D_RWKV)),
                  _const_spec((D_RWKV, D_RWKV)), shift_spec],
        out_specs=[hm_spec] * 6 + [ld_spec] * 3 + [shift_spec],
        out_shape=[hm_shape] * 6 + [ld_shape] * 3 + [nshift_shape],
        scratch_shapes=[pltpu.VMEM((1, N_SHIFT), F32)],
        compiler_params=_params(),
        name=name,
    )(x1, ada, ada, gpre, w["w_in"], w["mu"], w["w0"], w["wd"], w["a0"], w["wi"], w["wg"],
      w["k_k"], w["k_a"], w["r_k"], w["ebd"], shift0)


def _wkv_body(r_ref, k_ref, v_ref, a_ref, b_ref, ld_ref, s0_ref, y_ref, st_ref, s_scr, *, chunk, n_sub, ways, span):
    c = pl.program_id(1)
    rows = WKV_ROWS
    pp = rows // (2 * chunk)
    shift = chunk.bit_length() - 1

    @pl.when(c == 0)
    def _():
        zero = jnp.zeros(s0_ref.shape[:1] + (HEAD, HEAD), F32)
        s_scr[:, :HEAD, :] = jnp.concatenate([s0_ref[:, 0], zero], axis=-1)
        s_scr[:, HEAD:, :] = jnp.concatenate([zero, s0_ref[:, 1]], axis=-1)

    ri = lax.broadcasted_iota(jnp.int32, (rows, rows), 0)
    ci = lax.broadcasted_iota(jnp.int32, (rows, rows), 1)
    same = jnp.right_shift(ri, shift) == jnp.right_shift(ci, shift)
    strict = same & (ci < ri)
    incl = same & (ci <= ri)
    eye = (ri == ci).astype(F32)
    first = lax.broadcasted_iota(jnp.int32, (1, 1, PAIR), 2) < HEAD
    tpos = lax.broadcasted_iota(jnp.int32, (pp * chunk, 1), 0) & (chunk - 1)

    def cumsum(x3):
        x = x3.reshape(pp * chunk, PAIR)
        for s in range(shift):
            d = 1 << s
            x = x + jnp.where(tpos >= d, pltpu.roll(x, d, axis=0), 0.0)
        return x.reshape(pp, chunk, PAIR)

    stack = lambda x3: jnp.concatenate([jnp.where(first, x3, 0.0), jnp.where(first, 0.0, x3)], axis=1)
    flat = lambda x3: x3.reshape(rows, PAIR)
    three = lambda x: x.reshape(pp, 2 * chunk, PAIR)
    each = lambda f, *xs: [f(*t) for t in zip(*xs)]

    def sub(it, carry, t0):
        sls = [pl.ds(pl.multiple_of((it * ways + q) * pp, pp), pp) for q in range(ways)]
        tw = slice(t0, t0 + chunk)
        r, k, v, a, b, ld = ([ref[sl, tw, :] for sl in sls] for ref in (r_ref, k_ref, v_ref, a_ref, b_ref, ld_ref))
        cs = each(cumsum, ld)
        e_inv = each(lambda x: jnp.exp(-x), cs)
        at = each(lambda x, c_, l_: stack(x * jnp.exp(c_ - l_)), a, cs, ld)
        rt = each(lambda x, c_: stack(x * jnp.exp(c_)), r, cs)
        bt = each(lambda x, e: stack(x * e), b, e_inv)
        kt = each(lambda x, e: stack(x * e), k, e_inv)
        vs = each(stack, v)
        gram = each(lambda p, q, m, n: _mm(jnp.concatenate([flat(p), flat(q)], axis=0),
                                           jnp.concatenate([flat(m), flat(n)], axis=0), NT, 1, 1), at, rt, bt, kt)
        t_ab = each(lambda g: jnp.where(strict, g[:rows, :rows], 0.0).astype(BF16), gram)
        t_ak = each(lambda g: jnp.where(strict, g[:rows, rows:], 0.0), gram)
        m_rb = each(lambda g: jnp.where(incl, g[rows:, :rows], 0.0), gram)
        m_rk = each(lambda g: jnp.where(incl, g[rows:, rows:], 0.0), gram)
        inv = each(lambda t: eye + t.astype(F32), t_ab)
        pw = t_ab
        for _ in range(shift - 1):
            pw = each(lambda p: _mm(p, p, NN, 1, 1), pw)
            inv = each(lambda x, p: x + _mm(p, x, NN, 1, 1), inv, pw)
        s0 = [s_scr[sl] for sl in sls]
        uy0 = each(lambda p, q, s: _mm(jnp.concatenate([p, q], axis=1), s, B_NT, 1, 1), at, rt, s0)
        z = each(lambda o, t, x: flat(o[:, :2 * chunk, :]) + _mm(t, flat(x), NN, 1, 1), uy0, t_ak, vs)
        u = each(lambda x, z_: _mm(x, z_, NN, 1, 1), inv, z)
        res = each(lambda z_, u_, t: z_ - u_ + _mm(t, u_, NN, 1, 2), z, u, t_ab)
        u = each(lambda u_, x, r_: u_ + _mm(x, r_, NN, 1, 1), u, inv, res)
        y = each(lambda o, m, u_, n, x: three(flat(o[:, 2 * chunk:, :]) + _mm(m, u_, NN, 1, 1)
                                              + _mm(n, flat(x), NN, 1, 1)), uy0, m_rb, u, m_rk, vs)
        for sl, y_ in zip(sls, y):
            y_ref[sl, tw, :] = y_[:, :chunk, :] + y_[:, chunk:, :]
        c_last = each(lambda x: x[:, chunk - 1:chunk, :], cs)
        tail = each(lambda cl, x: jnp.exp(cl - x), c_last, cs)
        upd = each(lambda u_, x, b_, k_, t: _mm(jnp.concatenate([three(u_), x], axis=1),
                                                jnp.concatenate([stack(b_ * t), stack(k_ * t)], axis=1), B_TN, 1, 1),
                   u, vs, b, k, tail)
        for sl, s, cl, d in zip(sls, s0, c_last, upd):
            s_scr[sl] = s * jnp.exp(cl) + d
        return carry

    for ci in range(span):
        lax.fori_loop(0, n_sub // ways, functools.partial(sub, t0=ci * chunk), 0)

    @pl.when(c == pl.num_programs(1) - 1)
    def _():
        st_ref[:, 0] = s_scr[:, :HEAD, :HEAD]
        st_ref[:, 1] = s_scr[:, HEAD:, HEAD:]


def _wkv_call(ops, s0, *, pblk, chunk, span, name):
    n_pairs, t_total, _ = ops[0].shape
    pp = WKV_ROWS // (2 * chunk)
    op_spec = pl.BlockSpec((pblk, span * chunk, PAIR), lambda g, c: (g, c, 0))
    st_spec = pl.BlockSpec((pblk, 2, HEAD, HEAD), lambda g, c: (g, 0, 0, 0))
    return pl.pallas_call(
        functools.partial(_wkv_body, chunk=chunk, n_sub=pblk // pp, ways=WKV_WAYS, span=span),
        grid=(n_pairs // pblk, t_total // (span * chunk)),
        in_specs=[op_spec] * 6 + [st_spec],
        out_specs=[op_spec, st_spec],
        out_shape=[jax.ShapeDtypeStruct((n_pairs, t_total, PAIR), F32),
                   jax.ShapeDtypeStruct((n_pairs, 2, HEAD, HEAD), F32)],
        scratch_shapes=[pltpu.VMEM((pblk, PAIR, PAIR), F32)],
        compiler_params=_params(),
        name=name,
    )(*ops, s0)


def _wkvseq_body(r_ref, k_ref, v_ref, a_ref, b_ref, ld_ref, s0_ref, y_ref, st_ref, t_scr, yt_scr, *, steps, nb):
    for qi, ref in enumerate((r_ref, k_ref, v_ref, a_ref, b_ref, ld_ref)):
        for t in range(steps):
            xt = ref[t * nb:(t + 1) * nb, :].T
            t_scr[qi, t] = jnp.exp(xt) if qi == 5 else xt
    for s in range(2):
        lo = s * HEAD

        def body(vi, carry):
            st = s0_ref[s, vi]
            for t in range(steps):
                col = lambda qi: t_scr[qi, t, lo:lo + HEAD, :]
                sa = jnp.sum(st * col(3), axis=0, keepdims=True)
                vv = t_scr[2, t, pl.ds(lo + vi, 1), :]
                st = st * col(5) + sa * col(4) + vv * col(1)
                yt_scr[t, pl.ds(lo + vi, 1), :] = jnp.sum(st * col(0), axis=0, keepdims=True)
            st_ref[s, vi] = st
            return carry

        lax.fori_loop(0, HEAD, body, 0, unroll=4)
    for t in range(steps):
        y_ref[t * nb:(t + 1) * nb, :] = yt_scr[t].T


def _wkvseq_call(ops, s0, *, steps, nb, name):
    n = steps * nb
    op_spec = pl.BlockSpec((n, PAIR), lambda p: (0, p))
    st_spec = pl.BlockSpec((2, HEAD, HEAD, nb), lambda p: (p, 0, 0, 0))
    return pl.pallas_call(
        functools.partial(_wkvseq_body, steps=steps, nb=nb),
        grid=(N_PAIRS,),
        in_specs=[op_spec] * 6 + [st_spec],
        out_specs=[op_spec, st_spec],
        out_shape=[jax.ShapeDtypeStruct((n, D_RWKV), F32),
                   jax.ShapeDtypeStruct((N_HEADS, HEAD, HEAD, nb), F32)],
        scratch_shapes=[pltpu.VMEM((6, steps, PAIR, nb), F32), pltpu.VMEM((steps, PAIR, nb), F32)],
        compiler_params=pltpu.CompilerParams(dimension_semantics=("arbitrary",),
                                             vmem_limit_bytes=VMEM_LIMIT),
        name=name,
    )(*ops, s0)


def _gn_gate(y_ref, bonus_ref, gate_ref, lng_ref, lnb_ref, ebd_ref, head_major):
    if head_major:
        y = jnp.concatenate([y_ref[pr] for pr in range(N_PAIRS)], axis=-1)
    else:
        y = y_ref[...]
    ebd = ebd_ref[...]
    mu = _head_sum(y, ebd) * (1.0 / HEAD)
    d = y - mu
    var = _head_sum(d * d, ebd) * (1.0 / HEAD)
    yn = d * lax.rsqrt(var + GN_EPS) * lng_ref[...] + lnb_ref[...]
    return (yn + bonus_ref[...]) * gate_ref[...]


def _conv_pieces(u_ref, full_scr, z_scr, cw_ref, tm, stride):
    hist = full_scr.shape[0] - tm
    full_scr[hist:, :] = u_ref[...]
    base = hist - (CONV_W - 1) * stride
    offs = [base + j * stride for j in range(CONV_W)]
    pieces = []
    if stride % 8 == 0:
        def whole():
            z = jnp.zeros((tm, D_CONV), F32)
            for j in range(CONV_W):
                z = z + full_scr[pl.ds(offs[j], tm), :] * cw_ref[j:j + 1, :]
            z_scr[...] = z
            return None
        pieces.append(whole)
    else:
        for lo in range(0, D_CONV, LANES):
            for t0 in range(0, tm, CONV_ROWS):
                def piece(lo=lo, t0=t0):
                    zc = None
                    for rsd in range(8):
                        n = CONV_ROWS + (8 if rsd else 0)
                        part = None
                        for j in [j for j in range(CONV_W) if offs[j] % 8 == rsd]:
                            term = (full_scr[pl.ds(t0 + offs[j] - rsd, n), lo:lo + LANES]
                                    * cw_ref[j:j + 1, lo:lo + LANES])
                            part = term if part is None else part + term
                        if rsd:
                            part = pltpu.roll(part, n - rsd, axis=0)[:CONV_ROWS, :]
                        zc = part if zc is None else zc + part
                    z_scr[t0:t0 + CONV_ROWS, lo:lo + LANES] = zc
                    return _zero_of(zc)
                pieces.append(piece)
    return pieces


def _conv_finish(u_ref, full_scr, z_scr, cb_ref, clg_ref, clb_ref, tm, stride):
    hist = full_scr.shape[0] - tm
    if stride == 1:
        full_scr[:hist, :] = u_ref[tm - hist:, :]
    z = z_scr[...] + cb_ref[...]
    zm = jnp.mean(z, axis=-1, keepdims=True)
    zc = z - zm
    zv = jnp.mean(zc * zc, axis=-1, keepdims=True)
    zn = zc * lax.rsqrt(zv + LN_EPS) * clg_ref[...] + clb_ref[...]
    return zn * _sigmoid(zn)


def _mix_out(y_rwkv, zn, wout_ref, x, gt, gpost_ref):
    mixed = (jnp.dot(y_rwkv.astype(BF16), wout_ref[:D_RWKV, :].astype(BF16), preferred_element_type=F32)
             + jnp.dot(zn.astype(BF16), wout_ref[D_RWKV:, :].astype(BF16), preferred_element_type=F32))
    return x + gt * _rms(mixed, gpost_ref[...])


def _post_body(y_ref, bonus_ref, gate_ref, u_ref, hist_ref, x_ref, gt_ref,
               lng_ref, lnb_ref, cw_ref, cb_ref, clg_ref, clb_ref, wout_ref, gpost_ref, ebd_ref,
               o_ref, full_scr, z_scr, *, tm, stride, head_major):
    hist = full_scr.shape[0] - tm
    if stride == 1:
        @pl.when(pl.program_id(1) == 0)
        def _():
            full_scr[:hist, :] = hist_ref[...]
    else:
        full_scr[:hist, :] = hist_ref[...]
    y_rwkv = _gn_gate(y_ref, bonus_ref, gate_ref, lng_ref, lnb_ref, ebd_ref, head_major)
    for piece in _conv_pieces(u_ref, full_scr, z_scr, cw_ref, tm, stride):
        piece()
    zn = _conv_finish(u_ref, full_scr, z_scr, cb_ref, clg_ref, clb_ref, tm, stride)
    o_ref[...] = _mix_out(y_rwkv, zn, wout_ref, x_ref[...], _mod_rows(gt_ref, tm, stride), gpost_ref)


def _postffn_body(y_ref, bonus_ref, gate_ref, u_ref, hist_ref, x_ref, gt_ref,
                  lng_ref, lnb_ref, cw_ref, cb_ref, clg_ref, clb_ref, wout_ref, gpost_ref, ebd_ref,
                  sh2_ref, sc2_ref, gt2_ref, gpre2_ref, gpost2_ref, wup_ref, wdn_ref,
                  o_ref, full_scr, z_scr, x2_scr, act_scr, *, tm, tiles, n_tiles):
    s = pl.program_id(0)
    t_mix = jnp.minimum(s, n_tiles - 1)
    b_mix = t_mix // tiles
    b_ffn = jnp.maximum(s - 1, 0) // tiles
    hist = full_scr.shape[0] - tm

    @pl.when(s == 0)
    def _():
        x2_scr[...] = jnp.zeros_like(x2_scr)

    @pl.when(t_mix % tiles == 0)
    def _():
        full_scr[:hist, :] = hist_ref[...]

    x2_prev = x2_scr[...]
    y_rwkv = _gn_gate(y_ref, bonus_ref, gate_ref, lng_ref, lnb_ref, ebd_ref, True)
    pieces = _conv_pieces(u_ref, full_scr, z_scr, cw_ref, tm, 1)

    def between():
        return pieces.pop(0)() if pieces else None

    o_ref[...] = _ffn_tile(x2_prev, _mod_rows(sh2_ref, tm, 1, b_ffn), _mod_rows(sc2_ref, tm, 1, b_ffn),
                           _mod_rows(gt2_ref, tm, 1, b_ffn), gpre2_ref, gpost2_ref, wup_ref, wdn_ref, act_scr, 0.5,
                           between)
    while pieces:
        between()
    zn = _conv_finish(u_ref, full_scr, z_scr, cb_ref, clg_ref, clb_ref, tm, 1)
    x2_scr[...] = _mix_out(y_rwkv, zn, wout_ref, x_ref[...], _mod_rows(gt_ref, tm, 1, b_mix), gpost_ref)


def _postffn_call(y, bonus, gate, u, hist, x1, ada, w, *, nb, tiles, tm, mod_rows, name):
    n_tiles = nb * tiles
    mix = lambda s: jnp.minimum(s, n_tiles - 1)
    ffn = lambda s: jnp.maximum(s - 1, 0)
    const = lambda shape: pl.BlockSpec(shape, lambda s: (0,) * len(shape), pipeline_mode=pl.Buffered(1))
    mod = lambda slab: pl.BlockSpec((None, mod_rows[0], D_MODEL), lambda s: (slab, mod_rows[1] // mod_rows[0], 0))
    tok = lambda width: pl.BlockSpec((tm, width), lambda s: (mix(s), 0))
    y_spec = pl.BlockSpec((None, N_PAIRS, tm, PAIR), lambda s: (mix(s) // tiles, 0, mix(s) % tiles, 0))
    hist_spec = pl.BlockSpec((None, HALO, D_CONV), lambda s: (mix(s) // tiles, 0, 0))
    row512 = const((1, D_RWKV))
    return pl.pallas_call(
        functools.partial(_postffn_body, tm=tm, tiles=tiles, n_tiles=n_tiles),
        grid=(n_tiles + 1,),
        in_specs=[y_spec, tok(D_RWKV), tok(D_RWKV), tok(D_RWKV), hist_spec, tok(D_MODEL), mod(5),
                  row512, row512, const((CONV_W, D_CONV)), row512, row512, row512,
                  const((D_MODEL, D_MODEL)), const((1, D_MODEL)), const((D_RWKV, D_RWKV)),
                  mod(6), mod(7), mod(8), const((1, D_MODEL)), const((1, D_MODEL)),
                  const((D_MODEL, 2 * D_FF)), const((D_FF, D_MODEL))],
        out_specs=pl.BlockSpec((tm, D_MODEL), lambda s: (ffn(s), 0)),
        out_shape=jax.ShapeDtypeStruct(x1.shape, F32),
        scratch_shapes=[pltpu.VMEM((HALO + tm, D_CONV), F32), pltpu.VMEM((tm, D_CONV), F32),
                        pltpu.VMEM((tm, D_MODEL), F32), pltpu.VMEM((tm, D_FF), BF16)],
        compiler_params=pltpu.CompilerParams(dimension_semantics=("arbitrary",), vmem_limit_bytes=VMEM_LIMIT),
        name=name,
    )(y, bonus, gate, u, hist, x1, ada, w["ln_x_g"], w["ln_x_b"], w["conv_w"], w["conv_b"],
      w["conv_ln_g"], w["conv_ln_b"], w["w_out"], w["g_post1"], w["ebd"],
      ada, ada, ada, w["g_pre2"], w["g_post2"], w["wup2b"], w["wdn2b"])


def _post_call(y, bonus, gate, u, hist, x1, ada, w, *, nb, tiles, tm, stride, mod_rows, head_major, name):
    if head_major:
        y_spec = pl.BlockSpec((None, N_PAIRS, tm, PAIR), lambda b, j: (b, 0, j, 0))
        hist_spec = pl.BlockSpec((None, HALO, D_CONV), lambda b, j: (b, 0, 0))
        hist_rows = HALO
    else:
        y_spec = _tok_spec(tm, D_RWKV, tiles)
        hist_rows = hist.shape[0]
        hist_spec = pl.BlockSpec((hist_rows, D_CONV), lambda b, j: (0, 0))
    tok512 = _tok_spec(tm, D_RWKV, tiles)
    row512 = _const_spec((1, D_RWKV))
    return pl.pallas_call(
        functools.partial(_post_body, tm=tm, stride=stride, head_major=head_major),
        grid=(nb, tiles),
        in_specs=[y_spec, tok512, tok512, tok512, hist_spec, _tok_spec(tm, D_MODEL, tiles),
                  _mod_spec(5, *mod_rows),
                  row512, row512, _const_spec((CONV_W, D_CONV)), row512, row512, row512,
                  _const_spec((D_MODEL, D_MODEL)), _const_spec((1, D_MODEL)),
                  _const_spec((D_RWKV, D_RWKV))],
        out_specs=_tok_spec(tm, D_MODEL, tiles),
        out_shape=jax.ShapeDtypeStruct(x1.shape, F32),
        scratch_shapes=[pltpu.VMEM((hist_rows + tm, D_CONV), F32), pltpu.VMEM((tm, D_CONV), F32)],
        compiler_params=_params(),
        name=name,
    )(y, bonus, gate, u, hist, x1, ada, w["ln_x_g"], w["ln_x_b"], w["conv_w"], w["conv_b"],
      w["conv_ln_g"], w["conv_ln_b"], w["w_out"], w["g_post1"], w["ebd"])


def _layer(x, ada, w, shift0, hist, s0, *, nb, seq, tm, tm_post, stride, mod_rows, head_major, tag):
    tiles = (seq // tm) if head_major else 1
    tiles_post = (seq // tm_post) if head_major else 1
    x1 = _ffn_call(x, ada, 0, w["g_pre0"], w["g_post0"], w["wup1"], w["wdn1"], nb=nb, tiles=tiles, tm=tm,
                   stride=stride, mod_rows=mod_rows, res_w=0.5, name=tag + "_ffn1")
    r, k, v, a, b, ld, gate, u, bonus, nshift = _pre_call(
        x1, ada, w["g_pre1"], w, shift0, nb=nb, tiles=tiles, tm=tm, stride=stride, mod_rows=mod_rows,
        head_major=head_major, name=tag + "_mixpre")
    if head_major:
        ops = [z.reshape(nb * N_PAIRS, seq, PAIR) for z in (r, k, v, a, b, ld)]
        y, s_new = _wkv_call(ops, s0, pblk=nb * N_PAIRS, chunk=32, span=WKV_SPAN, name=tag + "_wkv")
        y = y.reshape(nb, N_PAIRS, seq, PAIR)
    else:
        y, s_new = _wkvseq_call((r, k, v, a, b, ld), s0, steps=x.shape[0] // stride, nb=stride, name=tag + "_wkv")
    if head_major:
        out = _postffn_call(y, bonus, gate, u, hist, x1, ada, w, nb=nb, tiles=tiles, tm=tm, mod_rows=mod_rows,
                            name=tag + "_mixpost_ffn2")
    else:
        x2 = _post_call(y, bonus, gate, u, hist, x1, ada, w, nb=nb, tiles=tiles_post, tm=tm_post, stride=stride,
                        mod_rows=mod_rows, head_major=head_major, name=tag + "_mixpost")
        out = _ffn_call(x2, ada, 2, w["g_pre2"], w["g_post2"], w["wup2"], w["wdn2"], nb=nb, tiles=tiles, tm=tm,
                        stride=stride, mod_rows=mod_rows, res_w=0.5, name=tag + "_ffn2")
    return out, s_new, nshift, u


def kernel(x_prompt, x_sample, c_prompt, c_sample, state_wkv, state_shift, state_conv, w_ada, b_ada, g_pre, g_post, w_ffn1_up, w_ffn1_down, w_in, mu_shift, w0, w_decay_up, a0, w_iclr_up, w_gate_up, k_k, k_a, r_k, ln_x_g, ln_x_b, conv_w, conv_b, conv_ln_g, conv_ln_b, w_out, w_ffn2_up, w_ffn2_down):
    bp, seq, _ = x_prompt.shape
    bs, dec, _ = x_sample.shape
    row = lambda z: z.reshape(1, -1).astype(F32)
    zeros64 = jnp.zeros((ICLR_RANK, D_RWKV), BF16)
    head_of_lane = jnp.arange(D_RWKV) // HEAD
    w = dict(
        wup1=w_ffn1_up, wdn1=w_ffn1_down, wup2=w_ffn2_up, wdn2=w_ffn2_down,
        wup2b=w_ffn2_up.astype(BF16), wdn2b=w_ffn2_down.astype(BF16),
        w_in=w_in, w_out=w_out,
        wd=jnp.concatenate([w_decay_up.astype(BF16), zeros64], axis=0),
        wi=jnp.concatenate([zeros64, w_iclr_up.astype(BF16)], axis=0),
        wg=w_gate_up.astype(BF16),
        mu=row(mu_shift), w0=row(w0), a0=row(a0), k_k=row(k_k), k_a=row(k_a), r_k=row(r_k),
        ln_x_g=row(ln_x_g), ln_x_b=row(ln_x_b), conv_w=conv_w, conv_b=row(conv_b),
        conv_ln_g=row(conv_ln_g), conv_ln_b=row(conv_ln_b),
        g_pre0=g_pre[0:1], g_pre1=g_pre[1:2], g_pre2=g_pre[2:3],
        g_post0=g_post[0:1], g_post1=g_post[1:2], g_post2=g_post[2:3],
        ebd=(head_of_lane[:, None] == head_of_lane[None, :]).astype(BF16),
    )

    assert bs % bp == 0
    ada = _ada_call(jnp.concatenate([c_sample, c_prompt], axis=0), w_ada, b_ada)

    yp, wkv_p, shift_p, u_p = _layer(
        x_prompt.reshape(bp * seq, D_MODEL), ada, w,
        jnp.zeros((bp, 1, N_SHIFT), F32), jnp.zeros((bp, HALO, D_CONV), F32),
        jnp.zeros((bp * N_PAIRS, 2, HEAD, HEAD), F32),
        nb=bp, seq=seq, tm=512, tm_post=1024, stride=1, mod_rows=(bp, bs), head_major=True, tag="p")
    y_p = yp.reshape(bp, seq, D_MODEL)
    wkv_p = wkv_p.reshape(bp, N_HEADS, HEAD, HEAD)
    conv_p = u_p.reshape(bp, seq, D_CONV)[:, seq - (CONV_W - 1):, :]

    xs_tok = x_sample.transpose(1, 0, 2).reshape(dec * bs, D_MODEL)
    hist_s = state_conv.transpose(1, 0, 2).reshape((CONV_W - 1) * bs, D_CONV)
    ys, wkv_s, shift_s, u_s = _layer(
        xs_tok, ada, w, state_shift.reshape(bs, N_SHIFT), hist_s,
        state_wkv.transpose(1, 2, 3, 0),
        nb=1, seq=dec * bs, tm=dec * bs, tm_post=dec * bs, stride=bs, mod_rows=(bs, 0), head_major=False, tag="s")
    y_s = ys.reshape(dec, bs, D_MODEL).transpose(1, 0, 2)
    wkv_s = wkv_s.transpose(3, 0, 1, 2)
    shift_s = shift_s.reshape(bs, 1, N_SHIFT)
    conv_s = jnp.concatenate([state_conv, u_s.reshape(dec, bs, D_CONV).transpose(1, 0, 2)],
                             axis=1)[:, dec:, :]
    return (y_p, y_s, wkv_p, shift_p, conv_p, wkv_s, shift_s, conv_s)
```

```python
import functools

import jax
import jax.numpy as jnp
from jax import lax
from jax.experimental import pallas as pl
from jax.experimental.pallas import tpu as pltpu

F32 = jnp.float32
BF16 = jnp.bfloat16

D_MODEL = 1024
D_RWKV = 512
HEAD = 64
LANES = 128
PAIR = 2 * HEAD
N_PAIRS = D_RWKV // PAIR
N_HEADS = D_RWKV // HEAD
D_CONV = D_MODEL - D_RWKV
CONV_W = 31
DECAY_RANK = 64
ICLR_RANK = 64
GATE_RANK = 128
D_FF = 2816
N_SHIFT = 3 * D_RWKV + DECAY_RANK + ICLR_RANK + GATE_RANK
D_IN = N_SHIFT + 2 * D_CONV
N_SUB = 3
RMS_EPS = 1e-6
LN_EPS = 1e-5
GN_EPS = 64e-5

FF_CHUNK = 256
WKV_ROWS = 256
PRE_ROWS = 256
CONV_ROWS = 128
WKV_WAYS = 8
WKV_SPAN = 2
HALO = 32
VMEM_LIMIT = 56 * 1024 * 1024

NN = (((1,), (0,)), ((), ()))
NT = (((1,), (1,)), ((), ()))
B_NT = (((2,), (2,)), ((0,), (0,)))
B_TN = (((1,), (1,)), ((0,), (0,)))


def _splits(x, n):
    if x.dtype == BF16:
        return [x]
    parts, rem = [], x
    for i in range(n):
        p = rem.astype(BF16)
        parts.append(p)
        if i + 1 < n:
            rem = rem - p.astype(F32)
    return parts


def _mm(a, b, dn, na=2, nb=2):
    ap, bp = _splits(a, na), _splits(b, nb)
    depth = max(len(ap), len(bp))
    out = None
    for i, x in enumerate(ap):
        for j, y in enumerate(bp):
            if i + j >= depth:
                continue
            t = lax.dot_general(x, y, dn, preferred_element_type=F32)
            out = t if out is None else out + t
    return out


def _zero_of(x):
    bits = lax.bitcast_convert_type(x, jnp.uint32)
    return lax.bitcast_convert_type((bits >> 16) >> 16, jnp.int32)


def _tie(x, zero):
    reps = (x.shape[0] // zero.shape[0], x.shape[1] // zero.shape[1])
    return jnp.where(jnp.tile(zero, reps) == 0, x, 0.0)


def _rms(x, g):
    return x * lax.rsqrt(jnp.mean(x * x, axis=-1, keepdims=True) + RMS_EPS) * g


def _sigmoid(x):
    return jax.nn.sigmoid(x)


def _head_sum(x, ebd):
    return _mm(x, ebd, NN, na=2, nb=1)


def _ada_body(c_ref, w_ref, b_ref, o_ref):
    c = c_ref[...]
    s = (c * _sigmoid(c)).astype(BF16)
    o_ref[...] = jnp.dot(s, w_ref[...].astype(BF16), preferred_element_type=F32) + b_ref[...]


def _ada_call(c_all, w_ada, b_ada):
    n = c_all.shape[0]
    width = w_ada.shape[1]
    return pl.pallas_call(
        _ada_body,
        grid=(width // D_MODEL,),
        in_specs=[pl.BlockSpec((n, D_MODEL), lambda i: (0, 0)),
                  pl.BlockSpec((D_MODEL, D_MODEL), lambda i: (0, i)),
                  pl.BlockSpec((1, D_MODEL), lambda i: (0, i))],
        out_specs=pl.BlockSpec((None, n, D_MODEL), lambda i: (i, 0, 0)),
        out_shape=jax.ShapeDtypeStruct((width // D_MODEL, n, D_MODEL), F32),
        compiler_params=pltpu.CompilerParams(dimension_semantics=("arbitrary",),
                                             vmem_limit_bytes=VMEM_LIMIT),
        name="ada",
    )(c_all, w_ada, b_ada.reshape(1, width))


def _const_spec(shape):
    nd = len(shape)
    return pl.BlockSpec(shape, lambda b, j: (0,) * nd, pipeline_mode=pl.Buffered(1))


def _tok_spec(tm, width, tiles):
    return pl.BlockSpec((tm, width), lambda b, j: (b * tiles + j, 0))


def _mod_spec(slab, rows, row0):
    return pl.BlockSpec((None, rows, D_MODEL), lambda b, j: (slab, row0 // rows, 0))


def _mod_rows(ref, tm, stride, seq_idx=None):
    if stride == 1:
        return ref[pl.ds(pl.program_id(0) if seq_idx is None else seq_idx, 1), :]
    return jnp.concatenate([ref[...]] * (tm // stride), axis=0)


def _params():
    return pltpu.CompilerParams(dimension_semantics=("arbitrary", "arbitrary"),
                                vmem_limit_bytes=VMEM_LIMIT)


def _ffn_tile(x, sh, sc, gt, gpre_ref, gpost_ref, wup_ref, wdn_ref, act_scr, res_w, between=lambda: None):
    h = _rms(x, gpre_ref[...]) * (1.0 + sc) + sh
    hb = h.astype(BF16)
    for j in range(D_FF // FF_CHUNK):
        lo = j * FF_CHUNK
        g = jnp.dot(hb, wup_ref[:, lo:lo + FF_CHUNK].astype(BF16), preferred_element_type=F32)
        u = jnp.dot(hb, wup_ref[:, D_FF + lo:D_FF + lo + FF_CHUNK].astype(BF16), preferred_element_type=F32)
        zero = between()
        if zero is not None:
            g = _tie(g, zero)
        act_scr[:, lo:lo + FF_CHUNK] = (g * _sigmoid(g) * u).astype(BF16)
    out = None
    for j in range(D_FF // FF_CHUNK):
        lo = j * FF_CHUNK
        part = jnp.dot(act_scr[:, lo:lo + FF_CHUNK], wdn_ref[lo:lo + FF_CHUNK, :].astype(BF16), preferred_element_type=F32)
        zero = between()
        if zero is not None:
            part = _tie(part, zero)
        out = part if out is None else out + part
    return x + res_w * gt * _rms(out, gpost_ref[...])


def _ffn_body(x_ref, sh_ref, sc_ref, gt_ref, gpre_ref, gpost_ref, wup_ref, wdn_ref, o_ref, act_scr, *, res_w, stride):
    x = x_ref[...]
    tm = x.shape[0]
    o_ref[...] = _ffn_tile(x, _mod_rows(sh_ref, tm, stride), _mod_rows(sc_ref, tm, stride), _mod_rows(gt_ref, tm, stride),
                           gpre_ref, gpost_ref, wup_ref, wdn_ref, act_scr, res_w)


def _ffn_call(x, ada, sub, gpre, gpost, wup, wdn, *, nb, tiles, tm, stride, mod_rows, res_w, name):
    return pl.pallas_call(
        functools.partial(_ffn_body, res_w=res_w, stride=stride),
        grid=(nb, tiles),
        in_specs=[_tok_spec(tm, D_MODEL, tiles),
                  _mod_spec(3 * sub, *mod_rows), _mod_spec(3 * sub + 1, *mod_rows), _mod_spec(3 * sub + 2, *mod_rows),
                  _const_spec((1, D_MODEL)), _const_spec((1, D_MODEL)),
                  _const_spec((D_MODEL, 2 * D_FF)), _const_spec((D_FF, D_MODEL))],
        out_specs=_tok_spec(tm, D_MODEL, tiles),
        out_shape=jax.ShapeDtypeStruct(x.shape, F32),
        scratch_shapes=[pltpu.VMEM((tm, D_FF), BF16)],
        compiler_params=_params(),
        name=name,
    )(x, ada, ada, ada, gpre, gpost, wup, wdn)


def _pre_body(x_ref, sh_ref, sc_ref, gpre_ref, win_ref, mu_ref, w0_ref, wd_ref, a0_ref, wi_ref, wg_ref,
              kk_ref, ka_ref, rk_ref, ebd_ref, shift0_ref,
              r_ref, k_ref, v_ref, a_ref, b_ref, ld_ref, gate_ref, u_ref, bonus_ref, nshift_ref,
              carry_scr, *, tm, stride, head_major):
    x = x_ref[...]
    h = _rms(x, gpre_ref[...]) * (1.0 + _mod_rows(sc_ref, tm, stride)) + _mod_rows(sh_ref, tm, stride)
    hb = h.astype(BF16)
    wb = win_ref[...].astype(BF16)
    rc = PRE_ROWS
    projs = [jnp.dot(hb[lo:lo + rc, :], wb, preferred_element_type=F32) for lo in range(0, tm, rc)]
    if stride == 1:
        @pl.when(pl.program_id(1) == 0)
        def _():
            carry_scr[...] = shift0_ref[...]
        first = carry_scr[...]
        row = lax.broadcasted_iota(jnp.int32, (rc, 1), 0)
    else:
        first = shift0_ref[...]
    ebd = ebd_ref[...]
    o1, o2, o3 = D_RWKV, 2 * D_RWKV, 3 * D_RWKV
    for ci, proj in enumerate(projs):
        rows = slice(ci * rc, (ci + 1) * rc)
        p_sh = proj[:, :N_SHIFT]
        p_cv = proj[:, N_SHIFT:]
        if stride == 1:
            prev = jnp.where(row == 0, first, pltpu.roll(p_sh, 1, axis=0))
        else:
            prev = jnp.concatenate([first, p_sh[:rc - stride, :]], axis=0)
        first = p_sh[rc - stride:rc, :]
        xs = p_sh + (prev - p_sh) * mu_ref[...]

        r, k, v = xs[:, :o1], xs[:, o1:o2], xs[:, o2:o3]
        dwda = xs[:, o3:o3 + DECAY_RANK + ICLR_RANK]
        dg = xs[:, o3 + DECAY_RANK + ICLR_RANK:]

        zdec = -(w0_ref[...] + jnp.dot(jnp.tanh(dwda).astype(BF16), wd_ref[...], preferred_element_type=F32))
        softplus = jnp.maximum(zdec, 0.0) + jnp.log1p(jnp.exp(-jnp.abs(zdec)))
        ld = -jnp.exp(-softplus - 0.5)
        iclr = _sigmoid(a0_ref[...] + jnp.dot(dwda.astype(BF16), wi_ref[...], preferred_element_type=F32))
        gate = jnp.dot(_sigmoid(dg).astype(BF16), wg_ref[...], preferred_element_type=F32)

        kk = k * kk_ref[...]
        kk = kk / jnp.maximum(jnp.sqrt(_head_sum(kk * kk, ebd)), 1e-12)
        kmod = k * (1.0 + (iclr - 1.0) * ka_ref[...])
        bonus = _head_sum(r * kmod * rk_ref[...], ebd) * v

        gate_ref[rows, :] = gate
        bonus_ref[rows, :] = bonus
        u_ref[rows, :] = p_cv[:, :D_CONV] * _sigmoid(p_cv[:, D_CONV:])
        outs = ((r_ref, r), (k_ref, kmod), (v_ref, v), (a_ref, -kk), (b_ref, kk * iclr), (ld_ref, ld))
        for ref, val in outs:
            if head_major:
                for pr in range(N_PAIRS):
                    ref[pr, rows, :] = val[:, pr * PAIR:(pr + 1) * PAIR]
            else:
                ref[rows, :] = val
    if stride == 1:
        carry_scr[...] = first
    nshift_ref[...] = first


def _pre_call(x1, ada, gpre, w, shift0, *, nb, tiles, tm, stride, mod_rows, head_major, name):
    n = x1.shape[0]
    if head_major:
        seq = tiles * tm
        hm_spec = pl.BlockSpec((None, N_PAIRS, tm, PAIR), lambda b, j: (b, 0, j, 0))
        hm_shape = jax.ShapeDtypeStruct((nb, N_PAIRS, seq, PAIR), F32)
    else:
        hm_spec = _tok_spec(tm, D_RWKV, tiles)
        hm_shape = jax.ShapeDtypeStruct((n, D_RWKV), F32)
    ld_spec = _tok_spec(tm, D_RWKV, tiles)
    ld_shape = jax.ShapeDtypeStruct((n, D_RWKV), F32)
    if stride == 1:
        shift_spec = pl.BlockSpec((None, 1, N_SHIFT), lambda b, j: (b, 0, 0))
        nshift_shape = jax.ShapeDtypeStruct((nb, 1, N_SHIFT), F32)
    else:
        shift_spec = pl.BlockSpec((stride, N_SHIFT), lambda b, j: (0, 0))
        nshift_shape = jax.ShapeDtypeStruct((stride, N_SHIFT), F32)
    return pl.pallas_call(
        functools.partial(_pre_body, tm=tm, stride=stride, head_major=head_major),
        grid=(nb, tiles),
        in_specs=[_tok_spec(tm, D_MODEL, tiles), _mod_spec(3, *mod_rows), _mod_spec(4, *mod_rows),
                  _const_spec((1, D_MODEL)), _const_spec((D_MODEL, D_IN)), _const_spec((1, N_SHIFT)),
                  _const_spec((1, D_RWKV)), _const_spec((DECAY_RANK + ICLR_RANK, D_RWKV)),
                  _const_spec((1, D_RWKV)), _const_spec((DECAY_RANK + ICLR_RANK, D_RWKV)),
                  _const_spec((GATE_RANK, D_RWKV)),
                  _const_spec((1, D_RWKV)), _const_spec((1, D_RWKV)), _const_spec((1, D_RWKV)),
                  _const_spec((D_RWKV, D_RWKV)), shift_spec],
        out_specs=[hm_spec] * 6 + [ld_spec] * 3 + [shift_spec],
        out_shape=[hm_shape] * 6 + [ld_shape] * 3 + [nshift_shape],
        scratch_shapes=[pltpu.VMEM((1, N_SHIFT), F32)],
        compiler_params=_params(),
        name=name,
    )(x1, ada, ada, gpre, w["w_in"], w["mu"], w["w0"], w["wd"], w["a0"], w["wi"], w["wg"],
      w["k_k"], w["k_a"], w["r_k"], w["ebd"], shift0)


def _wkv_body(r_ref, k_ref, v_ref, a_ref, b_ref, ld_ref, s0_ref, y_ref, st_ref, s_scr, *, chunk, n_sub, ways, span):
    c = pl.program_id(1)
    rows = WKV_ROWS
    pp = rows // (2 * chunk)
    shift = chunk.bit_length() - 1

    @pl.when(c == 0)
    def _():
        zero = jnp.zeros(s0_ref.shape[:1] + (HEAD, HEAD), F32)
        s_scr[:, :HEAD, :] = jnp.concatenate([s0_ref[:, 0], zero], axis=-1)
        s_scr[:, HEAD:, :] = jnp.concatenate([zero, s0_ref[:, 1]], axis=-1)

    ri = lax.broadcasted_iota(jnp.int32, (rows, rows), 0)
    ci = lax.broadcasted_iota(jnp.int32, (rows, rows), 1)
    same = jnp.right_shift(ri, shift) == jnp.right_shift(ci, shift)
    strict = same & (ci < ri)
    incl = same & (ci <= ri)
    eye = (ri == ci).astype(F32)
    first = lax.broadcasted_iota(jnp.int32, (1, 1, PAIR), 2) < HEAD
    tpos = lax.broadcasted_iota(jnp.int32, (pp * chunk, 1), 0) & (chunk - 1)

    def cumsum(x3):
        x = x3.reshape(pp * chunk, PAIR)
        for s in range(shift):
            d = 1 << s
            x = x + jnp.where(tpos >= d, pltpu.roll(x, d, axis=0), 0.0)
        return x.reshape(pp, chunk, PAIR)

    stack = lambda x3: jnp.concatenate([jnp.where(first, x3, 0.0), jnp.where(first, 0.0, x3)], axis=1)
    flat = lambda x3: x3.reshape(rows, PAIR)
    three = lambda x: x.reshape(pp, 2 * chunk, PAIR)
    each = lambda f, *xs: [f(*t) for t in zip(*xs)]

    def sub(it, carry, t0):
        sls = [pl.ds(pl.multiple_of((it * ways + q) * pp, pp), pp) for q in range(ways)]
        tw = slice(t0, t0 + chunk)
        r, k, v, a, b, ld = ([ref[sl, tw, :] for sl in sls] for ref in (r_ref, k_ref, v_ref, a_ref, b_ref, ld_ref))
        cs = each(cumsum, ld)
        e_inv = each(lambda x: jnp.exp(-x), cs)
        at = each(lambda x, c_, l_: stack(x * jnp.exp(c_ - l_)), a, cs, ld)
        rt = each(lambda x, c_: stack(x * jnp.exp(c_)), r, cs)
        bt = each(lambda x, e: stack(x * e), b, e_inv)
        kt = each(lambda x, e: stack(x * e), k, e_inv)
        vs = each(stack, v)
        gram = each(lambda p, q, m, n: _mm(jnp.concatenate([flat(p), flat(q)], axis=0),
                                           jnp.concatenate([flat(m), flat(n)], axis=0), NT, 1, 1), at, rt, bt, kt)
        t_ab = each(lambda g: jnp.where(strict, g[:rows, :rows], 0.0).astype(BF16), gram)
        t_ak = each(lambda g: jnp.where(strict, g[:rows, rows:], 0.0), gram)
        m_rb = each(lambda g: jnp.where(incl, g[rows:, :rows], 0.0), gram)
        m_rk = each(lambda g: jnp.where(incl, g[rows:, rows:], 0.0), gram)
        inv = each(lambda t: eye + t.astype(F32), t_ab)
        pw = t_ab
        for _ in range(shift - 1):
            pw = each(lambda p: _mm(p, p, NN, 1, 1), pw)
            inv = each(lambda x, p: x + _mm(p, x, NN, 1, 1), inv, pw)
        s0 = [s_scr[sl] for sl in sls]
        uy0 = each(lambda p, q, s: _mm(jnp.concatenate([p, q], axis=1), s, B_NT, 1, 1), at, rt, s0)
        z = each(lambda o, t, x: flat(o[:, :2 * chunk, :]) + _mm(t, flat(x), NN, 1, 1), uy0, t_ak, vs)
        u = each(lambda x, z_: _mm(x, z_, NN, 1, 1), inv, z)
        res = each(lambda z_, u_, t: z_ - u_ + _mm(t, u_, NN, 1, 2), z, u, t_ab)
        u = each(lambda u_, x, r_: u_ + _mm(x, r_, NN, 1, 1), u, inv, res)
        y = each(lambda o, m, u_, n, x: three(flat(o[:, 2 * chunk:, :]) + _mm(m, u_, NN, 1, 1)
                                              + _mm(n, flat(x), NN, 1, 1)), uy0, m_rb, u, m_rk, vs)
        for sl, y_ in zip(sls, y):
            y_ref[sl, tw, :] = y_[:, :chunk, :] + y_[:, chunk:, :]
        c_last = each(lambda x: x[:, chunk - 1:chunk, :], cs)
        tail = each(lambda cl, x: jnp.exp(cl - x), c_last, cs)
        upd = each(lambda u_, x, b_, k_, t: _mm(jnp.concatenate([three(u_), x], axis=1),
                                                jnp.concatenate([stack(b_ * t), stack(k_ * t)], axis=1), B_TN, 1, 1),
                   u, vs, b, k, tail)
        for sl, s, cl, d in zip(sls, s0, c_last, upd):
            s_scr[sl] = s * jnp.exp(cl) + d
        return carry

    for ci in range(span):
        lax.fori_loop(0, n_sub // ways, functools.partial(sub, t0=ci * chunk), 0)

    @pl.when(c == pl.num_programs(1) - 1)
    def _():
        st_ref[:, 0] = s_scr[:, :HEAD, :HEAD]
        st_ref[:, 1] = s_scr[:, HEAD:, HEAD:]


def _wkv_call(ops, s0, *, pblk, chunk, span, name):
    n_pairs, t_total, _ = ops[0].shape
    pp = WKV_ROWS // (2 * chunk)
    op_spec = pl.BlockSpec((pblk, span * chunk, PAIR), lambda g, c: (g, c, 0))
    st_spec = pl.BlockSpec((pblk, 2, HEAD, HEAD), lambda g, c: (g, 0, 0, 0))
    return pl.pallas_call(
        functools.partial(_wkv_body, chunk=chunk, n_sub=pblk // pp, ways=WKV_WAYS, span=span),
        grid=(n_pairs // pblk, t_total // (span * chunk)),
        in_specs=[op_spec] * 6 + [st_spec],
        out_specs=[op_spec, st_spec],
        out_shape=[jax.ShapeDtypeStruct((n_pairs, t_total, PAIR), F32),
                   jax.ShapeDtypeStruct((n_pairs, 2, HEAD, HEAD), F32)],
        scratch_shapes=[pltpu.VMEM((pblk, PAIR, PAIR), F32)],
        compiler_params=_params(),
        name=name,
    )(*ops, s0)


def _wkvseq_body(r_ref, k_ref, v_ref, a_ref, b_ref, ld_ref, s0_ref, y_ref, st_ref, t_scr, yt_scr, *, steps, nb):
    for qi, ref in enumerate((r_ref, k_ref, v_ref, a_ref, b_ref, ld_ref)):
        for t in range(steps):
            xt = ref[t * nb:(t + 1) * nb, :].T
            t_scr[qi, t] = jnp.exp(xt) if qi == 5 else xt
    for s in range(2):
        lo = s * HEAD

        def body(vi, carry):
            st = s0_ref[s, vi]
            for t in range(steps):
                col = lambda qi: t_scr[qi, t, lo:lo + HEAD, :]
                sa = jnp.sum(st * col(3), axis=0, keepdims=True)
                vv = t_scr[2, t, pl.ds(lo + vi, 1), :]
                st = st * col(5) + sa * col(4) + vv * col(1)
                yt_scr[t, pl.ds(lo + vi, 1), :] = jnp.sum(st * col(0), axis=0, keepdims=True)
            st_ref[s, vi] = st
            return carry

        lax.fori_loop(0, HEAD, body, 0, unroll=4)
    for t in range(steps):
        y_ref[t * nb:(t + 1) * nb, :] = yt_scr[t].T


def _wkvseq_call(ops, s0, *, steps, nb, name):
    n = steps * nb
    op_spec = pl.BlockSpec((n, PAIR), lambda p: (0, p))
    st_spec = pl.BlockSpec((2, HEAD, HEAD, nb), lambda p: (p, 0, 0, 0))
    return pl.pallas_call(
        functools.partial(_wkvseq_body, steps=steps, nb=nb),
        grid=(N_PAIRS,),
        in_specs=[op_spec] * 6 + [st_spec],
        out_specs=[op_spec, st_spec],
        out_shape=[jax.ShapeDtypeStruct((n, D_RWKV), F32),
                   jax.ShapeDtypeStruct((N_HEADS, HEAD, HEAD, nb), F32)],
        scratch_shapes=[pltpu.VMEM((6, steps, PAIR, nb), F32), pltpu.VMEM((steps, PAIR, nb), F32)],
        compiler_params=pltpu.CompilerParams(dimension_semantics=("arbitrary",),
                                             vmem_limit_bytes=VMEM_LIMIT),
        name=name,
    )(*ops, s0)


def _gn_gate(y_ref, bonus_ref, gate_ref, lng_ref, lnb_ref, ebd_ref, head_major):
    if head_major:
        y = jnp.concatenate([y_ref[pr] for pr in range(N_PAIRS)], axis=-1)
    else:
        y = y_ref[...]
    ebd = ebd_ref[...]
    mu = _head_sum(y, ebd) * (1.0 / HEAD)
    d = y - mu
    var = _head_sum(d * d, ebd) * (1.0 / HEAD)
    yn = d * lax.rsqrt(var + GN_EPS) * lng_ref[...] + lnb_ref[...]
    return (yn + bonus_ref[...]) * gate_ref[...]


def _conv_pieces(u_ref, full_scr, z_scr, cw_ref, tm, stride):
    hist = full_scr.shape[0] - tm
    full_scr[hist:, :] = u_ref[...]
    base = hist - (CONV_W - 1) * stride
    offs = [base + j * stride for j in range(CONV_W)]
    pieces = []
    if stride % 8 == 0:
        def whole():
            z = jnp.zeros((tm, D_CONV), F32)
            for j in range(CONV_W):
                z = z + full_scr[pl.ds(offs[j], tm), :] * cw_ref[j:j + 1, :]
            z_scr[...] = z
            return None
        pieces.append(whole)
    else:
        for lo in range(0, D_CONV, LANES):
            for t0 in range(0, tm, CONV_ROWS):
                def piece(lo=lo, t0=t0):
                    zc = None
                    for rsd in range(8):
                        n = CONV_ROWS + (8 if rsd else 0)
                        part = None
                        for j in [j for j in range(CONV_W) if offs[j] % 8 == rsd]:
                            term = (full_scr[pl.ds(t0 + offs[j] - rsd, n), lo:lo + LANES]
                                    * cw_ref[j:j + 1, lo:lo + LANES])
                            part = term if part is None else part + term
                        if rsd:
                            part = pltpu.roll(part, n - rsd, axis=0)[:CONV_ROWS, :]
                        zc = part if zc is None else zc + part
                    z_scr[t0:t0 + CONV_ROWS, lo:lo + LANES] = zc
                    return _zero_of(zc)
                pieces.append(piece)
    return pieces


def _conv_finish(u_ref, full_scr, z_scr, cb_ref, clg_ref, clb_ref, tm, stride):
    hist = full_scr.shape[0] - tm
    if stride == 1:
        full_scr[:hist, :] = u_ref[tm - hist:, :]
    z = z_scr[...] + cb_ref[...]
    zm = jnp.mean(z, axis=-1, keepdims=True)
    zc = z - zm
    zv = jnp.mean(zc * zc, axis=-1, keepdims=True)
    zn = zc * lax.rsqrt(zv + LN_EPS) * clg_ref[...] + clb_ref[...]
    return zn * _sigmoid(zn)


def _mix_out(y_rwkv, zn, wout_ref, x, gt, gpost_ref):
    mixed = (jnp.dot(y_rwkv.astype(BF16), wout_ref[:D_RWKV, :].astype(BF16), preferred_element_type=F32)
             + jnp.dot(zn.astype(BF16), wout_ref[D_RWKV:, :].astype(BF16), preferred_element_type=F32))
    return x + gt * _rms(mixed, gpost_ref[...])


def _post_body(y_ref, bonus_ref, gate_ref, u_ref, hist_ref, x_ref, gt_ref,
               lng_ref, lnb_ref, cw_ref, cb_ref, clg_ref, clb_ref, wout_ref, gpost_ref, ebd_ref,
               o_ref, full_scr, z_scr, *, tm, stride, head_major):
    hist = full_scr.shape[0] - tm
    if stride == 1:
        @pl.when(pl.program_id(1) == 0)
        def _():
            full_scr[:hist, :] = hist_ref[...]
    else:
        full_scr[:hist, :] = hist_ref[...]
    y_rwkv = _gn_gate(y_ref, bonus_ref, gate_ref, lng_ref, lnb_ref, ebd_ref, head_major)
    for piece in _conv_pieces(u_ref, full_scr, z_scr, cw_ref, tm, stride):
        piece()
    zn = _conv_finish(u_ref, full_scr, z_scr, cb_ref, clg_ref, clb_ref, tm, stride)
    o_ref[...] = _mix_out(y_rwkv, zn, wout_ref, x_ref[...], _mod_rows(gt_ref, tm, stride), gpost_ref)


def _postffn_body(y_ref, bonus_ref, gate_ref, u_ref, hist_ref, x_ref, gt_ref,
                  lng_ref, lnb_ref, cw_ref, cb_ref, clg_ref, clb_ref, wout_ref, gpost_ref, ebd_ref,
                  sh2_ref, sc2_ref, gt2_ref, gpre2_ref, gpost2_ref, wup_ref, wdn_ref,
                  o_ref, full_scr, z_scr, x2_scr, act_scr, *, tm, tiles, n_tiles):
    s = pl.program_id(0)
    t_mix = jnp.minimum(s, n_tiles - 1)
    b_mix = t_mix // tiles
    b_ffn = jnp.maximum(s - 1, 0) // tiles
    hist = full_scr.shape[0] - tm

    @pl.when(s == 0)
    def _():
        x2_scr[...] = jnp.zeros_like(x2_scr)

    @pl.when(t_mix % tiles == 0)
    def _():
        full_scr[:hist, :] = hist_ref[...]

    x2_prev = x2_scr[...]
    y_rwkv = _gn_gate(y_ref, bonus_ref, gate_ref, lng_ref, lnb_ref, ebd_ref, True)
    pieces = _conv_pieces(u_ref, full_scr, z_scr, cw_ref, tm, 1)

    def between():
        return pieces.pop(0)() if pieces else None

    o_ref[...] = _ffn_tile(x2_prev, _mod_rows(sh2_ref, tm, 1, b_ffn), _mod_rows(sc2_ref, tm, 1, b_ffn),
                           _mod_rows(gt2_ref, tm, 1, b_ffn), gpre2_ref, gpost2_ref, wup_ref, wdn_ref, act_scr, 0.5,
                           between)
    while pieces:
        between()
    zn = _conv_finish(u_ref, full_scr, z_scr, cb_ref, clg_ref, clb_ref, tm, 1)
    x2_scr[...] = _mix_out(y_rwkv, zn, wout_ref, x_ref[...], _mod_rows(gt_ref, tm, 1, b_mix), gpost_ref)


def _postffn_call(y, bonus, gate, u, hist, x1, ada, w, *, nb, tiles, tm, mod_rows, name):
    n_tiles = nb * tiles
    mix = lambda s: jnp.minimum(s, n_tiles - 1)
    ffn = lambda s: jnp.maximum(s - 1, 0)
    const = lambda shape: pl.BlockSpec(shape, lambda s: (0,) * len(shape), pipeline_mode=pl.Buffered(1))
    mod = lambda slab: pl.BlockSpec((None, mod_rows[0], D_MODEL), lambda s: (slab, mod_rows[1] // mod_rows[0], 0))
    tok = lambda width: pl.BlockSpec((tm, width), lambda s: (mix(s), 0))
    y_spec = pl.BlockSpec((None, N_PAIRS, tm, PAIR), lambda s: (mix(s) // tiles, 0, mix(s) % tiles, 0))
    hist_spec = pl.BlockSpec((None, HALO, D_CONV), lambda s: (mix(s) // tiles, 0, 0))
    row512 = const((1, D_RWKV))
    return pl.pallas_call(
        functools.partial(_postffn_body, tm=tm, tiles=tiles, n_tiles=n_tiles),
        grid=(n_tiles + 1,),
        in_specs=[y_spec, tok(D_RWKV), tok(D_RWKV), tok(D_RWKV), hist_spec, tok(D_MODEL), mod(5),
                  row512, row512, const((CONV_W, D_CONV)), row512, row512, row512,
                  const((D_MODEL, D_MODEL)), const((1, D_MODEL)), const((D_RWKV, D_RWKV)),
                  mod(6), mod(7), mod(8), const((1, D_MODEL)), const((1, D_MODEL)),
                  const((D_MODEL, 2 * D_FF)), const((D_FF, D_MODEL))],
        out_specs=pl.BlockSpec((tm, D_MODEL), lambda s: (ffn(s), 0)),
        out_shape=jax.ShapeDtypeStruct(x1.shape, F32),
        scratch_shapes=[pltpu.VMEM((HALO + tm, D_CONV), F32), pltpu.VMEM((tm, D_CONV), F32),
                        pltpu.VMEM((tm, D_MODEL), F32), pltpu.VMEM((tm, D_FF), BF16)],
        compiler_params=pltpu.CompilerParams(dimension_semantics=("arbitrary",), vmem_limit_bytes=VMEM_LIMIT),
        name=name,
    )(y, bonus, gate, u, hist, x1, ada, w["ln_x_g"], w["ln_x_b"], w["conv_w"], w["conv_b"],
      w["conv_ln_g"], w["conv_ln_b"], w["w_out"], w["g_post1"], w["ebd"],
      ada, ada, ada, w["g_pre2"], w["g_post2"], w["wup2b"], w["wdn2"])


def _post_call(y, bonus, gate, u, hist, x1, ada, w, *, nb, tiles, tm, stride, mod_rows, head_major, name):
    if head_major:
        y_spec = pl.BlockSpec((None, N_PAIRS, tm, PAIR), lambda b, j: (b, 0, j, 0))
        hist_spec = pl.BlockSpec((None, HALO, D_CONV), lambda b, j: (b, 0, 0))
        hist_rows = HALO
    else:
        y_spec = _tok_spec(tm, D_RWKV, tiles)
        hist_rows = hist.shape[0]
        hist_spec = pl.BlockSpec((hist_rows, D_CONV), lambda b, j: (0, 0))
    tok512 = _tok_spec(tm, D_RWKV, tiles)
    row512 = _const_spec((1, D_RWKV))
    return pl.pallas_call(
        functools.partial(_post_body, tm=tm, stride=stride, head_major=head_major),
        grid=(nb, tiles),
        in_specs=[y_spec, tok512, tok512, tok512, hist_spec, _tok_spec(tm, D_MODEL, tiles),
                  _mod_spec(5, *mod_rows),
                  row512, row512, _const_spec((CONV_W, D_CONV)), row512, row512, row512,
                  _const_spec((D_MODEL, D_MODEL)), _const_spec((1, D_MODEL)),
                  _const_spec((D_RWKV, D_RWKV))],
        out_specs=_tok_spec(tm, D_MODEL, tiles),
        out_shape=jax.ShapeDtypeStruct(x1.shape, F32),
        scratch_shapes=[pltpu.VMEM((hist_rows + tm, D_CONV), F32), pltpu.VMEM((tm, D_CONV), F32)],
        compiler_params=_params(),
        name=name,
    )(y, bonus, gate, u, hist, x1, ada, w["ln_x_g"], w["ln_x_b"], w["conv_w"], w["conv_b"],
      w["conv_ln_g"], w["conv_ln_b"], w["w_out"], w["g_post1"], w["ebd"])


def _layer(x, ada, w, shift0, hist, s0, *, nb, seq, tm, tm_post, stride, mod_rows, head_major, tag):
    tiles = (seq // tm) if head_major else 1
    tiles_post = (seq // tm_post) if head_major else 1
    x1 = _ffn_call(x, ada, 0, w["g_pre0"], w["g_post0"], w["wup1"], w["wdn1"], nb=nb, tiles=tiles, tm=tm,
                   stride=stride, mod_rows=mod_rows, res_w=0.5, name=tag + "_ffn1")
    r, k, v, a, b, ld, gate, u, bonus, nshift = _pre_call(
        x1, ada, w["g_pre1"], w, shift0, nb=nb, tiles=tiles, tm=tm, stride=stride, mod_rows=mod_rows,
        head_major=head_major, name=tag + "_mixpre")
    if head_major:
        ops = [z.reshape(nb * N_PAIRS, seq, PAIR) for z in (r, k, v, a, b, ld)]
        y, s_new = _wkv_call(ops, s0, pblk=nb * N_PAIRS, chunk=32, span=WKV_SPAN, name=tag + "_wkv")
        y = y.reshape(nb, N_PAIRS, seq, PAIR)
    else:
        y, s_new = _wkvseq_call((r, k, v, a, b, ld), s0, steps=x.shape[0] // stride, nb=stride, name=tag + "_wkv")
    if head_major:
        out = _postffn_call(y, bonus, gate, u, hist, x1, ada, w, nb=nb, tiles=tiles, tm=tm, mod_rows=mod_rows,
                            name=tag + "_mixpost_ffn2")
    else:
        x2 = _post_call(y, bonus, gate, u, hist, x1, ada, w, nb=nb, tiles=tiles_post, tm=tm_post, stride=stride,
                        mod_rows=mod_rows, head_major=head_major, name=tag + "_mixpost")
        out = _ffn_call(x2, ada, 2, w["g_pre2"], w["g_post2"], w["wup2"], w["wdn2"], nb=nb, tiles=tiles, tm=tm,
                        stride=stride, mod_rows=mod_rows, res_w=0.5, name=tag + "_ffn2")
    return out, s_new, nshift, u


def kernel(x_prompt, x_sample, c_prompt, c_sample, state_wkv, state_shift, state_conv, w_ada, b_ada, g_pre, g_post, w_ffn1_up, w_ffn1_down, w_in, mu_shift, w0, w_decay_up, a0, w_iclr_up, w_gate_up, k_k, k_a, r_k, ln_x_g, ln_x_b, conv_w, conv_b, conv_ln_g, conv_ln_b, w_out, w_ffn2_up, w_ffn2_down):
    bp, seq, _ = x_prompt.shape
    bs, dec, _ = x_sample.shape
    row = lambda z: z.reshape(1, -1).astype(F32)
    zeros64 = jnp.zeros((ICLR_RANK, D_RWKV), BF16)
    head_of_lane = jnp.arange(D_RWKV) // HEAD
    w = dict(
        wup1=w_ffn1_up, wdn1=w_ffn1_down, wup2=w_ffn2_up, wdn2=w_ffn2_down,
        wup2b=w_ffn2_up.astype(BF16),
        w_in=w_in, w_out=w_out,
        wd=jnp.concatenate([w_decay_up.astype(BF16), zeros64], axis=0),
        wi=jnp.concatenate([zeros64, w_iclr_up.astype(BF16)], axis=0),
        wg=w_gate_up.astype(BF16),
        mu=row(mu_shift), w0=row(w0), a0=row(a0), k_k=row(k_k), k_a=row(k_a), r_k=row(r_k),
        ln_x_g=row(ln_x_g), ln_x_b=row(ln_x_b), conv_w=conv_w, conv_b=row(conv_b),
        conv_ln_g=row(conv_ln_g), conv_ln_b=row(conv_ln_b),
        g_pre0=g_pre[0:1], g_pre1=g_pre[1:2], g_pre2=g_pre[2:3],
        g_post0=g_post[0:1], g_post1=g_post[1:2], g_post2=g_post[2:3],
        ebd=(head_of_lane[:, None] == head_of_lane[None, :]).astype(BF16),
    )

    assert bs % bp == 0
    ada = _ada_call(jnp.concatenate([c_sample, c_prompt], axis=0), w_ada, b_ada)

    yp, wkv_p, shift_p, u_p = _layer(
        x_prompt.reshape(bp * seq, D_MODEL), ada, w,
        jnp.zeros((bp, 1, N_SHIFT), F32), jnp.zeros((bp, HALO, D_CONV), F32),
        jnp.zeros((bp * N_PAIRS, 2, HEAD, HEAD), F32),
        nb=bp, seq=seq, tm=512, tm_post=1024, stride=1, mod_rows=(bp, bs), head_major=True, tag="p")
    y_p = yp.reshape(bp, seq, D_MODEL)
    wkv_p = wkv_p.reshape(bp, N_HEADS, HEAD, HEAD)
    conv_p = u_p.reshape(bp, seq, D_CONV)[:, seq - (CONV_W - 1):, :]

    xs_tok = x_sample.transpose(1, 0, 2).reshape(dec * bs, D_MODEL)
    hist_s = state_conv.transpose(1, 0, 2).reshape((CONV_W - 1) * bs, D_CONV)
    ys, wkv_s, shift_s, u_s = _layer(
        xs_tok, ada, w, state_shift.reshape(bs, N_SHIFT), hist_s,
        state_wkv.transpose(1, 2, 3, 0),
        nb=1, seq=dec * bs, tm=dec * bs, tm_post=dec * bs, stride=bs, mod_rows=(bs, 0), head_major=False, tag="s")
    y_s = ys.reshape(dec, bs, D_MODEL).transpose(1, 0, 2)
    wkv_s = wkv_s.transpose(3, 0, 1, 2)
    shift_s = shift_s.reshape(bs, 1, N_SHIFT)
    conv_s = jnp.concatenate([state_conv, u_s.reshape(dec, bs, D_CONV).transpose(1, 0, 2)],
                             axis=1)[:, dec:, :]
    return (y_p, y_s, wkv_p, shift_p, conv_p, wkv_s, shift_s, conv_s)
```

```python
import functools

import jax
import jax.numpy as jnp
from jax import lax
from jax.experimental import pallas as pl
from jax.experimental.pallas import tpu as pltpu

F32 = jnp.float32
BF16 = jnp.bfloat16

D_MODEL = 1024
D_RWKV = 512
HEAD = 64
LANES = 128
PAIR = 2 * HEAD
N_PAIRS = D_RWKV // PAIR
N_HEADS = D_RWKV // HEAD
D_CONV = D_MODEL - D_RWKV
CONV_W = 31
DECAY_RANK = 64
ICLR_RANK = 64
GATE_RANK = 128
D_FF = 2816
N_SHIFT = 3 * D_RWKV + DECAY_RANK + ICLR_RANK + GATE_RANK
D_IN = N_SHIFT + 2 * D_CONV
N_SUB = 3
RMS_EPS = 1e-6
LN_EPS = 1e-5
GN_EPS = 64e-5

FF_CHUNK = 256
WKV_ROWS = 256
PRE_ROWS = 256
CONV_ROWS = 128
WKV_WAYS = 8
WKV_SPAN = 2
HALO = 32
VMEM_LIMIT = 56 * 1024 * 1024

NN = (((1,), (0,)), ((), ()))
NT = (((1,), (1,)), ((), ()))
B_NT = (((2,), (2,)), ((0,), (0,)))
B_TN = (((1,), (1,)), ((0,), (0,)))


def _splits(x, n):
    if x.dtype == BF16:
        return [x]
    parts, rem = [], x
    for i in range(n):
        p = rem.astype(BF16)
        parts.append(p)
        if i + 1 < n:
            rem = rem - p.astype(F32)
    return parts


def _mm(a, b, dn, na=2, nb=2):
    ap, bp = _splits(a, na), _splits(b, nb)
    depth = max(len(ap), len(bp))
    out = None
    for i, x in enumerate(ap):
        for j, y in enumerate(bp):
            if i + j >= depth:
                continue
            t = lax.dot_general(x, y, dn, preferred_element_type=F32)
            out = t if out is None else out + t
    return out


def _zero_of(x):
    bits = lax.bitcast_convert_type(x, jnp.uint32)
    return lax.bitcast_convert_type((bits >> 16) >> 16, jnp.int32)


def _tie(x, zero):
    reps = (x.shape[0] // zero.shape[0], x.shape[1] // zero.shape[1])
    return jnp.where(jnp.tile(zero, reps) == 0, x, 0.0)


def _rms(x, g):
    return x * lax.rsqrt(jnp.mean(x * x, axis=-1, keepdims=True) + RMS_EPS) * g


def _sigmoid(x):
    return jax.nn.sigmoid(x)


def _head_sum(x, ebd):
    return _mm(x, ebd, NN, na=2, nb=1)


def _ada_body(c_ref, w_ref, b_ref, o_ref):
    c = c_ref[...]
    s = (c * _sigmoid(c)).astype(BF16)
    o_ref[...] = jnp.dot(s, w_ref[...].astype(BF16), preferred_element_type=F32) + b_ref[...]


def _ada_call(c_all, w_ada, b_ada):
    n = c_all.shape[0]
    width = w_ada.shape[1]
    return pl.pallas_call(
        _ada_body,
        grid=(width // D_MODEL,),
        in_specs=[pl.BlockSpec((n, D_MODEL), lambda i: (0, 0)),
                  pl.BlockSpec((D_MODEL, D_MODEL), lambda i: (0, i)),
                  pl.BlockSpec((1, D_MODEL), lambda i: (0, i))],
        out_specs=pl.BlockSpec((None, n, D_MODEL), lambda i: (i, 0, 0)),
        out_shape=jax.ShapeDtypeStruct((width // D_MODEL, n, D_MODEL), F32),
        compiler_params=pltpu.CompilerParams(dimension_semantics=("arbitrary",),
                                             vmem_limit_bytes=VMEM_LIMIT),
        name="ada",
    )(c_all, w_ada, b_ada.reshape(1, width))


def _const_spec(shape):
    nd = len(shape)
    return pl.BlockSpec(shape, lambda b, j: (0,) * nd, pipeline_mode=pl.Buffered(1))


def _tok_spec(tm, width, tiles):
    return pl.BlockSpec((tm, width), lambda b, j: (b * tiles + j, 0))


def _mod_spec(slab, rows, row0):
    return pl.BlockSpec((None, rows, D_MODEL), lambda b, j: (slab, row0 // rows, 0))


def _mod_rows(ref, tm, stride, seq_idx=None):
    if stride == 1:
        return ref[pl.ds(pl.program_id(0) if seq_idx is None else seq_idx, 1), :]
    return jnp.concatenate([ref[...]] * (tm // stride), axis=0)


def _params():
    return pltpu.CompilerParams(dimension_semantics=("arbitrary", "arbitrary"),
                                vmem_limit_bytes=VMEM_LIMIT)


def _ffn_tile(x, sh, sc, gt, gpre_ref, gpost_ref, wup_ref, wdn_ref, act_scr, res_w, between=lambda: None):
    h = _rms(x, gpre_ref[...]) * (1.0 + sc) + sh
    hb = h.astype(BF16)
    for j in range(D_FF // FF_CHUNK):
        lo = j * FF_CHUNK
        g = jnp.dot(hb, wup_ref[:, lo:lo + FF_CHUNK].astype(BF16), preferred_element_type=F32)
        u = jnp.dot(hb, wup_ref[:, D_FF + lo:D_FF + lo + FF_CHUNK].astype(BF16), preferred_element_type=F32)
        zero = between()
        if zero is not None:
            g = _tie(g, zero)
        act_scr[:, lo:lo + FF_CHUNK] = (g * _sigmoid(g) * u).astype(BF16)
    out = None
    for j in range(D_FF // FF_CHUNK):
        lo = j * FF_CHUNK
        part = jnp.dot(act_scr[:, lo:lo + FF_CHUNK], wdn_ref[lo:lo + FF_CHUNK, :].astype(BF16), preferred_element_type=F32)
        zero = between()
        if zero is not None:
            part = _tie(part, zero)
        out = part if out is None else out + part
    return x + res_w * gt * _rms(out, gpost_ref[...])


def _ffn_body(x_ref, sh_ref, sc_ref, gt_ref, gpre_ref, gpost_ref, wup_ref, wdn_ref, o_ref, act_scr, *, res_w, stride):
    x = x_ref[...]
    tm = x.shape[0]
    o_ref[...] = _ffn_tile(x, _mod_rows(sh_ref, tm, stride), _mod_rows(sc_ref, tm, stride), _mod_rows(gt_ref, tm, stride),
                           gpre_ref, gpost_ref, wup_ref, wdn_ref, act_scr, res_w)


def _ffn_call(x, ada, sub, gpre, gpost, wup, wdn, *, nb, tiles, tm, stride, mod_rows, res_w, name):
    return pl.pallas_call(
        functools.partial(_ffn_body, res_w=res_w, stride=stride),
        grid=(nb, tiles),
        in_specs=[_tok_spec(tm, D_MODEL, tiles),
                  _mod_spec(3 * sub, *mod_rows), _mod_spec(3 * sub + 1, *mod_rows), _mod_spec(3 * sub + 2, *mod_rows),
                  _const_spec((1, D_MODEL)), _const_spec((1, D_MODEL)),
                  _const_spec((D_MODEL, 2 * D_FF)), _const_spec((D_FF, D_MODEL))],
        out_specs=_tok_spec(tm, D_MODEL, tiles),
        out_shape=jax.ShapeDtypeStruct(x.shape, F32),
        scratch_shapes=[pltpu.VMEM((tm, D_FF), BF16)],
        compiler_params=_params(),
        name=name,
    )(x, ada, ada, ada, gpre, gpost, wup, wdn)


def _pre_body(x_ref, sh_ref, sc_ref, gpre_ref, win_ref, mu_ref, w0_ref, wd_ref, a0_ref, wi_ref, wg_ref,
              kk_ref, ka_ref, rk_ref, ebd_ref, shift0_ref,
              r_ref, k_ref, v_ref, a_ref, b_ref, ld_ref, gate_ref, u_ref, bonus_ref, nshift_ref,
              carry_scr, *, tm, stride, head_major):
    x = x_ref[...]
    h = _rms(x, gpre_ref[...]) * (1.0 + _mod_rows(sc_ref, tm, stride)) + _mod_rows(sh_ref, tm, stride)
    hb = h.astype(BF16)
    wb = win_ref[...].astype(BF16)
    rc = PRE_ROWS
    n_chunks = tm // rc
    projs = [jnp.dot(hb[:rc, :], wb, preferred_element_type=F32)] + [None] * (n_chunks - 1)
    cols = [[] for _ in range(n_chunks)]
    col_w = 2 * LANES

    def advance(ci, value=None):
        nxt = ci + 1
        if nxt >= n_chunks or len(cols[nxt]) * col_w >= D_IN:
            return
        lhs = hb[nxt * rc:(nxt + 1) * rc, :]
        if value is not None:
            lhs = _tie(lhs, _zero_of(value[:CONV_ROWS, :LANES]))
        c0 = len(cols[nxt]) * col_w
        cols[nxt].append(jnp.dot(lhs, wb[:, c0:c0 + col_w], preferred_element_type=F32))
    if stride == 1:
        @pl.when(pl.program_id(1) == 0)
        def _():
            carry_scr[...] = shift0_ref[...]
        first = carry_scr[...]
        row = lax.broadcasted_iota(jnp.int32, (rc, 1), 0)
    else:
        first = shift0_ref[...]
    ebd = ebd_ref[...]
    o1, o2, o3 = D_RWKV, 2 * D_RWKV, 3 * D_RWKV
    for ci in range(n_chunks):
        if projs[ci] is None:
            while len(cols[ci]) * col_w < D_IN:
                advance(ci - 1)
            projs[ci] = jnp.concatenate(cols[ci], axis=1)
        proj = projs[ci]
        rows = slice(ci * rc, (ci + 1) * rc)
        p_sh = proj[:, :N_SHIFT]
        p_cv = proj[:, N_SHIFT:]
        if stride == 1:
            prev = jnp.where(row == 0, first, pltpu.roll(p_sh, 1, axis=0))
        else:
            prev = jnp.concatenate([first, p_sh[:rc - stride, :]], axis=0)
        first = p_sh[rc - stride:rc, :]
        xs = p_sh + (prev - p_sh) * mu_ref[...]
        advance(ci, xs); advance(ci, xs[:, LANES:])

        r, k, v = xs[:, :o1], xs[:, o1:o2], xs[:, o2:o3]
        dwda = xs[:, o3:o3 + DECAY_RANK + ICLR_RANK]
        dg = xs[:, o3 + DECAY_RANK + ICLR_RANK:]

        zdec = -(w0_ref[...] + jnp.dot(jnp.tanh(dwda).astype(BF16), wd_ref[...], preferred_element_type=F32))
        softplus = jnp.maximum(zdec, 0.0) + jnp.log1p(jnp.exp(-jnp.abs(zdec)))
        ld = -jnp.exp(-softplus - 0.5)
        advance(ci, ld); advance(ci, ld[:, LANES:])
        iclr = _sigmoid(a0_ref[...] + jnp.dot(dwda.astype(BF16), wi_ref[...], preferred_element_type=F32))
        gate = jnp.dot(_sigmoid(dg).astype(BF16), wg_ref[...], preferred_element_type=F32)
        advance(ci, iclr); advance(ci, gate)

        kk = k * kk_ref[...]
        kk = kk / jnp.maximum(jnp.sqrt(_head_sum(kk * kk, ebd)), 1e-12)
        kmod = k * (1.0 + (iclr - 1.0) * ka_ref[...])
        advance(ci, kk); advance(ci, kmod)
        bonus = _head_sum(r * kmod * rk_ref[...], ebd) * v
        advance(ci, bonus); advance(ci, bonus[:, LANES:]); advance(ci, bonus[:, 2 * LANES:])

        gate_ref[rows, :] = gate
        bonus_ref[rows, :] = bonus
        u_ref[rows, :] = p_cv[:, :D_CONV] * _sigmoid(p_cv[:, D_CONV:])
        outs = ((r_ref, r), (k_ref, kmod), (v_ref, v), (a_ref, -kk), (b_ref, kk * iclr), (ld_ref, ld))
        for ref, val in outs:
            if head_major:
                for pr in range(N_PAIRS):
                    ref[pr, rows, :] = val[:, pr * PAIR:(pr + 1) * PAIR]
            else:
                ref[rows, :] = val
    if stride == 1:
        carry_scr[...] = first
    nshift_ref[...] = first


def _pre_call(x1, ada, gpre, w, shift0, *, nb, tiles, tm, stride, mod_rows, head_major, name):
    n = x1.shape[0]
    if head_major:
        seq = tiles * tm
        hm_spec = pl.BlockSpec((None, N_PAIRS, tm, PAIR), lambda b, j: (b, 0, j, 0))
        hm_shape = jax.ShapeDtypeStruct((nb, N_PAIRS, seq, PAIR), F32)
    else:
        hm_spec = _tok_spec(tm, D_RWKV, tiles)
        hm_shape = jax.ShapeDtypeStruct((n, D_RWKV), F32)
    ld_spec = _tok_spec(tm, D_RWKV, tiles)
    ld_shape = jax.ShapeDtypeStruct((n, D_RWKV), F32)
    if stride == 1:
        shift_spec = pl.BlockSpec((None, 1, N_SHIFT), lambda b, j: (b, 0, 0))
        nshift_shape = jax.ShapeDtypeStruct((nb, 1, N_SHIFT), F32)
    else:
        shift_spec = pl.BlockSpec((stride, N_SHIFT), lambda b, j: (0, 0))
        nshift_shape = jax.ShapeDtypeStruct((stride, N_SHIFT), F32)
    return pl.pallas_call(
        functools.partial(_pre_body, tm=tm, stride=stride, head_major=head_major),
        grid=(nb, tiles),
        in_specs=[_tok_spec(tm, D_MODEL, tiles), _mod_spec(3, *mod_rows), _mod_spec(4, *mod_rows),
                  _const_spec((1, D_MODEL)), _const_spec((D_MODEL, D_IN)), _const_spec((1, N_SHIFT)),
                  _const_spec((1, D_RWKV)), _const_spec((DECAY_RANK + ICLR_RANK, D_RWKV)),
                  _const_spec((1, D_RWKV)), _const_spec((DECAY_RANK + ICLR_RANK, D_RWKV)),
                  _const_spec((GATE_RANK, D_RWKV)),
                  _const_spec((1, D_RWKV)), _const_spec((1, D_RWKV)), _const_spec((1, D_RWKV)),
                  _const_spec((D_RWKV, D_RWKV)), shift_spec],
        out_specs=[hm_spec] * 6 + [ld_spec] * 3 + [shift_spec],
        out_shape=[hm_shape] * 6 + [ld_shape] * 3 + [nshift_shape],
        scratch_shapes=[pltpu.VMEM((1, N_SHIFT), F32)],
        compiler_params=_params(),
        name=name,
    )(x1, ada, ada, gpre, w["w_in"], w["mu"], w["w0"], w["wd"], w["a0"], w["wi"], w["wg"],
      w["k_k"], w["k_a"], w["r_k"], w["ebd"], shift0)


def _wkv_body(r_ref, k_ref, v_ref, a_ref, b_ref, ld_ref, s0_ref, y_ref, st_ref, s_scr, *, chunk, n_sub, ways, span):
    c = pl.program_id(1)
    rows = WKV_ROWS
    pp = rows // (2 * chunk)
    shift = chunk.bit_length() - 1

    @pl.when(c == 0)
    def _():
        zero = jnp.zeros(s0_ref.shape[:1] + (HEAD, HEAD), F32)
        s_scr[:, :HEAD, :] = jnp.concatenate([s0_ref[:, 0], zero], axis=-1)
        s_scr[:, HEAD:, :] = jnp.concatenate([zero, s0_ref[:, 1]], axis=-1)

    ri = lax.broadcasted_iota(jnp.int32, (rows, rows), 0)
    ci = lax.broadcasted_iota(jnp.int32, (rows, rows), 1)
    same = jnp.right_shift(ri, shift) == jnp.right_shift(ci, shift)
    strict = same & (ci < ri)
    incl = same & (ci <= ri)
    eye = (ri == ci).astype(F32)
    first = lax.broadcasted_iota(jnp.int32, (1, 1, PAIR), 2) < HEAD
    tpos = lax.broadcasted_iota(jnp.int32, (pp * chunk, 1), 0) & (chunk - 1)

    def cumsum(x3):
        x = x3.reshape(pp * chunk, PAIR)
        for s in range(shift):
            d = 1 << s
            x = x + jnp.where(tpos >= d, pltpu.roll(x, d, axis=0), 0.0)
        return x.reshape(pp, chunk, PAIR)

    stack = lambda x3: jnp.concatenate([jnp.where(first, x3, 0.0), jnp.where(first, 0.0, x3)], axis=1)
    flat = lambda x3: x3.reshape(rows, PAIR)
    three = lambda x: x.reshape(pp, 2 * chunk, PAIR)
    each = lambda f, *xs: [f(*t) for t in zip(*xs)]

    def sub(it, carry, t0):
        sls = [pl.ds(pl.multiple_of((it * ways + q) * pp, pp), pp) for q in range(ways)]
        tw = slice(t0, t0 + chunk)
        r, k, v, a, b, ld = ([ref[sl, tw, :] for sl in sls] for ref in (r_ref, k_ref, v_ref, a_ref, b_ref, ld_ref))
        cs = each(cumsum, ld)
        e_inv = each(lambda x: jnp.exp(-x), cs)
        at = each(lambda x, c_, l_: stack(x * jnp.exp(c_ - l_)), a, cs, ld)
        rt = each(lambda x, c_: stack(x * jnp.exp(c_)), r, cs)
        bt = each(lambda x, e: stack(x * e), b, e_inv)
        kt = each(lambda x, e: stack(x * e), k, e_inv)
        vs = each(stack, v)
        gram = each(lambda p, q, m, n: _mm(jnp.concatenate([flat(p), flat(q)], axis=0),
                                           jnp.concatenate([flat(m), flat(n)], axis=0), NT, 1, 1), at, rt, bt, kt)
        t_ab = each(lambda g: jnp.where(strict, g[:rows, :rows], 0.0).astype(BF16), gram)
        t_ak = each(lambda g: jnp.where(strict, g[:rows, rows:], 0.0), gram)
        m_rb = each(lambda g: jnp.where(incl, g[rows:, :rows], 0.0), gram)
        m_rk = each(lambda g: jnp.where(incl, g[rows:, rows:], 0.0), gram)
        inv = each(lambda t: eye + t.astype(F32), t_ab)
        pw = t_ab
        for _ in range(shift - 1):
            pw = each(lambda p: _mm(p, p, NN, 1, 1), pw)
            inv = each(lambda x, p: x + _mm(p, x, NN, 1, 1), inv, pw)
        s0 = [s_scr[sl] for sl in sls]
        uy0 = each(lambda p, q, s: _mm(jnp.concatenate([p, q], axis=1), s, B_NT, 1, 1), at, rt, s0)
        z = each(lambda o, t, x: flat(o[:, :2 * chunk, :]) + _mm(t, flat(x), NN, 1, 1), uy0, t_ak, vs)
        u = each(lambda x, z_: _mm(x, z_, NN, 1, 1), inv, z)
        res = each(lambda z_, u_, t: z_ - u_ + _mm(t, u_, NN, 1, 2), z, u, t_ab)
        u = each(lambda u_, x, r_: u_ + _mm(x, r_, NN, 1, 1), u, inv, res)
        y = each(lambda o, m, u_, n, x: three(flat(o[:, 2 * chunk:, :]) + _mm(m, u_, NN, 1, 1)
                                              + _mm(n, flat(x), NN, 1, 1)), uy0, m_rb, u, m_rk, vs)
        for sl, y_ in zip(sls, y):
            y_ref[sl, tw, :] = y_[:, :chunk, :] + y_[:, chunk:, :]
        c_last = each(lambda x: x[:, chunk - 1:chunk, :], cs)
        tail = each(lambda cl, x: jnp.exp(cl - x), c_last, cs)
        upd = each(lambda u_, x, b_, k_, t: _mm(jnp.concatenate([three(u_), x], axis=1),
                                                jnp.concatenate([stack(b_ * t), stack(k_ * t)], axis=1), B_TN, 1, 1),
                   u, vs, b, k, tail)
        for sl, s, cl, d in zip(sls, s0, c_last, upd):
            s_scr[sl] = s * jnp.exp(cl) + d
        return carry

    for ci in range(span):
        lax.fori_loop(0, n_sub // ways, functools.partial(sub, t0=ci * chunk), 0)

    @pl.when(c == pl.num_programs(1) - 1)
    def _():
        st_ref[:, 0] = s_scr[:, :HEAD, :HEAD]
        st_ref[:, 1] = s_scr[:, HEAD:, HEAD:]


def _wkv_call(ops, s0, *, pblk, chunk, span, name):
    n_pairs, t_total, _ = ops[0].shape
    pp = WKV_ROWS // (2 * chunk)
    op_spec = pl.BlockSpec((pblk, span * chunk, PAIR), lambda g, c: (g, c, 0))
    st_spec = pl.BlockSpec((pblk, 2, HEAD, HEAD), lambda g, c: (g, 0, 0, 0))
    return pl.pallas_call(
        functools.partial(_wkv_body, chunk=chunk, n_sub=pblk // pp, ways=WKV_WAYS, span=span),
        grid=(n_pairs // pblk, t_total // (span * chunk)),
        in_specs=[op_spec] * 6 + [st_spec],
        out_specs=[op_spec, st_spec],
        out_shape=[jax.ShapeDtypeStruct((n_pairs, t_total, PAIR), F32),
                   jax.ShapeDtypeStruct((n_pairs, 2, HEAD, HEAD), F32)],
        scratch_shapes=[pltpu.VMEM((pblk, PAIR, PAIR), F32)],
        compiler_params=_params(),
        name=name,
    )(*ops, s0)


def _wkvseq_body(r_ref, k_ref, v_ref, a_ref, b_ref, ld_ref, s0_ref, y_ref, st_ref, t_scr, yt_scr, *, steps, nb):
    for qi, ref in enumerate((r_ref, k_ref, v_ref, a_ref, b_ref, ld_ref)):
        for t in range(steps):
            xt = ref[t * nb:(t + 1) * nb, :].T
            t_scr[qi, t] = jnp.exp(xt) if qi == 5 else xt
    for s in range(2):
        lo = s * HEAD

        def body(vi, carry):
            st = s0_ref[s, vi]
            for t in range(steps):
                col = lambda qi: t_scr[qi, t, lo:lo + HEAD, :]
                sa = jnp.sum(st * col(3), axis=0, keepdims=True)
                vv = t_scr[2, t, pl.ds(lo + vi, 1), :]
                st = st * col(5) + sa * col(4) + vv * col(1)
                yt_scr[t, pl.ds(lo + vi, 1), :] = jnp.sum(st * col(0), axis=0, keepdims=True)
            st_ref[s, vi] = st
            return carry

        lax.fori_loop(0, HEAD, body, 0, unroll=4)
    for t in range(steps):
        y_ref[t * nb:(t + 1) * nb, :] = yt_scr[t].T


def _wkvseq_call(ops, s0, *, steps, nb, name):
    n = steps * nb
    op_spec = pl.BlockSpec((n, PAIR), lambda p: (0, p))
    st_spec = pl.BlockSpec((2, HEAD, HEAD, nb), lambda p: (p, 0, 0, 0))
    return pl.pallas_call(
        functools.partial(_wkvseq_body, steps=steps, nb=nb),
        grid=(N_PAIRS,),
        in_specs=[op_spec] * 6 + [st_spec],
        out_specs=[op_spec, st_spec],
        out_shape=[jax.ShapeDtypeStruct((n, D_RWKV), F32),
                   jax.ShapeDtypeStruct((N_HEADS, HEAD, HEAD, nb), F32)],
        scratch_shapes=[pltpu.VMEM((6, steps, PAIR, nb), F32), pltpu.VMEM((steps, PAIR, nb), F32)],
        compiler_params=pltpu.CompilerParams(dimension_semantics=("arbitrary",),
                                             vmem_limit_bytes=VMEM_LIMIT),
        name=name,
    )(*ops, s0)


def _gn_gate(y_ref, bonus_ref, gate_ref, lng_ref, lnb_ref, ebd_ref, head_major):
    if head_major:
        y = jnp.concatenate([y_ref[pr] for pr in range(N_PAIRS)], axis=-1)
    else:
        y = y_ref[...]
    ebd = ebd_ref[...]
    mu = _head_sum(y, ebd) * (1.0 / HEAD)
    d = y - mu
    var = _head_sum(d * d, ebd) * (1.0 / HEAD)
    yn = d * lax.rsqrt(var + GN_EPS) * lng_ref[...] + lnb_ref[...]
    return (yn + bonus_ref[...]) * gate_ref[...]


def _conv_pieces(u_ref, full_scr, z_scr, cw_ref, tm, stride):
    hist = full_scr.shape[0] - tm
    full_scr[hist:, :] = u_ref[...]
    base = hist - (CONV_W - 1) * stride
    offs = [base + j * stride for j in range(CONV_W)]
    pieces = []
    if stride % 8 == 0:
        def whole():
            z = jnp.zeros((tm, D_CONV), F32)
            for j in range(CONV_W):
                z = z + full_scr[pl.ds(offs[j], tm), :] * cw_ref[j:j + 1, :]
            z_scr[...] = z
            return None
        pieces.append(whole)
    else:
        for lo in range(0, D_CONV, LANES):
            for t0 in range(0, tm, CONV_ROWS):
                def piece(lo=lo, t0=t0):
                    zc = None
                    for rsd in range(8):
                        n = CONV_ROWS + (8 if rsd else 0)
                        part = None
                        for j in [j for j in range(CONV_W) if offs[j] % 8 == rsd]:
                            term = (full_scr[pl.ds(t0 + offs[j] - rsd, n), lo:lo + LANES]
                                    * cw_ref[j:j + 1, lo:lo + LANES])
                            part = term if part is None else part + term
                        if rsd:
                            part = pltpu.roll(part, n - rsd, axis=0)[:CONV_ROWS, :]
                        zc = part if zc is None else zc + part
                    z_scr[t0:t0 + CONV_ROWS, lo:lo + LANES] = zc
                    return _zero_of(zc)
                pieces.append(piece)
    return pieces


def _conv_finish(u_ref, full_scr, z_scr, cb_ref, clg_ref, clb_ref, tm, stride):
    hist = full_scr.shape[0] - tm
    if stride == 1:
        full_scr[:hist, :] = u_ref[tm - hist:, :]
    z = z_scr[...] + cb_ref[...]
    zm = jnp.mean(z, axis=-1, keepdims=True)
    zc = z - zm
    zv = jnp.mean(zc * zc, axis=-1, keepdims=True)
    zn = zc * lax.rsqrt(zv + LN_EPS) * clg_ref[...] + clb_ref[...]
    return zn * _sigmoid(zn)


def _mix_out(y_rwkv, zn, wout_ref, x, gt, gpost_ref):
    mixed = (jnp.dot(y_rwkv.astype(BF16), wout_ref[:D_RWKV, :].astype(BF16), preferred_element_type=F32)
             + jnp.dot(zn.astype(BF16), wout_ref[D_RWKV:, :].astype(BF16), preferred_element_type=F32))
    return x + gt * _rms(mixed, gpost_ref[...])


def _post_body(y_ref, bonus_ref, gate_ref, u_ref, hist_ref, x_ref, gt_ref,
               lng_ref, lnb_ref, cw_ref, cb_ref, clg_ref, clb_ref, wout_ref, gpost_ref, ebd_ref,
               o_ref, full_scr, z_scr, *, tm, stride, head_major):
    hist = full_scr.shape[0] - tm
    if stride == 1:
        @pl.when(pl.program_id(1) == 0)
        def _():
            full_scr[:hist, :] = hist_ref[...]
    else:
        full_scr[:hist, :] = hist_ref[...]
    y_rwkv = _gn_gate(y_ref, bonus_ref, gate_ref, lng_ref, lnb_ref, ebd_ref, head_major)
    for piece in _conv_pieces(u_ref, full_scr, z_scr, cw_ref, tm, stride):
        piece()
    zn = _conv_finish(u_ref, full_scr, z_scr, cb_ref, clg_ref, clb_ref, tm, stride)
    o_ref[...] = _mix_out(y_rwkv, zn, wout_ref, x_ref[...], _mod_rows(gt_ref, tm, stride), gpost_ref)


def _postffn_body(y_ref, bonus_ref, gate_ref, u_ref, hist_ref, x_ref, gt_ref,
                  lng_ref, lnb_ref, cw_ref, cb_ref, clg_ref, clb_ref, wout_ref, gpost_ref, ebd_ref,
                  sh2_ref, sc2_ref, gt2_ref, gpre2_ref, gpost2_ref, wup_ref, wdn_ref,
                  o_ref, full_scr, z_scr, x2_scr, act_scr, *, tm, tiles, n_tiles):
    s = pl.program_id(0)
    t_mix = jnp.minimum(s, n_tiles - 1)
    b_mix = t_mix // tiles
    b_ffn = jnp.maximum(s - 1, 0) // tiles
    hist = full_scr.shape[0] - tm

    @pl.when(s == 0)
    def _():
        x2_scr[...] = jnp.zeros_like(x2_scr)

    @pl.when(t_mix % tiles == 0)
    def _():
        full_scr[:hist, :] = hist_ref[...]

    x2_prev = x2_scr[...]
    y_rwkv = _gn_gate(y_ref, bonus_ref, gate_ref, lng_ref, lnb_ref, ebd_ref, True)
    pieces = _conv_pieces(u_ref, full_scr, z_scr, cw_ref, tm, 1)

    def between():
        return pieces.pop(0)() if pieces else None

    o_ref[...] = _ffn_tile(x2_prev, _mod_rows(sh2_ref, tm, 1, b_ffn), _mod_rows(sc2_ref, tm, 1, b_ffn),
                           _mod_rows(gt2_ref, tm, 1, b_ffn), gpre2_ref, gpost2_ref, wup_ref, wdn_ref, act_scr, 0.5,
                           between)
    while pieces:
        between()
    zn = _conv_finish(u_ref, full_scr, z_scr, cb_ref, clg_ref, clb_ref, tm, 1)
    x2_scr[...] = _mix_out(y_rwkv, zn, wout_ref, x_ref[...], _mod_rows(gt_ref, tm, 1, b_mix), gpost_ref)


def _postffn_call(y, bonus, gate, u, hist, x1, ada, w, *, nb, tiles, tm, mod_rows, name):
    n_tiles = nb * tiles
    mix = lambda s: jnp.minimum(s, n_tiles - 1)
    ffn = lambda s: jnp.maximum(s - 1, 0)
    const = lambda shape: pl.BlockSpec(shape, lambda s: (0,) * len(shape), pipeline_mode=pl.Buffered(1))
    mod = lambda slab: pl.BlockSpec((None, mod_rows[0], D_MODEL), lambda s: (slab, mod_rows[1] // mod_rows[0], 0))
    tok = lambda width: pl.BlockSpec((tm, width), lambda s: (mix(s), 0))
    y_spec = pl.BlockSpec((None, N_PAIRS, tm, PAIR), lambda s: (mix(s) // tiles, 0, mix(s) % tiles, 0))
    hist_spec = pl.BlockSpec((None, HALO, D_CONV), lambda s: (mix(s) // tiles, 0, 0))
    row512 = const((1, D_RWKV))
    return pl.pallas_call(
        functools.partial(_postffn_body, tm=tm, tiles=tiles, n_tiles=n_tiles),
        grid=(n_tiles + 1,),
        in_specs=[y_spec, tok(D_RWKV), tok(D_RWKV), tok(D_RWKV), hist_spec, tok(D_MODEL), mod(5),
                  row512, row512, const((CONV_W, D_CONV)), row512, row512, row512,
                  const((D_MODEL, D_MODEL)), const((1, D_MODEL)), const((D_RWKV, D_RWKV)),
                  mod(6), mod(7), mod(8), const((1, D_MODEL)), const((1, D_MODEL)),
                  const((D_MODEL, 2 * D_FF)), const((D_FF, D_MODEL))],
        out_specs=pl.BlockSpec((tm, D_MODEL), lambda s: (ffn(s), 0)),
        out_shape=jax.ShapeDtypeStruct(x1.shape, F32),
        scratch_shapes=[pltpu.VMEM((HALO + tm, D_CONV), F32), pltpu.VMEM((tm, D_CONV), F32),
                        pltpu.VMEM((tm, D_MODEL), F32), pltpu.VMEM((tm, D_FF), BF16)],
        compiler_params=pltpu.CompilerParams(dimension_semantics=("arbitrary",), vmem_limit_bytes=VMEM_LIMIT),
        name=name,
    )(y, bonus, gate, u, hist, x1, ada, w["ln_x_g"], w["ln_x_b"], w["conv_w"], w["conv_b"],
      w["conv_ln_g"], w["conv_ln_b"], w["w_out"], w["g_post1"], w["ebd"],
      ada, ada, ada, w["g_pre2"], w["g_post2"], w["wup2b"], w["wdn2"])


def _post_call(y, bonus, gate, u, hist, x1, ada, w, *, nb, tiles, tm, stride, mod_rows, head_major, name):
    if head_major:
        y_spec = pl.BlockSpec((None, N_PAIRS, tm, PAIR), lambda b, j: (b, 0, j, 0))
        hist_spec = pl.BlockSpec((None, HALO, D_CONV), lambda b, j: (b, 0, 0))
        hist_rows = HALO
    else:
        y_spec = _tok_spec(tm, D_RWKV, tiles)
        hist_rows = hist.shape[0]
        hist_spec = pl.BlockSpec((hist_rows, D_CONV), lambda b, j: (0, 0))
    tok512 = _tok_spec(tm, D_RWKV, tiles)
    row512 = _const_spec((1, D_RWKV))
    return pl.pallas_call(
        functools.partial(_post_body, tm=tm, stride=stride, head_major=head_major),
        grid=(nb, tiles),
        in_specs=[y_spec, tok512, tok512, tok512, hist_spec, _tok_spec(tm, D_MODEL, tiles),
                  _mod_spec(5, *mod_rows),
                  row512, row512, _const_spec((CONV_W, D_CONV)), row512, row512, row512,
                  _const_spec((D_MODEL, D_MODEL)), _const_spec((1, D_MODEL)),
                  _const_spec((D_RWKV, D_RWKV))],
        out_specs=_tok_spec(tm, D_MODEL, tiles),
        out_shape=jax.ShapeDtypeStruct(x1.shape, F32),
        scratch_shapes=[pltpu.VMEM((hist_rows + tm, D_CONV), F32), pltpu.VMEM((tm, D_CONV), F32)],
        compiler_params=_params(),
        name=name,
    )(y, bonus, gate, u, hist, x1, ada, w["ln_x_g"], w["ln_x_b"], w["conv_w"], w["conv_b"],
      w["conv_ln_g"], w["conv_ln_b"], w["w_out"], w["g_post1"], w["ebd"])


def _layer(x, ada, w, shift0, hist, s0, *, nb, seq, tm, tm_post, stride, mod_rows, head_major, tag):
    tiles = (seq // tm) if head_major else 1
    tiles_post = (seq // tm_post) if head_major else 1
    x1 = _ffn_call(x, ada, 0, w["g_pre0"], w["g_post0"], w["wup1"], w["wdn1"], nb=nb, tiles=tiles, tm=tm,
                   stride=stride, mod_rows=mod_rows, res_w=0.5, name=tag + "_ffn1")
    r, k, v, a, b, ld, gate, u, bonus, nshift = _pre_call(
        x1, ada, w["g_pre1"], w, shift0, nb=nb, tiles=tiles, tm=tm, stride=stride, mod_rows=mod_rows,
        head_major=head_major, name=tag + "_mixpre")
    if head_major:
        ops = [z.reshape(nb * N_PAIRS, seq, PAIR) for z in (r, k, v, a, b, ld)]
        y, s_new = _wkv_call(ops, s0, pblk=nb * N_PAIRS, chunk=32, span=WKV_SPAN, name=tag + "_wkv")
        y = y.reshape(nb, N_PAIRS, seq, PAIR)
    else:
        y, s_new = _wkvseq_call((r, k, v, a, b, ld), s0, steps=x.shape[0] // stride, nb=stride, name=tag + "_wkv")
    if head_major:
        out = _postffn_call(y, bonus, gate, u, hist, x1, ada, w, nb=nb, tiles=tiles, tm=tm, mod_rows=mod_rows,
                            name=tag + "_mixpost_ffn2")
    else:
        x2 = _post_call(y, bonus, gate, u, hist, x1, ada, w, nb=nb, tiles=tiles_post, tm=tm_post, stride=stride,
                        mod_rows=mod_rows, head_major=head_major, name=tag + "_mixpost")
        out = _ffn_call(x2, ada, 2, w["g_pre2"], w["g_post2"], w["wup2"], w["wdn2"], nb=nb, tiles=tiles, tm=tm,
                        stride=stride, mod_rows=mod_rows, res_w=0.5, name=tag + "_ffn2")
    return out, s_new, nshift, u


def kernel(x_prompt, x_sample, c_prompt, c_sample, state_wkv, state_shift, state_conv, w_ada, b_ada, g_pre, g_post, w_ffn1_up, w_ffn1_down, w_in, mu_shift, w0, w_decay_up, a0, w_iclr_up, w_gate_up, k_k, k_a, r_k, ln_x_g, ln_x_b, conv_w, conv_b, conv_ln_g, conv_ln_b, w_out, w_ffn2_up, w_ffn2_down):
    bp, seq, _ = x_prompt.shape
    bs, dec, _ = x_sample.shape
    row = lambda z: z.reshape(1, -1).astype(F32)
    zeros64 = jnp.zeros((ICLR_RANK, D_RWKV), BF16)
    head_of_lane = jnp.arange(D_RWKV) // HEAD
    w = dict(
        wup1=w_ffn1_up, wdn1=w_ffn1_down, wup2=w_ffn2_up, wdn2=w_ffn2_down,
        wup2b=w_ffn2_up.astype(BF16),
        w_in=w_in, w_out=w_out,
        wd=jnp.concatenate([w_decay_up.astype(BF16), zeros64], axis=0),
        wi=jnp.concatenate([zeros64, w_iclr_up.astype(BF16)], axis=0),
        wg=w_gate_up.astype(BF16),
        mu=row(mu_shift), w0=row(w0), a0=row(a0), k_k=row(k_k), k_a=row(k_a), r_k=row(r_k),
        ln_x_g=row(ln_x_g), ln_x_b=row(ln_x_b), conv_w=conv_w, conv_b=row(conv_b),
        conv_ln_g=row(conv_ln_g), conv_ln_b=row(conv_ln_b),
        g_pre0=g_pre[0:1], g_pre1=g_pre[1:2], g_pre2=g_pre[2:3],
        g_post0=g_post[0:1], g_post1=g_post[1:2], g_post2=g_post[2:3],
        ebd=(head_of_lane[:, None] == head_of_lane[None, :]).astype(BF16),
    )

    assert bs % bp == 0
    ada = _ada_call(jnp.concatenate([c_sample, c_prompt], axis=0), w_ada, b_ada)

    yp, wkv_p, shift_p, u_p = _layer(
        x_prompt.reshape(bp * seq, D_MODEL), ada, w,
        jnp.zeros((bp, 1, N_SHIFT), F32), jnp.zeros((bp, HALO, D_CONV), F32),
        jnp.zeros((bp * N_PAIRS, 2, HEAD, HEAD), F32),
        nb=bp, seq=seq, tm=512, tm_post=1024, stride=1, mod_rows=(bp, bs), head_major=True, tag="p")
    y_p = yp.reshape(bp, seq, D_MODEL)
    wkv_p = wkv_p.reshape(bp, N_HEADS, HEAD, HEAD)
    conv_p = u_p.reshape(bp, seq, D_CONV)[:, seq - (CONV_W - 1):, :]

    xs_tok = x_sample.transpose(1, 0, 2).reshape(dec * bs, D_MODEL)
    hist_s = state_conv.transpose(1, 0, 2).reshape((CONV_W - 1) * bs, D_CONV)
    ys, wkv_s, shift_s, u_s = _layer(
        xs_tok, ada, w, state_shift.reshape(bs, N_SHIFT), hist_s,
        state_wkv.transpose(1, 2, 3, 0),
        nb=1, seq=dec * bs, tm=dec * bs, tm_post=dec * bs, stride=bs, mod_rows=(bs, 0), head_major=False, tag="s")
    y_s = ys.reshape(dec, bs, D_MODEL).transpose(1, 0, 2)
    wkv_s = wkv_s.transpose(3, 0, 1, 2)
    shift_s = shift_s.reshape(bs, 1, N_SHIFT)
    conv_s = jnp.concatenate([state_conv, u_s.reshape(dec, bs, D_CONV).transpose(1, 0, 2)],
                             axis=1)[:, dec:, :]
    return (y_p, y_s, wkv_p, shift_p, conv_p, wkv_s, shift_s, conv_s)
```
